```python
import jax, jax.numpy as jnp
from jax import lax
import numpy as np

D_MODEL = 4096
BATCH = 4
SEQ = 2048
DEPTH = 4
DEC_BATCH = 8
DEC_SEQ = 8
PAST_LEN = 8192
PAGE_SIZE = 128

RW_HEADS = 24
RW_HEAD = 64
RW_WIDTH = RW_HEADS * RW_HEAD
DECAY_LORA = 64
AAA_LORA = 64
GATE_LORA = 128
RW_COLS = 3 * RW_WIDTH + DECAY_LORA + AAA_LORA + GATE_LORA
GN_EPS = 64e-5
GM_CHUNK = 128
GM_GROUPS = 12
GM_GROUP_DIM = 128
GM_WIDTH = GM_GROUPS * GM_GROUP_DIM
LN_EPS = 1e-5
DIL_PAIRS = ((128, 1), (512, 4), (2048, 16))
N_DIL = 3
ATT_GROUP_HEADS = 4
ATT_HEAD = 128
ATT_WIDTH = N_DIL * ATT_GROUP_HEADS * ATT_HEAD
ATT_OUT = ATT_GROUP_HEADS * ATT_HEAD
ROPE_THETA = 10000.0
N_BRANCH = 3
D_FF = 4 * D_MODEL
IN_COLS = RW_COLS + 2 * GM_WIDTH + 3 * ATT_WIDTH + N_BRANCH * D_MODEL
NORM_EPS = 1e-6
NEG_INF = -1e30
F32 = jnp.float32

kernel_name = 'hybrid_rwkv7_gmlp_dilated_attn_step'


def rmsnorm(x, g):
    xf = x.astype(F32)
    y = xf * lax.rsqrt(jnp.mean(xf * xf, -1, keepdims=True) + NORM_EPS)
    return (y * g.astype(F32)).astype(x.dtype)


def rope(x, pos):
    half = ATT_HEAD // 2
    inv = ROPE_THETA ** (-jnp.arange(half, dtype=F32) / half)
    ang = pos.astype(F32)[:, None] * inv[None, :]
    shp = (1, pos.shape[0]) + (1,) * (x.ndim - 3) + (half,)
    cos, sin = jnp.cos(ang).reshape(shp), jnp.sin(ang).reshape(shp)
    xf = x.astype(F32)
    x1, x2 = xf[..., :half], xf[..., half:]
    return jnp.concatenate([x1 * cos - x2 * sin, x2 * cos + x1 * sin], -1).astype(x.dtype)


def rwkv_scan(r, w, k, v, kk, a, s0):
    def step(s, inp):
        r_t, w_t, k_t, v_t, kk_t, a_t = inp
        sa = jnp.einsum('bhvk,bhk->bhv', s, -kk_t)
        s = (s * w_t[:, :, None, :] + sa[..., None] * (kk_t * a_t)[:, :, None, :]
             + v_t[..., None] * k_t[:, :, None, :])
        return s, jnp.einsum('bhvk,bhk->bhv', s, r_t)
    xs = tuple(jnp.moveaxis(t, 1, 0) for t in (r, w, k, v, kk, a))
    s_fin, ys = lax.scan(step, s0, xs)
    return jnp.moveaxis(ys, 0, 1), s_fin


def rwkv_mixer(p, p_prev_row, s0, mu, w0, w_up, a0, a_up, g_up, k_k, k_a, r_k, ln_w, ln_b):
    B, T, _ = p.shape
    p = p.astype(F32)
    p_prev = jnp.concatenate([p_prev_row.astype(F32)[:, None], p[:, :-1]], axis=1)
    ps = p + mu * (p_prev - p)
    cuts = [RW_WIDTH, 2 * RW_WIDTH, 3 * RW_WIDTH, 3 * RW_WIDTH + DECAY_LORA, 3 * RW_WIDTH + DECAY_LORA + AAA_LORA]
    r, k, v, w_in, a_in, g_in = jnp.split(ps, cuts, axis=-1)
    w_log = -jax.nn.softplus(-(w0 + jnp.tanh(w_in) @ w_up)) - 0.5
    decay = jnp.exp(-jnp.exp(w_log))
    a = jax.nn.sigmoid(a0 + a_in @ a_up)
    g = jax.nn.sigmoid(g_in) @ g_up
    hd = lambda t: t.reshape(B, T, RW_HEADS, RW_HEAD)
    r, k, v, decay, a = hd(r), hd(k), hd(v), hd(decay), hd(a)
    kk = k * k_k
    kk = kk * lax.rsqrt(jnp.maximum(jnp.sum(kk * kk, -1, keepdims=True), 1e-24))
    k = k * (1.0 + (a - 1.0) * k_a)
    y, s_fin = rwkv_scan(r, decay, k, v, kk, a, s0.astype(F32))
    mean = jnp.mean(y, -1, keepdims=True)
    var = jnp.mean(jnp.square(y - mean), -1, keepdims=True)
    y = (y - mean) * lax.rsqrt(var + GN_EPS) * ln_w + ln_b
    y = y + jnp.sum(r * k * r_k, -1, keepdims=True) * v
    y = y.reshape(B, T, RW_WIDTH) * g
    return y, s_fin, p[:, -1]


def gmlp_mixer(u, v, ln_g, ln_b, w_s, b_s):
    B, T, _ = u.shape
    u = jax.nn.gelu(u, approximate=False)
    vf = jax.nn.gelu(v, approximate=False).astype(F32)
    mean = jnp.mean(vf, -1, keepdims=True)
    var = jnp.mean(jnp.square(vf - mean), -1, keepdims=True)
    vn = ((vf - mean) * lax.rsqrt(var + LN_EPS) * ln_g + ln_b).astype(u.dtype)
    Tp = -(-T // GM_CHUNK) * GM_CHUNK
    vp = jnp.pad(vn, ((0, 0), (0, Tp - T), (0, 0))).reshape(B, Tp // GM_CHUNK, GM_CHUNK, GM_GROUPS, GM_GROUP_DIM)
    causal = jnp.tril(jnp.ones((GM_CHUNK, GM_CHUNK), bool))
    wm = jnp.where(causal[None], w_s, jnp.zeros((), w_s.dtype))
    mixed = jnp.einsum('gts,bnsgc->bntgc', wm, vp) + b_s.T[None, None, :, :, None]
    mixed = mixed.reshape(B, Tp, GM_WIDTH)[:, :T]
    return u * mixed.astype(u.dtype), vn


def dilated_attn_prompt(q, k, v, window, dil):
    B, S, H, D = q.shape
    L = S // dil
    blk = window // dil
    Lp = -(-L // blk) * blk
    nb = Lp // blk
    def streams(t):
        t = t.reshape(B, L, dil, H, D).transpose(0, 2, 1, 3, 4)
        t = jnp.pad(t, ((0, 0), (0, 0), (0, Lp - L), (0, 0), (0, 0)))
        return t.reshape(B, dil, nb, blk, H, D)
    def with_prev(t):
        prev = jnp.pad(t, ((0, 0), (0, 0), (1, 0), (0, 0), (0, 0), (0, 0)))[:, :, :-1]
        return jnp.concatenate([prev, t], axis=3)
    qs = streams(q)
    kb, vb = with_prev(streams(k)), with_prev(streams(v))
    qi = jnp.arange(blk)[:, None]
    kj = jnp.arange(2 * blk)[None, :]
    band = (kj >= qi) & (kj <= qi + blk)
    mask = jnp.where((jnp.arange(nb) == 0)[:, None, None], band & (kj >= blk), band)
    s = jnp.einsum('bcnqhd,bcnkhd->bcnhqk', qs, kb).astype(F32) * (D ** -0.5)
    s = jnp.where(mask[None, None, :, None], s, NEG_INF)
    lse = jax.nn.logsumexp(s, axis=-1)
    pr = jnp.exp(s - lse[..., None])
    o = jnp.einsum('bcnhqk,bcnkhd->bcnqhd', pr.astype(vb.dtype), vb)
    o = o.reshape(B, dil, Lp, H, D)[:, :, :L].transpose(0, 2, 1, 3, 4).reshape(B, S, H, D)
    lse = lse.transpose(0, 1, 2, 4, 3).reshape(B, dil, Lp, H)[:, :, :L].transpose(0, 2, 1, 3).reshape(B, S, H)
    return o, lse


def dilated_attn_sample(q, k_new, v_new, k_buf, v_buf, window, dil):
    B, T, H, D = q.shape
    Wb = k_buf.shape[1]
    keys = jnp.concatenate([k_buf, k_new], axis=1)
    vals = jnp.concatenate([v_buf, v_new], axis=1)
    n_k = window // dil + 1
    idx = Wb + jnp.arange(T)[:, None] - dil * jnp.arange(n_k)[None, :]
    valid = idx >= 0
    idx = jnp.maximum(idx, 0)
    kg, vg = keys[:, idx], vals[:, idx]
    s = jnp.einsum('bthd,btkhd->bthk', q, kg).astype(F32) * (D ** -0.5)
    s = jnp.where(valid[None, :, None, :], s, NEG_INF)
    lse = jax.nn.logsumexp(s, axis=-1)
    pr = jnp.exp(s - lse[..., None])
    o = jnp.einsum('bthk,btkhd->bthd', pr.astype(vg.dtype), vg)
    return o, lse


def dilated_attention(p_at, pos, kv_bufs, q_gain, k_gain):
    B, T, _ = p_at.shape
    q, k, v = jnp.split(p_at, 3, axis=-1)
    shp = (B, T, N_DIL, ATT_GROUP_HEADS, ATT_HEAD)
    q = rope(rmsnorm(q.reshape(shp), q_gain), pos)
    k = rope(rmsnorm(k.reshape(shp), k_gain), pos)
    v = v.reshape(shp)
    outs, lses, rows = [], [], []
    for g, (window, dil) in enumerate(DIL_PAIRS):
        qg, kg, vg = q[:, :, g], k[:, :, g], v[:, :, g]
        if kv_bufs is None:
            o, lse = dilated_attn_prompt(qg, kg, vg, window, dil)
            keep = min(window, T)
            rows.append(jnp.stack([kg[:, T - keep:], vg[:, T - keep:]], axis=2))
        else:
            buf = kv_bufs[g]
            o, lse = dilated_attn_sample(qg, kg, vg, buf[:, :, 0], buf[:, :, 1], window, dil)
            rows.append(jnp.stack([kg, vg], axis=2))
        outs.append(o)
        lses.append(lse)
    wts = jax.nn.softmax(jnp.stack(lses, 0), axis=0)
    o = jnp.sum(wts[..., None] * jnp.stack(outs, 0).astype(F32), axis=0)
    return o.astype(p_at.dtype).reshape(B, T, ATT_OUT), rows


def trunk_layer(x, pos, rw_s0, rw_prev, kv_bufs, norm1, w_in, rw_mu, rw_w0, rw_w_up, rw_a0, rw_a_up,
                rw_g_up, rw_k_k, rw_k_a, rw_r_k, rw_ln_w, rw_ln_b, gm_ln_g, gm_ln_b, gm_ws, gm_bs,
                att_q_gain, att_k_gain, w_br_rwkv, w_br_gmlp, w_br_attn, w_out, norm2, w_ff1, w_ff2):
    B, T, _ = x.shape
    h = rmsnorm(x, norm1)
    proj = h @ w_in
    c1 = RW_COLS
    c2 = c1 + 2 * GM_WIDTH
    c3 = c2 + 3 * ATT_WIDTH
    p_rw, p_gm, p_at, p_gate = jnp.split(proj, [c1, c2, c3], axis=-1)
    y_rw, rw_sT, rw_last = rwkv_mixer(p_rw, rw_prev, rw_s0, rw_mu, rw_w0, rw_w_up, rw_a0, rw_a_up, rw_g_up,
                                      rw_k_k, rw_k_a, rw_r_k, rw_ln_w, rw_ln_b)
    y_gm, gm_v = gmlp_mixer(p_gm[..., :GM_WIDTH], p_gm[..., GM_WIDTH:], gm_ln_g, gm_ln_b, gm_ws, gm_bs)
    y_at, kv_rows = dilated_attention(p_at, pos, kv_bufs, att_q_gain, att_k_gain)
    gates = jax.nn.sigmoid(p_gate.astype(F32)).reshape(B, T, N_BRANCH, D_MODEL)
    merged = (gates[:, :, 0] * (y_rw.astype(x.dtype) @ w_br_rwkv)
              + gates[:, :, 1] * (y_gm @ w_br_gmlp)
              + gates[:, :, 2] * (y_at @ w_br_attn))
    x = x + merged.astype(x.dtype) @ w_out
    h2 = rmsnorm(x, norm2)
    x = x + jnp.square(jax.nn.relu(h2 @ w_ff1)) @ w_ff2
    return x, rw_sT, rw_last, gm_v, kv_rows


def setup_inputs(seed: int = 0) -> dict:
    key = jax.random.key(seed)
    ks = iter(jax.random.split(key, 48))
    def nrm(shape, scale):
        return scale * jax.random.normal(next(ks), shape, F32)
    H, N = RW_HEADS, RW_HEAD
    kv_shape = lambda w: (DEPTH, DEC_BATCH, min(w, PAST_LEN), 2, ATT_GROUP_HEADS, ATT_HEAD)
    return {
        'x_prompt': nrm((BATCH, SEQ, D_MODEL), 1.0),
        'x_sample': nrm((DEC_BATCH, DEC_SEQ, D_MODEL), 1.0),
        'cache_kv_w128': nrm(kv_shape(DIL_PAIRS[0][0]), 1.0),
        'cache_kv_w512': nrm(kv_shape(DIL_PAIRS[1][0]), 1.0),
        'cache_kv_w2048': nrm(kv_shape(DIL_PAIRS[2][0]), 1.0),
        'state_rwkv': nrm((DEPTH, DEC_BATCH, H, N, N), 0.5),
        'state_rwkv_shift': nrm((DEPTH, DEC_BATCH, RW_COLS), 1.0),
        'norm1': 1.0 + nrm((DEPTH, D_MODEL), 0.02),
        'w_in': nrm((DEPTH, D_MODEL, IN_COLS), D_MODEL ** -0.5),
        'rw_mu': jax.random.uniform(next(ks), (DEPTH, RW_COLS), F32, 0.05, 0.95),
        'rw_w0': nrm((DEPTH, RW_WIDTH), 0.5),
        'rw_w_up': nrm((DEPTH, DECAY_LORA, RW_WIDTH), 0.1 * DECAY_LORA ** -0.5),
        'rw_a0': nrm((DEPTH, RW_WIDTH), 0.5),
        'rw_a_up': nrm((DEPTH, AAA_LORA, RW_WIDTH), 0.5 * AAA_LORA ** -0.5),
        'rw_g_up': nrm((DEPTH, GATE_LORA, RW_WIDTH), GATE_LORA ** -0.5),
        'rw_k_k': 0.85 + nrm((DEPTH, H, N), 0.05),
        'rw_k_a': 1.0 + nrm((DEPTH, H, N), 0.05),
        'rw_r_k': nrm((DEPTH, H, N), 0.1),
        'rw_ln_w': 1.0 + nrm((DEPTH, H, N), 0.02),
        'rw_ln_b': nrm((DEPTH, H, N), 0.02),
        'gm_ln_g': 1.0 + nrm((DEPTH, GM_WIDTH), 0.02),
        'gm_ln_b': nrm((DEPTH, GM_WIDTH), 0.02),
        'gm_ws': nrm((DEPTH, GM_GROUPS, GM_CHUNK, GM_CHUNK), GM_CHUNK ** -0.5),
        'gm_bs': 1.0 + nrm((DEPTH, GM_GROUPS, GM_CHUNK), 0.1),
        'att_q_gain': 1.0 + nrm((DEPTH, ATT_HEAD), 0.02),
        'att_k_gain': 1.0 + nrm((DEPTH, ATT_HEAD), 0.02),
        'w_br_rwkv': nrm((DEPTH, RW_WIDTH, D_MODEL), RW_WIDTH ** -0.5),
        'w_br_gmlp': nrm((DEPTH, GM_WIDTH, D_MODEL), GM_WIDTH ** -0.5),
        'w_br_attn': nrm((DEPTH, ATT_OUT, D_MODEL), ATT_OUT ** -0.5),
        'w_out': nrm((DEPTH, D_MODEL, D_MODEL), D_MODEL ** -0.5),
        'norm2': 1.0 + nrm((DEPTH, D_MODEL), 0.02),
        'w_ff1': nrm((DEPTH, D_MODEL, D_FF), D_MODEL ** -0.5),
        'w_ff2': nrm((DEPTH, D_FF, D_MODEL), (1.5 * D_FF) ** -0.5),
    }


def reference(x_prompt, x_sample, cache_kv_w128, cache_kv_w512, cache_kv_w2048, state_rwkv, state_rwkv_shift,
              norm1, w_in, rw_mu, rw_w0, rw_w_up, rw_a0, rw_a_up, rw_g_up, rw_k_k, rw_k_a, rw_r_k, rw_ln_w,
              rw_ln_b, gm_ln_g, gm_ln_b, gm_ws, gm_bs, att_q_gain, att_k_gain, w_br_rwkv, w_br_gmlp, w_br_attn,
              w_out, norm2, w_ff1, w_ff2):
    Bp, Sp, _ = x_prompt.shape
    Bs, Ss, _ = x_sample.shape
    pos_p = jnp.arange(Sp)
    pos_s = PAST_LEN + jnp.arange(Ss)
    xp, xs = x_prompt, x_sample
    kvp = ([], [], [])
    kvs = ([], [], [])
    st_p, st_s, sh_p, sh_s, gmv_s = [], [], [], [], []
    for l in range(DEPTH):
        params = (norm1[l], w_in[l], rw_mu[l], rw_w0[l], rw_w_up[l], rw_a0[l], rw_a_up[l], rw_g_up[l],
                  rw_k_k[l], rw_k_a[l], rw_r_k[l], rw_ln_w[l], rw_ln_b[l], gm_ln_g[l], gm_ln_b[l], gm_ws[l],
                  gm_bs[l], att_q_gain[l], att_k_gain[l], w_br_rwkv[l], w_br_gmlp[l], w_br_attn[l], w_out[l],
                  norm2[l], w_ff1[l], w_ff2[l])
        xp, s_p, last_p, _, rows_p = trunk_layer(
            xp, pos_p, jnp.zeros((Bp, RW_HEADS, RW_HEAD, RW_HEAD), F32), jnp.zeros((Bp, RW_COLS), F32),
            None, *params)
        xs, s_s, last_s, gv_s, rows_s = trunk_layer(
            xs, pos_s, state_rwkv[l], state_rwkv_shift[l],
            (cache_kv_w128[l], cache_kv_w512[l], cache_kv_w2048[l]), *params)
        for g in range(N_DIL):
            kvp[g].append(rows_p[g])
            kvs[g].append(rows_s[g])
        st_p.append(s_p)
        st_s.append(s_s)
        sh_p.append(last_p)
        sh_s.append(last_s)
        gmv_s.append(gv_s)
    y_prompt, y_sample = xp, xs
    kv_w128_prompt = jnp.stack(kvp[0], 0)
    kv_w512_prompt = jnp.stack(kvp[1], 0)
    kv_w2048_prompt = jnp.stack(kvp[2], 0)
    kv_w128_sample = jnp.stack(kvs[0], 0)
    kv_w512_sample = jnp.stack(kvs[1], 0)
    kv_w2048_sample = jnp.stack(kvs[2], 0)
    rwkv_state_prompt = jnp.stack(st_p, 0)
    rwkv_state_sample = jnp.stack(st_s, 0)
    rwkv_shift_prompt = jnp.stack(sh_p, 0)
    rwkv_shift_sample = jnp.stack(sh_s, 0)
    gmlp_v_sample = jnp.stack(gmv_s, 0)
    return (y_prompt, y_sample, kv_w128_prompt, kv_w512_prompt, kv_w2048_prompt, kv_w128_sample, kv_w512_sample,
            kv_w2048_sample, rwkv_state_prompt, rwkv_state_sample, rwkv_shift_prompt, rwkv_shift_sample,
            gmlp_v_sample)
```

```python
import functools

import jax
import jax.numpy as jnp
from jax import lax
from jax.experimental import pallas as pl
from jax.experimental.pallas import tpu as pltpu

F32 = jnp.float32
BF16 = jnp.bfloat16

LANES = 128
V7X_VMEM_LIMIT = 56 * 1024 * 1024

D_MODEL = 4096
RW_HEADS = 24
RW_HEAD = 64
RW_WIDTH = RW_HEADS * RW_HEAD
DECAY_LORA = 64
AAA_LORA = 64
GATE_LORA = 128
RW_COLS = 3 * RW_WIDTH + DECAY_LORA + AAA_LORA + GATE_LORA
GN_EPS = 64e-5
GM_CHUNK = 128
GM_GROUPS = 12
GM_GROUP_DIM = 128
GM_WIDTH = GM_GROUPS * GM_GROUP_DIM
LN_EPS = 1e-5
DIL_PAIRS = ((128, 1), (512, 4), (2048, 16))
N_DIL = 3
ATT_GROUP_HEADS = 4
ATT_HEAD = 128
ATT_WIDTH = N_DIL * ATT_GROUP_HEADS * ATT_HEAD
ATT_OUT = ATT_GROUP_HEADS * ATT_HEAD
ATT_BLK = 128
ROPE_THETA = 10000.0
N_BRANCH = 3
D_FF = 4 * D_MODEL
NORM_EPS = 1e-6
NEG_INF = -1e30
PAST_LEN = 8192

ROW_TILE = 688

HPG = 4
GW = HPG * RW_HEAD
NG = RW_HEADS // HPG
SCAN_C = 64

NN = (((1,), (0,)), ((), ()))
NT = (((1,), (1,)), ((), ()))


def _dot(a, b, dims=NN):
    return lax.dot_general(a, b, dims, preferred_element_type=F32)


def _split2(x):
    hi = x.astype(BF16)
    lo = (x - hi.astype(F32)).astype(BF16)
    return hi, lo


def _dot3(a, b, dims=NN):
    ah, al = _split2(a)
    bh, bl = _split2(b)
    return _dot(ah, bh, dims) + (_dot(ah, bl, dims) + _dot(al, bh, dims))


def _dot_exact_rhs(a, e):
    hi = a.astype(BF16)
    r1 = a - hi.astype(F32)
    mid = r1.astype(BF16)
    lo = (r1 - mid.astype(F32)).astype(BF16)
    return _dot(hi, e) + (_dot(mid, e) + _dot(lo, e))


def _params(sem, vmem=None):
    return pltpu.CompilerParams(dimension_semantics=sem, vmem_limit_bytes=vmem)


def _sigmoid(x):
    return 1.0 / (1.0 + jnp.exp(-x))


def _mm_kernel(*refs, nk, epilogue):
    if epilogue == "residual":
        a_ref, b_ref, res_ref, o_ref = refs[:4]
        rest = refs[4:]
    else:
        a_ref, b_ref, o_ref = refs[:3]
        res_ref = None
        rest = refs[3:]
    part = _dot(a_ref[...], b_ref[...].astype(BF16))

    def finish(acc):
        if epilogue == "relu2":
            acc = jnp.square(jnp.maximum(acc, 0.0))
        elif epilogue == "residual":
            acc = acc + res_ref[...]
        o_ref[...] = acc.astype(o_ref.dtype)

    if nk == 1:
        finish(part)
    else:
        acc_ref = rest[0]
        k = pl.program_id(2)

        @pl.when(k == 0)
        def _():
            acc_ref[...] = part

        @pl.when(k > 0)
        def _():
            acc_ref[...] += part

        @pl.when(k == nk - 1)
        def _():
            finish(acc_ref[...])


def _matmul(a, b, *, col_off=0, n_cols=None, tm, tn, tk, epilogue="none", res=None, out_dtype=F32):
    M, K = a.shape
    n_cols = b.shape[1] if n_cols is None else n_cols
    assert M % tm == 0 and n_cols % tn == 0 and K % tk == 0 and col_off % tn == 0
    nk = K // tk
    off = col_off // tn
    in_specs = [pl.BlockSpec((tm, tk), lambda i, j, k: (i, k)),
                pl.BlockSpec((tk, tn), lambda i, j, k: (k, j + off))]
    args = [a, b]
    if epilogue == "residual":
        in_specs.append(pl.BlockSpec((tm, tn), lambda i, j, k: (i, j)))
        args.append(res)
    scratch = [pltpu.VMEM((tm, tn), F32)] if nk > 1 else []
    return pl.pallas_call(
        functools.partial(_mm_kernel, nk=nk, epilogue=epilogue),
        out_shape=jax.ShapeDtypeStruct((M, n_cols), out_dtype),
        grid=(M // tm, n_cols // tn, nk),
        in_specs=in_specs,
        out_specs=pl.BlockSpec((tm, tn), lambda i, j, k: (i, j)),
        scratch_shapes=scratch,
        compiler_params=_params(("parallel", "parallel", "arbitrary"), V7X_VMEM_LIMIT),
    )(*args)


def _rmsnorm_kernel(x_ref, g_ref, o_ref):
    x = x_ref[...]
    y = x * lax.rsqrt(jnp.mean(x * x, axis=-1, keepdims=True) + NORM_EPS)
    o_ref[...] = (y * g_ref[...]).astype(o_ref.dtype)


def _rmsnorm(x, g, tm):
    M, D = x.shape
    return pl.pallas_call(
        _rmsnorm_kernel,
        out_shape=jax.ShapeDtypeStruct((M, D), BF16),
        grid=(M // tm,),
        in_specs=[pl.BlockSpec((tm, D), lambda i: (i, 0)), pl.BlockSpec((1, D), lambda i: (0, 0))],
        out_specs=pl.BlockSpec((tm, D), lambda i: (i, 0)),
        compiler_params=_params(("parallel",), V7X_VMEM_LIMIT),
    )(x, g.reshape(1, D))


def _head_ones(width, head):
    r = lax.broadcasted_iota(jnp.int32, (width, width), 0) // head
    c = lax.broadcasted_iota(jnp.int32, (width, width), 1) // head
    return jnp.where(r == c, 1.0, 0.0).astype(BF16)


def _rwkv_prep_kernel(p_ref, prev_ref, mu_ref, w0_ref, wup_ref, a0_ref, aup_ref, gup_ref, kk_ref, ka_ref,
                      r_o, lw_o, k_o, v_o, kkn_o, b_o, g_o, carry_ref, *, tm):
    j = pl.program_id(1)

    @pl.when(j == 0)
    def _():
        carry_ref[...] = prev_ref[0]

    p = p_ref[...]
    row = lax.broadcasted_iota(jnp.int32, p.shape, 0)
    p_prev = jnp.where(row == 0, carry_ref[...], pltpu.roll(p, 1, axis=0))
    carry_ref[...] = p[tm - 1:tm, :]
    ps = p + mu_ref[...] * (p_prev - p)

    W = RW_WIDTH
    r = ps[:, 0:W]
    k = ps[:, W:2 * W]
    v = ps[:, 2 * W:3 * W]
    wa = ps[:, 3 * W:3 * W + LANES]
    g_in = ps[:, 3 * W + LANES:3 * W + 2 * LANES]

    z = -(w0_ref[...] + _dot(jnp.tanh(wa).astype(BF16), wup_ref[...].astype(BF16)))
    softplus = jnp.maximum(z, 0.0) + jnp.log1p(jnp.exp(-jnp.abs(z)))
    lw = -jnp.exp(-softplus - 0.5)
    a = _sigmoid(a0_ref[...] + _dot(wa.astype(BF16), aup_ref[...].astype(BF16)))
    g = _dot(_sigmoid(g_in).astype(BF16), gup_ref[...].astype(BF16))

    kk = k * kk_ref[...]
    ones = _head_ones(GW, RW_HEAD)
    kmod = k * (1.0 + (a - 1.0) * ka_ref[...])
    for q in range(NG):
        sl = slice(q * GW, (q + 1) * GW)
        kq = kk[:, sl]
        ssq = _dot_exact_rhs(kq * kq, ones)
        kn = kq * lax.rsqrt(jnp.maximum(ssq, 1e-24))
        r_o[0, q] = r[:, sl]
        lw_o[0, q] = lw[:, sl]
        k_o[0, q] = kmod[:, sl]
        v_o[0, q] = v[:, sl]
        kkn_o[0, q] = kn
        b_o[0, q] = kn * a[:, sl]
        g_o[0, q] = g[:, sl]


def _rwkv_prep(p_rw, row_off, B, T, tm, prev, mu, w0, wup_pad, a0, aup_pad, gup, k_k, k_a):
    nt = T // tm
    off = row_off // tm
    assert row_off % tm == 0 and T % tm == 0
    row = lambda n: pl.BlockSpec((1, n), lambda b, j: (0, 0))
    out_spec = pl.BlockSpec((1, NG, tm, GW), lambda b, j: (b, 0, j, 0))
    out = jax.ShapeDtypeStruct((B, NG, T, GW), F32)
    return pl.pallas_call(
        functools.partial(_rwkv_prep_kernel, tm=tm),
        out_shape=[out] * 7,
        grid=(B, nt),
        in_specs=[pl.BlockSpec((tm, RW_COLS), lambda b, j: (off + b * nt + j, 0)),
                  pl.BlockSpec((1, 1, RW_COLS), lambda b, j: (b, 0, 0)),
                  row(RW_COLS), row(RW_WIDTH),
                  pl.BlockSpec((LANES, RW_WIDTH), lambda b, j: (0, 0)),
                  row(RW_WIDTH),
                  pl.BlockSpec((LANES, RW_WIDTH), lambda b, j: (0, 0)),
                  pl.BlockSpec((GATE_LORA, RW_WIDTH), lambda b, j: (0, 0)),
                  row(RW_WIDTH), row(RW_WIDTH)],
        out_specs=[out_spec] * 7,
        scratch_shapes=[pltpu.VMEM((1, RW_COLS), F32)],
        compiler_params=_params(("parallel", "arbitrary"), V7X_VMEM_LIMIT),
    )(p_rw, prev.reshape(B, 1, RW_COLS), mu.reshape(1, -1), w0.reshape(1, -1), wup_pad, a0.reshape(1, -1),
      aup_pad, gup, k_k.reshape(1, -1), k_a.reshape(1, -1))


def _stack_heads(x, lane_head):
    return jnp.concatenate([jnp.where(lane_head == h, x, 0.0) for h in range(HPG)], axis=0)


def _tril_inverse(n, ii, jj, size):
    eye = jnp.where(ii == jj, 1.0, 0.0)
    t = eye + jnp.where((ii >> 1) == (jj >> 1), n, 0.0)
    s = 2
    while s < size:
        sh = s.bit_length() - 1
        off = jnp.where(((ii >> (sh + 1)) == (jj >> (sh + 1))) & ((ii >> sh) != (jj >> sh)), n, 0.0)
        t = t + _dot3(t, _dot3(off, t))
        s *= 2
    return t


def _rwkv_scan_kernel(r_ref, lw_ref, k_ref, v_ref, kk_ref, b_ref, g_ref, s0_ref, lnw_ref, lnb_ref, rk_ref,
                      y_ref, st_ref, s_scr, *, C):
    j = pl.program_id(1)

    @pl.when(j == 0)
    def _():
        s_scr[...] = s0_ref[0]

    R = HPG * C
    lane_head = lax.broadcasted_iota(jnp.int32, (1, GW), 1) // RW_HEAD
    ii = lax.broadcasted_iota(jnp.int32, (R, R), 0)
    jj = lax.broadcasted_iota(jnp.int32, (R, R), 1)
    ti = lax.broadcasted_iota(jnp.int32, (C, C), 0)
    tj = lax.broadcasted_iota(jnp.int32, (C, C), 1)
    tril = jnp.where(ti >= tj, 1.0, 0.0).astype(BF16)
    ones = _head_ones(GW, RW_HEAD)
    strict = ii > jj
    incl = ii >= jj

    for q in range(NG):
        r = r_ref[0, q]
        lw = lw_ref[0, q]
        k = k_ref[0, q]
        v = v_ref[0, q]
        kk = kk_ref[0, q]
        b = b_ref[0, q]
        cum = _dot_exact_rhs_left(tril, lw)
        pc = jnp.exp(cum)
        pinv = jnp.exp(-cum)
        pend = pc[C - 1:C, :]
        ar = jnp.concatenate([_stack_heads(-kk * jnp.exp(cum - lw), lane_head),
                              _stack_heads(r * pc, lane_head)], axis=0)
        bk = jnp.concatenate([_stack_heads(b * pinv, lane_head),
                              _stack_heads(k * pinv, lane_head)], axis=0)
        gram = _dot3(ar, bk, NT)
        a_ab = jnp.where(strict, gram[:R, :R], 0.0)
        a_ak = jnp.where(strict, gram[:R, R:], 0.0)
        a_rb = jnp.where(incl, gram[R:, :R], 0.0)
        a_rk = jnp.where(incl, gram[R:, R:], 0.0)

        s = s_scr[q]
        v_st = _stack_heads(v, lane_head)
        from_state = _dot3(ar, s, NT)
        from_v = _dot3(jnp.concatenate([a_ak, a_rk], axis=0), v_st)
        t_inv = _tril_inverse(a_ab, ii, jj, C)
        u = _dot3(t_inv, from_state[:R] + from_v[:R])
        y_st = from_state[R:] + from_v[R:] + _dot3(a_rb, u)
        y = y_st[0:C]
        for h in range(1, HPG):
            y = y + y_st[h * C:(h + 1) * C]
        uv = jnp.concatenate([u, v_st], axis=0)
        s_scr[q] = s * pend + _dot3(uv.T, bk * pend)

        mean = _dot_exact_rhs(y, ones) * (1.0 / RW_HEAD)
        yc = y - mean
        var = _dot_exact_rhs(yc * yc, ones) * (1.0 / RW_HEAD)
        sl = slice(q * GW, (q + 1) * GW)
        yn = yc * lax.rsqrt(var + GN_EPS) * lnw_ref[:, sl] + lnb_ref[:, sl]
        bonus = _dot_exact_rhs(r * k * rk_ref[:, sl], ones) * v
        y_ref[:, sl] = ((yn + bonus) * g_ref[0, q]).astype(y_ref.dtype)

    @pl.when(j == pl.num_programs(1) - 1)
    def _():
        st_ref[0] = s_scr[...]


def _dot_exact_rhs_left(e, a):
    hi = a.astype(BF16)
    r1 = a - hi.astype(F32)
    mid = r1.astype(BF16)
    lo = (r1 - mid.astype(F32)).astype(BF16)
    return _dot(e, hi) + (_dot(e, mid) + _dot(e, lo))


def _rwkv_scan(arrs, s0, ln_w, ln_b, r_k, C):
    B, _, T, _ = arrs[0].shape
    nt = T // C
    in_spec = pl.BlockSpec((1, NG, C, GW), lambda b, j: (b, 0, j, 0))
    st_spec = pl.BlockSpec((1, NG, GW, GW), lambda b, j: (b, 0, 0, 0))
    row = pl.BlockSpec((1, RW_WIDTH), lambda b, j: (0, 0))
    return pl.pallas_call(
        functools.partial(_rwkv_scan_kernel, C=C),
        out_shape=[jax.ShapeDtypeStruct((B * T, RW_WIDTH), BF16),
                   jax.ShapeDtypeStruct((B, NG, GW, GW), F32)],
        grid=(B, nt),
        in_specs=[in_spec] * 7 + [st_spec, row, row, row],
        out_specs=[pl.BlockSpec((C, RW_WIDTH), lambda b, j: (b * nt + j, 0)), st_spec],
        scratch_shapes=[pltpu.VMEM((NG, GW, GW), F32)],
        compiler_params=_params(("parallel", "arbitrary"), V7X_VMEM_LIMIT),
    )(*arrs, s0, ln_w.reshape(1, -1), ln_b.reshape(1, -1), r_k.reshape(1, -1))


def _state_to_blockdiag(s):
    B = s.shape[0]
    s = s.reshape(B, NG, HPG, RW_HEAD, RW_HEAD)
    eye = jnp.eye(HPG, dtype=s.dtype)
    bd = s[:, :, :, :, None, :] * eye[None, None, :, None, :, None]
    return bd.reshape(B, NG, GW, GW)


def _blockdiag_to_state(bd):
    B = bd.shape[0]
    x = bd.reshape(B, NG, HPG, RW_HEAD, HPG, RW_HEAD)
    return jnp.stack([x[:, :, h, :, h, :] for h in range(HPG)], axis=2).reshape(B, RW_HEADS, RW_HEAD, RW_HEAD)


def _gelu(x):
    return 0.5 * x * (1.0 + lax.erf(x * (2.0 ** -0.5)))


def _gmlp_kernel(u_ref, v_ref, lng_ref, lnb_ref, ws_ref, bs_ref, y_ref, vn_ref, *, tm):
    u = _gelu(u_ref[...])
    vf = _gelu(v_ref[...])
    mean = jnp.mean(vf, axis=-1, keepdims=True)
    vc = vf - mean
    var = jnp.mean(vc * vc, axis=-1, keepdims=True)
    vn = vc * lax.rsqrt(var + LN_EPS) * lng_ref[...] + lnb_ref[...]
    vn_ref[...] = vn
    ti = lax.broadcasted_iota(jnp.int32, (GM_CHUNK, GM_CHUNK), 0)
    tj = lax.broadcasted_iota(jnp.int32, (GM_CHUNK, GM_CHUNK), 1)
    causal = ti >= tj
    rows = min(tm, GM_CHUNK)
    for g in range(GM_GROUPS):
        wm = jnp.where(causal, ws_ref[g], 0.0).astype(BF16)
        sl = slice(g * GM_GROUP_DIM, (g + 1) * GM_GROUP_DIM)
        for c in range(max(tm // GM_CHUNK, 1)):
            rs = slice(c * GM_CHUNK, c * GM_CHUNK + rows)
            vg = vn[rs, sl].astype(BF16)
            if rows < GM_CHUNK:
                vg = jnp.concatenate([vg, jnp.zeros((GM_CHUNK - rows, GM_GROUP_DIM), BF16)], axis=0)
            mixed = (_dot(wm, vg) + bs_ref[g])[:rows]
            y_ref[rs, sl] = (u[rs, sl] * mixed).astype(y_ref.dtype)


def _gmlp(p_gm, row_off, rows, tm, ln_g, ln_b, ws, bs_b, y_dtype):
    off = row_off // tm
    assert row_off % tm == 0 and rows % tm == 0
    row = pl.BlockSpec((1, GM_WIDTH), lambda i: (0, 0))
    full3 = pl.BlockSpec((GM_GROUPS, GM_CHUNK, GM_CHUNK), lambda i: (0, 0, 0))
    return pl.pallas_call(
        functools.partial(_gmlp_kernel, tm=tm),
        out_shape=[jax.ShapeDtypeStruct((rows, GM_WIDTH), y_dtype), jax.ShapeDtypeStruct((rows, GM_WIDTH), F32)],
        grid=(rows // tm,),
        in_specs=[pl.BlockSpec((tm, GM_WIDTH), lambda i: (off + i, 0)),
                  pl.BlockSpec((tm, GM_WIDTH), lambda i: (off + i, 1)),
                  row, row, full3, full3],
        out_specs=[pl.BlockSpec((tm, GM_WIDTH), lambda i: (i, 0))] * 2,
        compiler_params=_params(("parallel",), V7X_VMEM_LIMIT),
    )(p_gm, p_gm, ln_g.reshape(1, -1), ln_b.reshape(1, -1), ws, bs_b)


def _attn_prep_kernel(q_ref, k_ref, cos_ref, sin_ref, qg_ref, kg_ref, qo_ref, ko_ref):
    cos = cos_ref[...]
    sin = sin_ref[...]

    def norm_rope(x, gain):
        y = x * lax.rsqrt(jnp.mean(x * x, axis=-1, keepdims=True) + NORM_EPS) * gain
        return y * cos + pltpu.roll(y, ATT_HEAD // 2, axis=1) * sin

    for h in range(N_DIL * ATT_GROUP_HEADS):
        sl = slice(h * ATT_HEAD, (h + 1) * ATT_HEAD)
        qo_ref[:, sl] = norm_rope(q_ref[:, sl], qg_ref[...]).astype(qo_ref.dtype)
        ko_ref[:, sl] = norm_rope(k_ref[:, sl], kg_ref[...]).astype(ko_ref.dtype)


def _attn_prep(p_at, cos, sin, q_gain, k_gain, tm):
    M = p_at.shape[0]
    blk = lambda c: pl.BlockSpec((tm, ATT_WIDTH), lambda i: (i, c))
    tab = pl.BlockSpec((tm, ATT_HEAD), lambda i: (i, 0))
    gain = pl.BlockSpec((1, ATT_HEAD), lambda i: (0, 0))
    return pl.pallas_call(
        _attn_prep_kernel,
        out_shape=[jax.ShapeDtypeStruct((M, ATT_WIDTH), F32), jax.ShapeDtypeStruct((M, ATT_WIDTH), F32)],
        grid=(M // tm,),
        in_specs=[blk(0), blk(1), tab, tab, gain, gain],
        out_specs=[pl.BlockSpec((tm, ATT_WIDTH), lambda i: (i, 0))] * 2,
        compiler_params=_params(("parallel",), V7X_VMEM_LIMIT),
    )(p_at, p_at, cos, sin, q_gain.reshape(1, -1), k_gain.reshape(1, -1))


def _band_attn_kernel(q_ref, k_ref, v_ref, o_ref, lse_ref, *, L, ncb):
    nb = L // ATT_BLK
    scale = ATT_HEAD ** -0.5
    qi = lax.broadcasted_iota(jnp.int32, (ATT_BLK, ATT_BLK), 0)
    kj = lax.broadcasted_iota(jnp.int32, (ATT_BLK, ATT_BLK), 1)
    for c in range(ncb):
        sl = slice(c * ATT_HEAD, (c + 1) * ATT_HEAD)
        for n in range(nb):
            rs = slice(n * ATT_BLK, (n + 1) * ATT_BLK)
            q = q_ref[0, rs, sl].astype(BF16)
            kc = k_ref[0, rs, sl].astype(BF16)
            vc = v_ref[0, rs, sl].astype(BF16)
            s_c = jnp.where(kj <= qi, _dot(q, kc, NT) * scale, NEG_INF)
            m = jnp.max(s_c, axis=-1, keepdims=True)
            if n > 0:
                ps_ = slice((n - 1) * ATT_BLK, n * ATT_BLK)
                kp = k_ref[0, ps_, sl].astype(BF16)
                vp = v_ref[0, ps_, sl].astype(BF16)
                s_p = jnp.where(kj >= qi, _dot(q, kp, NT) * scale, NEG_INF)
                m = jnp.maximum(m, jnp.max(s_p, axis=-1, keepdims=True))
                e_p = jnp.exp(s_p - m)
            e_c = jnp.exp(s_c - m)
            den = jnp.sum(e_c, axis=-1, keepdims=True)
            if n > 0:
                den = den + jnp.sum(e_p, axis=-1, keepdims=True)
            lse = m + jnp.log(den)
            acc = _dot(jnp.exp(s_c - lse).astype(BF16), vc)
            if n > 0:
                acc = acc + _dot(jnp.exp(s_p - lse).astype(BF16), vp)
            o_ref[0, rs, sl] = acc
            lse_ref[0, rs, sl] = jnp.broadcast_to(lse, (ATT_BLK, ATT_HEAD))


def _band_attn(q, k, v, ncb):
    B, L, NCW = q.shape
    cw = ncb * ATT_HEAD
    spec = pl.BlockSpec((1, L, cw), lambda b, c: (b, 0, c))
    return pl.pallas_call(
        functools.partial(_band_attn_kernel, L=L, ncb=ncb),
        out_shape=[jax.ShapeDtypeStruct((B, L, NCW), F32)] * 2,
        grid=(B, NCW // cw),
        in_specs=[spec] * 3,
        out_specs=[spec] * 2,
        compiler_params=_params(("parallel", "parallel"), V7X_VMEM_LIMIT),
    )(q, k, v)


def _attn_merge_kernel(o0, o1, o2, l0, l1, l2, y_ref):
    a, b, c = l0[...], l1[...], l2[...]
    m = jnp.maximum(jnp.maximum(a, b), c)
    wa, wb, wc = jnp.exp(a - m), jnp.exp(b - m), jnp.exp(c - m)
    y = (wa * o0[...] + wb * o1[...] + wc * o2[...]) / (wa + wb + wc)
    y_ref[...] = y.astype(y_ref.dtype)


def _attn_merge(os_, ls_, tm):
    M = os_[0].shape[0]
    spec = pl.BlockSpec((tm, ATT_OUT), lambda i: (i, 0))
    return pl.pallas_call(
        _attn_merge_kernel,
        out_shape=jax.ShapeDtypeStruct((M, ATT_OUT), BF16),
        grid=(M // tm,),
        in_specs=[spec] * 6,
        out_specs=spec,
        compiler_params=_params(("parallel",), V7X_VMEM_LIMIT),
    )(*os_, *ls_)


def _attn_sample_kernel(q_ref, k_ref, v_ref, c0_ref, c1_ref, c2_ref, y_ref, *, T):
    scale = ATT_HEAD ** -0.5
    caches = (c0_ref, c1_ref, c2_ref)
    zpad = jnp.zeros((LANES - T, ATT_HEAD), F32)
    outs = [[None] * N_DIL for _ in range(ATT_GROUP_HEADS)]
    lses = [[None] * N_DIL for _ in range(ATT_GROUP_HEADS)]
    for g, (window, dil) in enumerate(DIL_PAIRS):
        cref = caches[g]
        wb = cref.shape[1]
        nkeys = wb + LANES
        t = lax.broadcasted_iota(jnp.int32, (T, nkeys), 0)
        jrow = lax.broadcasted_iota(jnp.int32, (T, nkeys), 1)
        dist = wb + t - jrow
        valid = (dist >= 0) & (dist <= window) & ((dist & (dil - 1)) == 0)
        for h in range(ATT_GROUP_HEADS):
            sl = slice((g * ATT_GROUP_HEADS + h) * ATT_HEAD, (g * ATT_GROUP_HEADS + h + 1) * ATT_HEAD)
            q = q_ref[:, sl].astype(BF16)
            kcat = jnp.concatenate([cref[0, :, h * ATT_HEAD:(h + 1) * ATT_HEAD],
                                    k_ref[:, sl], zpad], axis=0).astype(BF16)
            vcat = jnp.concatenate([cref[0, :, ATT_OUT + h * ATT_HEAD:ATT_OUT + (h + 1) * ATT_HEAD],
                                    v_ref[:, sl], zpad], axis=0).astype(BF16)
            s = jnp.where(valid, _dot(q, kcat, NT) * scale, NEG_INF)
            m = jnp.max(s, axis=-1, keepdims=True)
            lse = m + jnp.log(jnp.sum(jnp.exp(s - m), axis=-1, keepdims=True))
            outs[h][g] = _dot(jnp.exp(s - lse).astype(BF16), vcat)
            lses[h][g] = lse
    for h in range(ATT_GROUP_HEADS):
        m = jnp.maximum(jnp.maximum(lses[h][0], lses[h][1]), lses[h][2])
        w = [jnp.exp(l - m) for l in lses[h]]
        y = (w[0] * outs[h][0] + w[1] * outs[h][1] + w[2] * outs[h][2]) / (w[0] + w[1] + w[2])
        y_ref[:, h * ATT_HEAD:(h + 1) * ATT_HEAD] = y.astype(y_ref.dtype)


def _attn_sample(q_rot, k_rot, p_at, row_off, B, T, caches):
    off = row_off // T
    blk = lambda c: pl.BlockSpec((T, ATT_WIDTH), lambda b: (off + b, c))
    cspec = lambda a: pl.BlockSpec((1,) + a.shape[1:], lambda b: (b, 0, 0))
    return pl.pallas_call(
        functools.partial(_attn_sample_kernel, T=T),
        out_shape=jax.ShapeDtypeStruct((B * T, ATT_OUT), F32),
        grid=(B,),
        in_specs=[blk(0), blk(0), blk(2)] + [cspec(c) for c in caches],
        out_specs=pl.BlockSpec((T, ATT_OUT), lambda b: (b, 0)),
        compiler_params=_params(("parallel",), V7X_VMEM_LIMIT),
    )(q_rot, k_rot, p_at, *caches)


def _branch_kernel(yr_ref, yg_ref, ya_ref, wr_ref, wg_ref, wa_ref, g0_ref, g1_ref, g2_ref, o_ref):
    acc = _sigmoid(g0_ref[...]) * _dot(yr_ref[...], wr_ref[...].astype(BF16))
    acc = acc + _sigmoid(g1_ref[...]) * _dot(yg_ref[...], wg_ref[...].astype(BF16))
    acc = acc + _sigmoid(g2_ref[...]) * _dot(ya_ref[...], wa_ref[...].astype(BF16))
    o_ref[...] = acc.astype(o_ref.dtype)


def _branch(y_rw, y_gm, y_at, w_rw, w_gm, w_at, p_gate, tm, tn):
    M = y_rw.shape[0]
    nb = D_MODEL // tn
    lhs = lambda kdim: pl.BlockSpec((tm, kdim), lambda i, j: (i, 0))
    rhs = lambda kdim: pl.BlockSpec((kdim, tn), lambda i, j: (0, j))
    gate = lambda br: pl.BlockSpec((tm, tn), lambda i, j: (i, br * nb + j))
    return pl.pallas_call(
        _branch_kernel,
        out_shape=jax.ShapeDtypeStruct((M, D_MODEL), BF16),
        grid=(M // tm, nb),
        in_specs=[lhs(RW_WIDTH), lhs(GM_WIDTH), lhs(ATT_OUT), rhs(RW_WIDTH), rhs(GM_WIDTH), rhs(ATT_OUT),
                  gate(0), gate(1), gate(2)],
        out_specs=pl.BlockSpec((tm, tn), lambda i, j: (i, j)),
        compiler_params=_params(("parallel", "parallel"), V7X_VMEM_LIMIT),
    )(y_rw, y_gm, y_at, w_rw, w_gm, w_at, p_gate, p_gate, p_gate)


def _rope_tables(pos):
    half = ATT_HEAD // 2
    inv = ROPE_THETA ** (-jnp.arange(half, dtype=F32) / half)
    ang = pos.astype(F32)[:, None] * inv[None, :]
    cos, sin = jnp.cos(ang), jnp.sin(ang)
    return jnp.concatenate([cos, cos], -1), jnp.concatenate([-sin, sin], -1)


def _streams(x, B, T, g, dil):
    return x[:B * T, g * ATT_OUT:(g + 1) * ATT_OUT].reshape(B, T // dil, dil * ATT_OUT)


def kernel(x_prompt, x_sample, cache_kv_w128, cache_kv_w512, cache_kv_w2048, state_rwkv, state_rwkv_shift, norm1, w_in, rw_mu, rw_w0, rw_w_up, rw_a0, rw_a_up, rw_g_up, rw_k_k, rw_k_a, rw_r_k, rw_ln_w, rw_ln_b, gm_ln_g, gm_ln_b, gm_ws, gm_bs, att_q_gain, att_k_gain, w_br_rwkv, w_br_gmlp, w_br_attn, w_out, norm2, w_ff1, w_ff2):
    BP, TP, _ = x_prompt.shape
    BS, TS, _ = x_sample.shape
    depth = w_in.shape[0]
    MP, MS = BP * TP, BS * TS
    M = MP + MS
    TM = ROW_TILE
    assert M % (2 * TM) == 0 and MP % 1024 == 0 and TP % SCAN_C == 0 and TS <= SCAN_C
    x = jnp.concatenate([x_prompt.reshape(MP, D_MODEL), x_sample.reshape(MS, D_MODEL)], axis=0)

    pos = jnp.concatenate([jnp.tile(jnp.arange(TP), BP), jnp.tile(PAST_LEN + jnp.arange(TS), BS)])
    cos_t, sin_t = _rope_tables(pos)
    caches_all = [c.reshape(depth, BS, c.shape[2], 2 * ATT_OUT) for c in (cache_kv_w128, cache_kv_w512, cache_kv_w2048)]
    zeros_state = jnp.zeros((BP, NG, GW, GW), F32)
    zeros_shift = jnp.zeros((BP, RW_COLS), F32)
    c_rw, c_gm, c_at = RW_COLS, RW_COLS + 2 * GM_WIDTH, RW_COLS + 2 * GM_WIDTH + 3 * ATT_WIDTH

    kvp = [[] for _ in range(N_DIL)]
    kvs = [[] for _ in range(N_DIL)]
    st_p, st_s, sh_p, sh_s, gmv_s = [], [], [], [], []
    for l in range(depth):
        h = _rmsnorm(x, norm1[l], TM)
        mm_in = functools.partial(_matmul, h, w_in[l], tm=TM * 2, tn=256, tk=D_MODEL)
        p_rw = mm_in(col_off=0, n_cols=RW_COLS)
        p_gm = mm_in(col_off=c_rw, n_cols=2 * GM_WIDTH)
        p_at = mm_in(col_off=c_gm, n_cols=3 * ATT_WIDTH)
        p_gate = mm_in(col_off=c_at, n_cols=N_BRANCH * D_MODEL)

        zpad = jnp.zeros((DECAY_LORA, RW_WIDTH), F32)
        wup_pad = jnp.concatenate([rw_w_up[l], zpad], axis=0)
        aup_pad = jnp.concatenate([zpad, rw_a_up[l]], axis=0)
        prep = functools.partial(_rwkv_prep, p_rw, mu=rw_mu[l], w0=rw_w0[l], wup_pad=wup_pad, a0=rw_a0[l],
                                 aup_pad=aup_pad, gup=rw_g_up[l], k_k=rw_k_k[l].reshape(-1), k_a=rw_k_a[l].reshape(-1))
        scan = functools.partial(_rwkv_scan, ln_w=rw_ln_w[l].reshape(-1), ln_b=rw_ln_b[l].reshape(-1),
                                 r_k=rw_r_k[l].reshape(-1), C=SCAN_C)
        arrs_p = prep(row_off=0, B=BP, T=TP, tm=128, prev=zeros_shift)
        y_rw_p, sT_p = scan(arrs_p, zeros_state)
        arrs_s = prep(row_off=MP, B=BS, T=TS, tm=TS, prev=state_rwkv_shift[l])
        arrs_s = [jnp.pad(a, ((0, 0), (0, 0), (0, SCAN_C - TS), (0, 0))) for a in arrs_s]
        y_rw_s, sT_s = scan(arrs_s, _state_to_blockdiag(state_rwkv[l]))
        y_rw_s = y_rw_s.reshape(BS, SCAN_C, RW_WIDTH)[:, :TS].reshape(MS, RW_WIDTH)
        y_rw = jnp.concatenate([y_rw_p, y_rw_s], axis=0)
        st_p.append(_blockdiag_to_state(sT_p))
        st_s.append(_blockdiag_to_state(sT_s))
        sh_p.append(p_rw[:MP].reshape(BP, TP, RW_COLS)[:, -1])
        sh_s.append(p_rw[MP:].reshape(BS, TS, RW_COLS)[:, -1])

        bs_b = jnp.broadcast_to(gm_bs[l][:, :, None], (GM_GROUPS, GM_CHUNK, GM_GROUP_DIM))
        y_gm_p, _ = _gmlp(p_gm, 0, MP, 256, gm_ln_g[l], gm_ln_b[l], gm_ws[l], bs_b, BF16)
        y_gm_s, vn_s = _gmlp(p_gm, MP, MS, TS, gm_ln_g[l], gm_ln_b[l], gm_ws[l], bs_b, F32)
        y_gm = jnp.concatenate([y_gm_p, y_gm_s.astype(BF16)], axis=0)
        gmv_s.append(vn_s.reshape(BS, TS, GM_WIDTH))

        q_rot, k_rot = _attn_prep(p_at, cos_t, sin_t, att_q_gain[l], att_k_gain[l], TM)
        v_all = p_at[:, 2 * ATT_WIDTH:]
        os_, ls_ = [], []
        for g, (window, dil) in enumerate(DIL_PAIRS):
            o, lse = _band_attn(_streams(q_rot, BP, TP, g, dil), _streams(k_rot, BP, TP, g, dil),
                                _streams(v_all, BP, TP, g, dil), ncb=1 if dil == 1 else ATT_GROUP_HEADS)
            os_.append(o.reshape(MP, ATT_OUT))
            ls_.append(lse.reshape(MP, ATT_OUT))
            keep = min(window, TP)
            kg = k_rot[:MP].reshape(BP, TP, N_DIL, ATT_GROUP_HEADS, ATT_HEAD)[:, TP - keep:, g]
            vg = v_all[:MP].reshape(BP, TP, N_DIL, ATT_GROUP_HEADS, ATT_HEAD)[:, TP - keep:, g]
            kvp[g].append(jnp.stack([kg, vg], axis=2))
            kgs = k_rot[MP:].reshape(BS, TS, N_DIL, ATT_GROUP_HEADS, ATT_HEAD)[:, :, g]
            vgs = v_all[MP:].reshape(BS, TS, N_DIL, ATT_GROUP_HEADS, ATT_HEAD)[:, :, g]
            kvs[g].append(jnp.stack([kgs, vgs], axis=2))
        y_at_p = _attn_merge(os_, ls_, 1024)
        y_at_s = _attn_sample(q_rot, k_rot, p_at, MP, BS, TS, [c[l] for c in caches_all])
        y_at = jnp.concatenate([y_at_p, y_at_s.astype(BF16)], axis=0)

        merged = _branch(y_rw, y_gm, y_at, w_br_rwkv[l], w_br_gmlp[l], w_br_attn[l], p_gate, TM * 2, 256)
        x = _matmul(merged, w_out[l], tm=TM * 2, tn=256, tk=D_MODEL, epilogue="residual", res=x)
        h2 = _rmsnorm(x, norm2[l], TM)
        act = _matmul(h2, w_ff1[l], tm=TM * 2, tn=256, tk=D_MODEL, epilogue="relu2", out_dtype=BF16)
        x = _matmul(act, w_ff2[l], tm=TM * 2, tn=512, tk=2048, epilogue="residual", res=x)

    return (x[:MP].reshape(BP, TP, D_MODEL), x[MP:].reshape(BS, TS, D_MODEL),
            jnp.stack(kvp[0], 0), jnp.stack(kvp[1], 0), jnp.stack(kvp[2], 0),
            jnp.stack(kvs[0], 0), jnp.stack(kvs[1], 0), jnp.stack(kvs[2], 0),
            jnp.stack(st_p, 0), jnp.stack(st_s, 0), jnp.stack(sh_p, 0), jnp.stack(sh_s, 0),
            jnp.stack(gmv_s, 0))
```

```python
import functools

import jax
import jax.numpy as jnp
from jax import lax
from jax.experimental import pallas as pl
from jax.experimental.pallas import tpu as pltpu

F32 = jnp.float32
BF16 = jnp.bfloat16

LANES = 128
V7X_VMEM_LIMIT = 56 * 1024 * 1024

D_MODEL = 4096
RW_HEADS = 24
RW_HEAD = 64
RW_WIDTH = RW_HEADS * RW_HEAD
DECAY_LORA = 64
AAA_LORA = 64
GATE_LORA = 128
RW_COLS = 3 * RW_WIDTH + DECAY_LORA + AAA_LORA + GATE_LORA
GN_EPS = 64e-5
GM_CHUNK = 128
GM_GROUPS = 12
GM_GROUP_DIM = 128
GM_WIDTH = GM_GROUPS * GM_GROUP_DIM
LN_EPS = 1e-5
DIL_PAIRS = ((128, 1), (512, 4), (2048, 16))
N_DIL = 3
ATT_GROUP_HEADS = 4
ATT_HEAD = 128
ATT_WIDTH = N_DIL * ATT_GROUP_HEADS * ATT_HEAD
ATT_OUT = ATT_GROUP_HEADS * ATT_HEAD
ATT_BLK = 128
ROPE_THETA = 10000.0
N_BRANCH = 3
D_FF = 4 * D_MODEL
NORM_EPS = 1e-6
NEG_INF = -1e30
PAST_LEN = 8192

ROW_TILE = 688

HPG = 4
GW = HPG * RW_HEAD
NG = RW_HEADS // HPG
SCAN_C = 64

NN = (((1,), (0,)), ((), ()))
NT = (((1,), (1,)), ((), ()))


def _dot(a, b, dims=NN):
    return lax.dot_general(a, b, dims, preferred_element_type=F32)


def _dot1(a, b, dims=NN):
    return _dot(a.astype(BF16), b.astype(BF16), dims)


def _split3(a):
    hi = a.astype(BF16)
    r1 = a - hi.astype(F32)
    mid = r1.astype(BF16)
    return hi, mid, (r1 - mid.astype(F32)).astype(BF16)


def _dot_exact_rhs(a, e):
    hi, mid, lo = _split3(a)
    return _dot(hi, e) + (_dot(mid, e) + _dot(lo, e))


def _dot_exact_lhs(e, a):
    hi, mid, lo = _split3(a)
    return _dot(e, hi) + (_dot(e, mid) + _dot(e, lo))


def _params(sem, vmem=None):
    return pltpu.CompilerParams(dimension_semantics=sem, vmem_limit_bytes=vmem)


def _sigmoid(x):
    return 1.0 / (1.0 + jnp.exp(-x))


def _mm_kernel(*refs, nk, epilogue):
    if epilogue == "residual":
        a_ref, b_ref, res_ref, o_ref = refs
    else:
        a_ref, b_ref, o_ref = refs
    part = _dot(a_ref[...], b_ref[...].astype(BF16))
    if nk == 1:
        if epilogue == "relu2":
            part = jnp.square(jnp.maximum(part, 0.0))
        elif epilogue == "residual":
            part = part + res_ref[...]
        o_ref[...] = part.astype(o_ref.dtype)
    else:
        assert epilogue == "residual"
        k = pl.program_id(2)

        @pl.when(k == 0)
        def _():
            o_ref[...] = res_ref[...] + part

        @pl.when(k > 0)
        def _():
            o_ref[...] += part


def _matmul(a, b, layer, *, col_off=0, n_cols=None, tm, tn, tk, epilogue="none", res=None, out_dtype=F32,
            name="matmul"):
    M, K = a.shape
    n_cols = b.shape[2] if n_cols is None else n_cols
    assert M % tm == 0 and n_cols % tn == 0 and K % tk == 0 and col_off % tn == 0
    nk = K // tk
    assert nk == 1 or out_dtype == F32
    off = col_off // tn
    in_specs = [pl.BlockSpec((tm, tk), lambda i, j, k: (i, k)),
                pl.BlockSpec((None, tk, tn), lambda i, j, k: (layer, k, j + off))]
    args = [a, b]
    if epilogue == "residual":
        in_specs.append(pl.BlockSpec((tm, tn), lambda i, j, k: (i, j)))
        args.append(res)
    return pl.pallas_call(
        functools.partial(_mm_kernel, nk=nk, epilogue=epilogue),
        out_shape=jax.ShapeDtypeStruct((M, n_cols), out_dtype),
        grid=(M // tm, n_cols // tn, nk),
        in_specs=in_specs,
        out_specs=pl.BlockSpec((tm, tn), lambda i, j, k: (i, j)),
        compiler_params=_params(("parallel", "parallel", "arbitrary"), V7X_VMEM_LIMIT),
        name=name,
    )(*args)


def _rmsnorm_kernel(x_ref, g_ref, o_ref):
    x = x_ref[...]
    y = x * lax.rsqrt(jnp.mean(x * x, axis=-1, keepdims=True) + NORM_EPS)
    o_ref[...] = (y * g_ref[...]).astype(o_ref.dtype)


def _rmsnorm(x, g, tm):
    M, D = x.shape
    return pl.pallas_call(
        _rmsnorm_kernel,
        name="rmsnorm",
        out_shape=jax.ShapeDtypeStruct((M, D), BF16),
        grid=(M // tm,),
        in_specs=[pl.BlockSpec((tm, D), lambda i: (i, 0)), pl.BlockSpec((1, D), lambda i: (0, 0))],
        out_specs=pl.BlockSpec((tm, D), lambda i: (i, 0)),
        compiler_params=_params(("parallel",), V7X_VMEM_LIMIT),
    )(x, g.reshape(1, D))


def _head_ones(width, head):
    r = lax.broadcasted_iota(jnp.int32, (width, width), 0) // head
    c = lax.broadcasted_iota(jnp.int32, (width, width), 1) // head
    return jnp.where(r == c, 1.0, 0.0).astype(BF16)


def _rwkv_prep_kernel(p_ref, prev_ref, mu_ref, w0_ref, wup_ref, a0_ref, aup_ref, gup_ref, kk_ref, ka_ref,
                      r_o, lw_o, k_o, v_o, kkn_o, b_o, g_o, carry_ref, *, tm, t_out):
    j = pl.program_id(1)

    @pl.when(j == 0)
    def _():
        carry_ref[...] = prev_ref[0]

    p = p_ref[...]
    row = lax.broadcasted_iota(jnp.int32, p.shape, 0)
    p_prev = jnp.where(row == 0, carry_ref[...], pltpu.roll(p, 1, axis=0))
    carry_ref[...] = p[tm - 1:tm, :]
    ps = p + mu_ref[...] * (p_prev - p)

    W = RW_WIDTH
    r = ps[:, 0:W]
    k = ps[:, W:2 * W]
    v = ps[:, 2 * W:3 * W]
    wa = ps[:, 3 * W:3 * W + LANES]
    g_in = ps[:, 3 * W + LANES:3 * W + 2 * LANES]

    z = -(w0_ref[...] + _dot(jnp.tanh(wa).astype(BF16), wup_ref[...].astype(BF16)))
    softplus = jnp.maximum(z, 0.0) + jnp.log1p(jnp.exp(-jnp.abs(z)))
    lw = -jnp.exp(-softplus - 0.5)
    a = _sigmoid(a0_ref[...] + _dot(wa.astype(BF16), aup_ref[...].astype(BF16)))
    g = _dot(_sigmoid(g_in).astype(BF16), gup_ref[...].astype(BF16))

    kk = k * kk_ref[...]
    ones = _head_ones(GW, RW_HEAD)
    kmod = k * (1.0 + (a - 1.0) * ka_ref[...])
    outs = (r_o, lw_o, k_o, v_o, kkn_o, b_o, g_o)
    if t_out > tm:
        for o in outs:
            o[...] = jnp.zeros(o.shape, o.dtype)
    for q in range(NG):
        sl = slice(q * GW, (q + 1) * GW)
        kq = kk[:, sl]
        ssq = _dot_exact_rhs(kq * kq, ones)
        kn = kq * lax.rsqrt(jnp.maximum(ssq, 1e-24))
        vals = (r[:, sl], lw[:, sl], kmod[:, sl], v[:, sl], kn, kn * a[:, sl], g[:, sl])
        for o, val in zip(outs, vals):
            o[0, q, 0:tm, :] = val


def _rwkv_prep(p_rw, row_off, B, T, tm, t_out, prev, mu, w0, wup_pad, a0, aup_pad, gup, k_k, k_a):
    nt = T // tm
    off = row_off // tm
    assert row_off % tm == 0 and T % tm == 0 and t_out >= tm
    row = lambda n: pl.BlockSpec((1, n), lambda b, j: (0, 0))
    out_spec = pl.BlockSpec((1, NG, t_out, GW), lambda b, j: (b, 0, j, 0))
    out = jax.ShapeDtypeStruct((B, NG, nt * t_out, GW), F32)
    return pl.pallas_call(
        functools.partial(_rwkv_prep_kernel, tm=tm, t_out=t_out),
        name="rwkv_prep",
        out_shape=[out] * 7,
        grid=(B, nt),
        in_specs=[pl.BlockSpec((tm, RW_COLS), lambda b, j: (off + b * nt + j, 0)),
                  pl.BlockSpec((1, 1, RW_COLS), lambda b, j: (b, 0, 0)),
                  row(RW_COLS), row(RW_WIDTH),
                  pl.BlockSpec((LANES, RW_WIDTH), lambda b, j: (0, 0)),
                  row(RW_WIDTH),
                  pl.BlockSpec((LANES, RW_WIDTH), lambda b, j: (0, 0)),
                  pl.BlockSpec((GATE_LORA, RW_WIDTH), lambda b, j: (0, 0)),
                  row(RW_WIDTH), row(RW_WIDTH)],
        out_specs=[out_spec] * 7,
        scratch_shapes=[pltpu.VMEM((1, RW_COLS), F32)],
        compiler_params=_params(("parallel", "arbitrary"), V7X_VMEM_LIMIT),
    )(p_rw, prev.reshape(B, 1, RW_COLS), mu.reshape(1, -1), w0.reshape(1, -1), wup_pad, a0.reshape(1, -1),
      aup_pad, gup, k_k.reshape(1, -1), k_a.reshape(1, -1))


def _stack_heads(x, lane_head):
    return jnp.concatenate([jnp.where(lane_head == h, x, 0.0) for h in range(HPG)], axis=0)


def _rwkv_scan_kernel(r_ref, lw_ref, k_ref, v_ref, kk_ref, b_ref, g_ref, s0_ref, lnw_ref, lnb_ref, rk_ref,
                      y_ref, st_ref, s_scr, *, C):
    j = pl.program_id(1)

    @pl.when(j == 0)
    def _():
        s_scr[...] = s0_ref[0]

    R = HPG * C
    G = range(NG)
    lane_head = lax.broadcasted_iota(jnp.int32, (1, GW), 1) // RW_HEAD
    ii = lax.broadcasted_iota(jnp.int32, (R, R), 0)
    jj = lax.broadcasted_iota(jnp.int32, (R, R), 1)
    ti = lax.broadcasted_iota(jnp.int32, (C, C), 0)
    tj = lax.broadcasted_iota(jnp.int32, (C, C), 1)
    tril = jnp.where(ti >= tj, 1.0, 0.0).astype(BF16)
    ones = _head_ones(GW, RW_HEAD)
    strict = ii > jj
    incl = ii >= jj
    sls = [slice(q * GW, (q + 1) * GW) for q in G]

    r = [r_ref[0, q] for q in G]
    k = [k_ref[0, q] for q in G]
    v = [v_ref[0, q] for q in G]
    lw_all = jnp.concatenate([lw_ref[0, q] for q in G], axis=1)
    cum_all = _dot_exact_lhs(tril, lw_all)
    cum = [cum_all[:, sl] for sl in sls]
    pc = [jnp.exp(c) for c in cum]
    pinv = [jnp.exp(-c) for c in cum]
    pend = [p[C - 1:C, :] for p in pc]
    ar = [jnp.concatenate([_stack_heads(-kk_ref[0, q] * jnp.exp(cum[q] - lw_ref[0, q]), lane_head),
                           _stack_heads(r[q] * pc[q], lane_head)], axis=0).astype(BF16) for q in G]
    bk = [jnp.concatenate([_stack_heads(b_ref[0, q] * pinv[q], lane_head),
                           _stack_heads(k[q] * pinv[q], lane_head)], axis=0) for q in G]
    gram = [_dot(ar[q], bk[q].astype(BF16), NT) for q in G]
    a_ab = [jnp.where(strict, gram[q][:R, :R], 0.0) for q in G]
    a_kr = [jnp.concatenate([jnp.where(strict, gram[q][:R, R:], 0.0),
                             jnp.where(incl, gram[q][R:, R:], 0.0)], axis=0).astype(BF16) for q in G]
    a_rb = [jnp.where(incl, gram[q][R:, :R], 0.0).astype(BF16) for q in G]

    eye = jnp.where(ii == jj, 1.0, 0.0)
    t = [eye + jnp.where((ii >> 1) == (jj >> 1), a_ab[q], 0.0) for q in G]
    size = 2
    while size < C:
        sh = size.bit_length() - 1
        sel = ((ii >> (sh + 1)) == (jj >> (sh + 1))) & ((ii >> sh) != (jj >> sh))
        tb = [t[q].astype(BF16) for q in G]
        mid = [_dot(jnp.where(sel, a_ab[q], 0.0).astype(BF16), tb[q]) for q in G]
        t = [t[q] + _dot(tb[q], mid[q].astype(BF16)) for q in G]
        size *= 2

    s = [s_scr[q] for q in G]
    v_st = [_stack_heads(v[q], lane_head) for q in G]
    from_state = [_dot(ar[q], s[q].astype(BF16), NT) for q in G]
    from_v = [_dot(a_kr[q], v_st[q].astype(BF16)) for q in G]
    u = [_dot1(t[q], from_state[q][:R] + from_v[q][:R]) for q in G]
    y_st = [from_state[q][R:] + from_v[q][R:] + _dot(a_rb[q], u[q].astype(BF16)) for q in G]
    for q in G:
        uv = jnp.concatenate([u[q], v_st[q]], axis=0)
        s_scr[q] = s[q] * pend[q] + _dot1(uv.T, bk[q] * pend[q])

    for q in G:
        y = y_st[q][0:C]
        for h in range(1, HPG):
            y = y + y_st[q][h * C:(h + 1) * C]
        mean = _dot_exact_rhs(y, ones) * (1.0 / RW_HEAD)
        yc = y - mean
        var = _dot_exact_rhs(yc * yc, ones) * (1.0 / RW_HEAD)
        sl = sls[q]
        yn = yc * lax.rsqrt(var + GN_EPS) * lnw_ref[:, sl] + lnb_ref[:, sl]
        bonus = _dot_exact_rhs(r[q] * k[q] * rk_ref[:, sl], ones) * v[q]
        y_ref[:, sl] = ((yn + bonus) * g_ref[0, q]).astype(y_ref.dtype)

    @pl.when(j == pl.num_programs(1) - 1)
    def _():
        st_ref[0] = s_scr[...]


def _rwkv_scan(arrs, s0, ln_w, ln_b, r_k, C):
    B, _, T, _ = arrs[0].shape
    nt = T // C
    in_spec = pl.BlockSpec((1, NG, C, GW), lambda b, j: (b, 0, j, 0))
    st_spec = pl.BlockSpec((1, NG, GW, GW), lambda b, j: (b, 0, 0, 0))
    row = pl.BlockSpec((1, RW_WIDTH), lambda b, j: (0, 0))
    return pl.pallas_call(
        functools.partial(_rwkv_scan_kernel, C=C),
        name="rwkv_scan",
        out_shape=[jax.ShapeDtypeStruct((B * T, RW_WIDTH), BF16),
                   jax.ShapeDtypeStruct((B, NG, GW, GW), F32)],
        grid=(B, nt),
        in_specs=[in_spec] * 7 + [st_spec, row, row, row],
        out_specs=[pl.BlockSpec((C, RW_WIDTH), lambda b, j: (b * nt + j, 0)), st_spec],
        scratch_shapes=[pltpu.VMEM((NG, GW, GW), F32)],
        compiler_params=_params(("parallel", "arbitrary"), V7X_VMEM_LIMIT),
    )(*arrs, s0, ln_w.reshape(1, -1), ln_b.reshape(1, -1), r_k.reshape(1, -1))


def _state_to_blockdiag(s):
    B = s.shape[0]
    s = s.reshape(B, NG, HPG, RW_HEAD, RW_HEAD)
    eye = jnp.eye(HPG, dtype=s.dtype)
    bd = s[:, :, :, :, None, :] * eye[None, None, :, None, :, None]
    return bd.reshape(B, NG, GW, GW)


def _blockdiag_to_state(bd):
    B = bd.shape[0]
    x = bd.reshape(B, NG, HPG, RW_HEAD, HPG, RW_HEAD)
    return jnp.stack([x[:, :, h, :, h, :] for h in range(HPG)], axis=2).reshape(B, RW_HEADS, RW_HEAD, RW_HEAD)


def _gelu(x):
    return 0.5 * x * (1.0 + lax.erf(x * (2.0 ** -0.5)))


def _gmlp_kernel(u_ref, v_ref, lng_ref, lnb_ref, ws_ref, bs_ref, y_ref, vn_ref, *, tm):
    u = _gelu(u_ref[...])
    vf = _gelu(v_ref[...])
    mean = jnp.mean(vf, axis=-1, keepdims=True)
    vc = vf - mean
    var = jnp.mean(vc * vc, axis=-1, keepdims=True)
    vn = vc * lax.rsqrt(var + LN_EPS) * lng_ref[...] + lnb_ref[...]
    vn_ref[...] = vn
    ti = lax.broadcasted_iota(jnp.int32, (GM_CHUNK, GM_CHUNK), 0)
    tj = lax.broadcasted_iota(jnp.int32, (GM_CHUNK, GM_CHUNK), 1)
    causal = ti >= tj
    rows = min(tm, GM_CHUNK)
    for g in range(GM_GROUPS):
        wm = jnp.where(causal, ws_ref[g], 0.0).astype(BF16)
        sl = slice(g * GM_GROUP_DIM, (g + 1) * GM_GROUP_DIM)
        for c in range(max(tm // GM_CHUNK, 1)):
            rs = slice(c * GM_CHUNK, c * GM_CHUNK + rows)
            vg = vn[rs, sl].astype(BF16)
            if rows < GM_CHUNK:
                vg = jnp.concatenate([vg, jnp.zeros((GM_CHUNK - rows, GM_GROUP_DIM), BF16)], axis=0)
            mixed = (_dot(wm, vg) + bs_ref[g])[:rows]
            y_ref[rs, sl] = (u[rs, sl] * mixed).astype(y_ref.dtype)


def _gmlp(p_gm, row_off, rows, tm, ln_g, ln_b, ws, layer, bs_b, y_dtype):
    off = row_off // tm
    assert row_off % tm == 0 and rows % tm == 0
    row = pl.BlockSpec((1, GM_WIDTH), lambda i: (0, 0))
    full3 = pl.BlockSpec((GM_GROUPS, GM_CHUNK, GM_CHUNK), lambda i: (0, 0, 0))
    ws_spec = pl.BlockSpec((None, GM_GROUPS, GM_CHUNK, GM_CHUNK), lambda i: (layer, 0, 0, 0))
    return pl.pallas_call(
        functools.partial(_gmlp_kernel, tm=tm),
        name="gmlp",
        out_shape=[jax.ShapeDtypeStruct((rows, GM_WIDTH), y_dtype), jax.ShapeDtypeStruct((rows, GM_WIDTH), F32)],
        grid=(rows // tm,),
        in_specs=[pl.BlockSpec((tm, GM_WIDTH), lambda i: (off + i, 0)),
                  pl.BlockSpec((tm, GM_WIDTH), lambda i: (off + i, 1)),
                  row, row, ws_spec, full3],
        out_specs=[pl.BlockSpec((tm, GM_WIDTH), lambda i: (i, 0))] * 2,
        compiler_params=_params(("parallel",), V7X_VMEM_LIMIT),
    )(p_gm, p_gm, ln_g.reshape(1, -1), ln_b.reshape(1, -1), ws, bs_b)


def _attn_prep_kernel(q_ref, k_ref, cos_ref, sin_ref, qg_ref, kg_ref, qo_ref, ko_ref):
    cos = cos_ref[...]
    sin = sin_ref[...]

    def norm_rope(x, gain):
        y = x * lax.rsqrt(jnp.mean(x * x, axis=-1, keepdims=True) + NORM_EPS) * gain
        return y * cos + pltpu.roll(y, ATT_HEAD // 2, axis=1) * sin

    for h in range(N_DIL * ATT_GROUP_HEADS):
        sl = slice(h * ATT_HEAD, (h + 1) * ATT_HEAD)
        qo_ref[:, sl] = norm_rope(q_ref[:, sl], qg_ref[...]).astype(qo_ref.dtype)
        ko_ref[:, sl] = norm_rope(k_ref[:, sl], kg_ref[...]).astype(ko_ref.dtype)


def _attn_prep(p_at, cos, sin, q_gain, k_gain, tm):
    M = p_at.shape[0]
    blk = lambda c: pl.BlockSpec((tm, ATT_WIDTH), lambda i: (i, c))
    tab = pl.BlockSpec((tm, ATT_HEAD), lambda i: (i, 0))
    gain = pl.BlockSpec((1, ATT_HEAD), lambda i: (0, 0))
    return pl.pallas_call(
        _attn_prep_kernel,
        name="attn_prep",
        out_shape=[jax.ShapeDtypeStruct((M, ATT_WIDTH), F32), jax.ShapeDtypeStruct((M, ATT_WIDTH), F32)],
        grid=(M // tm,),
        in_specs=[blk(0), blk(1), tab, tab, gain, gain],
        out_specs=[pl.BlockSpec((tm, ATT_WIDTH), lambda i: (i, 0))] * 2,
        compiler_params=_params(("parallel",), V7X_VMEM_LIMIT),
    )(p_at, p_at, cos, sin, q_gain.reshape(1, -1), k_gain.reshape(1, -1))


def _band_attn_kernel(q_ref, k_ref, v_ref, o_ref, lse_ref, *, L, ncb):
    nb = L // ATT_BLK
    scale = ATT_HEAD ** -0.5
    qi = lax.broadcasted_iota(jnp.int32, (ATT_BLK, ATT_BLK), 0)
    kj = lax.broadcasted_iota(jnp.int32, (ATT_BLK, ATT_BLK), 1)
    for c in range(ncb):
        sl = slice(c * ATT_HEAD, (c + 1) * ATT_HEAD)
        for n in range(nb):
            rs = slice(n * ATT_BLK, (n + 1) * ATT_BLK)
            q = q_ref[0, rs, sl].astype(BF16)
            kc = k_ref[0, rs, sl].astype(BF16)
            vc = v_ref[0, rs, sl].astype(BF16)
            s_c = jnp.where(kj <= qi, _dot(q, kc, NT) * scale, NEG_INF)
            m = jnp.max(s_c, axis=-1, keepdims=True)
            if n > 0:
                ps_ = slice((n - 1) * ATT_BLK, n * ATT_BLK)
                kp = k_ref[0, ps_, sl].astype(BF16)
                vp = v_ref[0, ps_, sl].astype(BF16)
                s_p = jnp.where(kj >= qi, _dot(q, kp, NT) * scale, NEG_INF)
                m = jnp.maximum(m, jnp.max(s_p, axis=-1, keepdims=True))
                e_p = jnp.exp(s_p - m)
            e_c = jnp.exp(s_c - m)
            den = jnp.sum(e_c, axis=-1, keepdims=True)
            if n > 0:
                den = den + jnp.sum(e_p, axis=-1, keepdims=True)
            lse = m + jnp.log(den)
            acc = _dot(jnp.exp(s_c - lse).astype(BF16), vc)
            if n > 0:
                acc = acc + _dot(jnp.exp(s_p - lse).astype(BF16), vp)
            o_ref[0, rs, sl] = acc
            lse_ref[0, rs, sl] = jnp.broadcast_to(lse, (ATT_BLK, ATT_HEAD))


def _band_attn(q, k, v, ncb):
    B, L, NCW = q.shape
    cw = ncb * ATT_HEAD
    spec = pl.BlockSpec((1, L, cw), lambda b, c: (b, 0, c))
    return pl.pallas_call(
        functools.partial(_band_attn_kernel, L=L, ncb=ncb),
        name="band_attn",
        out_shape=[jax.ShapeDtypeStruct((B, L, NCW), F32)] * 2,
        grid=(B, NCW // cw),
        in_specs=[spec] * 3,
        out_specs=[spec] * 2,
        compiler_params=_params(("parallel", "parallel"), V7X_VMEM_LIMIT),
    )(q, k, v)


def _attn_merge_kernel(o0, o1, o2, l0, l1, l2, y_ref):
    a, b, c = l0[...], l1[...], l2[...]
    m = jnp.maximum(jnp.maximum(a, b), c)
    wa, wb, wc = jnp.exp(a - m), jnp.exp(b - m), jnp.exp(c - m)
    y = (wa * o0[...] + wb * o1[...] + wc * o2[...]) / (wa + wb + wc)
    y_ref[...] = y.astype(y_ref.dtype)


def _attn_merge(os_, ls_, tm):
    M = os_[0].shape[0]
    spec = pl.BlockSpec((tm, ATT_OUT), lambda i: (i, 0))
    return pl.pallas_call(
        _attn_merge_kernel,
        name="attn_merge",
        out_shape=jax.ShapeDtypeStruct((M, ATT_OUT), BF16),
        grid=(M // tm,),
        in_specs=[spec] * 6,
        out_specs=spec,
        compiler_params=_params(("parallel",), V7X_VMEM_LIMIT),
    )(*os_, *ls_)


def _attn_sample_kernel(q_ref, k_ref, v_ref, c0_ref, c1_ref, c2_ref, y_ref, *, T):
    scale = ATT_HEAD ** -0.5
    caches = (c0_ref, c1_ref, c2_ref)
    zpad = jnp.zeros((LANES - T, ATT_HEAD), F32)
    outs = [[None] * N_DIL for _ in range(ATT_GROUP_HEADS)]
    lses = [[None] * N_DIL for _ in range(ATT_GROUP_HEADS)]
    for g, (window, dil) in enumerate(DIL_PAIRS):
        cref = caches[g]
        wb = cref.shape[1]
        nkeys = wb + LANES
        t = lax.broadcasted_iota(jnp.int32, (T, nkeys), 0)
        jrow = lax.broadcasted_iota(jnp.int32, (T, nkeys), 1)
        dist = wb + t - jrow
        valid = (dist >= 0) & (dist <= window) & ((dist & (dil - 1)) == 0)
        for h in range(ATT_GROUP_HEADS):
            sl = slice((g * ATT_GROUP_HEADS + h) * ATT_HEAD, (g * ATT_GROUP_HEADS + h + 1) * ATT_HEAD)
            q = q_ref[:, sl].astype(BF16)
            kcat = jnp.concatenate([cref[0, :, h * ATT_HEAD:(h + 1) * ATT_HEAD],
                                    k_ref[:, sl], zpad], axis=0).astype(BF16)
            vcat = jnp.concatenate([cref[0, :, ATT_OUT + h * ATT_HEAD:ATT_OUT + (h + 1) * ATT_HEAD],
                                    v_ref[:, sl], zpad], axis=0).astype(BF16)
            s = jnp.where(valid, _dot(q, kcat, NT) * scale, NEG_INF)
            m = jnp.max(s, axis=-1, keepdims=True)
            lse = m + jnp.log(jnp.sum(jnp.exp(s - m), axis=-1, keepdims=True))
            outs[h][g] = _dot(jnp.exp(s - lse).astype(BF16), vcat)
            lses[h][g] = lse
    for h in range(ATT_GROUP_HEADS):
        m = jnp.maximum(jnp.maximum(lses[h][0], lses[h][1]), lses[h][2])
        w = [jnp.exp(l - m) for l in lses[h]]
        y = (w[0] * outs[h][0] + w[1] * outs[h][1] + w[2] * outs[h][2]) / (w[0] + w[1] + w[2])
        y_ref[:, h * ATT_HEAD:(h + 1) * ATT_HEAD] = y.astype(y_ref.dtype)


def _attn_sample(q_rot, k_rot, p_at, row_off, B, T, caches, layer):
    off = row_off // T
    blk = lambda c: pl.BlockSpec((T, ATT_WIDTH), lambda b: (off + b, c))
    cspec = lambda a: pl.BlockSpec((None, 1) + a.shape[2:], lambda b: (layer, b, 0, 0))
    return pl.pallas_call(
        functools.partial(_attn_sample_kernel, T=T),
        name="attn_sample",
        out_shape=jax.ShapeDtypeStruct((B * T, ATT_OUT), F32),
        grid=(B,),
        in_specs=[blk(0), blk(0), blk(2)] + [cspec(c) for c in caches],
        out_specs=pl.BlockSpec((T, ATT_OUT), lambda b: (b, 0)),
        compiler_params=_params(("parallel",), V7X_VMEM_LIMIT),
    )(q_rot, k_rot, p_at, *caches)


def _branch_kernel(yr_ref, yg_ref, ya_ref, wr_ref, wg_ref, wa_ref, g0_ref, g1_ref, g2_ref, o_ref):
    acc = _sigmoid(g0_ref[...]) * _dot(yr_ref[...], wr_ref[...].astype(BF16))
    acc = acc + _sigmoid(g1_ref[...]) * _dot(yg_ref[...], wg_ref[...].astype(BF16))
    acc = acc + _sigmoid(g2_ref[...]) * _dot(ya_ref[...], wa_ref[...].astype(BF16))
    o_ref[...] = acc.astype(o_ref.dtype)


def _branch(y_rw, y_gm, y_at, w_rw, w_gm, w_at, layer, p_gate, tm, tn):
    M = y_rw.shape[0]
    nb = D_MODEL // tn
    lhs = lambda kdim: pl.BlockSpec((tm, kdim), lambda i, j: (i, 0))
    rhs = lambda kdim: pl.BlockSpec((None, kdim, tn), lambda i, j: (layer, 0, j))
    gate = lambda br: pl.BlockSpec((tm, tn), lambda i, j: (i, br * nb + j))
    return pl.pallas_call(
        _branch_kernel,
        name="branch_merge",
        out_shape=jax.ShapeDtypeStruct((M, D_MODEL), BF16),
        grid=(M // tm, nb),
        in_specs=[lhs(RW_WIDTH), lhs(GM_WIDTH), lhs(ATT_OUT), rhs(RW_WIDTH), rhs(GM_WIDTH), rhs(ATT_OUT),
                  gate(0), gate(1), gate(2)],
        out_specs=pl.BlockSpec((tm, tn), lambda i, j: (i, j)),
        compiler_params=_params(("parallel", "parallel"), V7X_VMEM_LIMIT),
    )(y_rw, y_gm, y_at, w_rw, w_gm, w_at, p_gate, p_gate, p_gate)


def _rope_tables(pos):
    half = ATT_HEAD // 2
    inv = ROPE_THETA ** (-jnp.arange(half, dtype=F32) / half)
    ang = pos.astype(F32)[:, None] * inv[None, :]
    cos, sin = jnp.cos(ang), jnp.sin(ang)
    return jnp.concatenate([cos, cos], -1), jnp.concatenate([-sin, sin], -1)


def _streams(x, B, T, g, dil, col0=0):
    c = col0 + g * ATT_OUT
    return x[:B * T, c:c + ATT_OUT].reshape(B, T // dil, dil * ATT_OUT)


def kernel(x_prompt, x_sample, cache_kv_w128, cache_kv_w512, cache_kv_w2048, state_rwkv, state_rwkv_shift, norm1, w_in, rw_mu, rw_w0, rw_w_up, rw_a0, rw_a_up, rw_g_up, rw_k_k, rw_k_a, rw_r_k, rw_ln_w, rw_ln_b, gm_ln_g, gm_ln_b, gm_ws, gm_bs, att_q_gain, att_k_gain, w_br_rwkv, w_br_gmlp, w_br_attn, w_out, norm2, w_ff1, w_ff2):
    BP, TP, _ = x_prompt.shape
    BS, TS, _ = x_sample.shape
    depth = w_in.shape[0]
    MP, MS = BP * TP, BS * TS
    M = MP + MS
    TM = ROW_TILE
    assert M % (2 * TM) == 0 and MP % 1024 == 0 and TP % SCAN_C == 0 and TS <= SCAN_C
    x = jnp.concatenate([x_prompt.reshape(MP, D_MODEL), x_sample.reshape(MS, D_MODEL)], axis=0)

    pos = jnp.concatenate([jnp.tile(jnp.arange(TP), BP), jnp.tile(PAST_LEN + jnp.arange(TS), BS)])
    cos_t, sin_t = _rope_tables(pos)
    caches_all = [c.reshape(depth, BS, c.shape[2], 2 * ATT_OUT) for c in (cache_kv_w128, cache_kv_w512, cache_kv_w2048)]
    zeros_state = jnp.zeros((BP, NG, GW, GW), F32)
    zeros_shift = jnp.zeros((BP, RW_COLS), F32)
    c_rw, c_gm, c_at = RW_COLS, RW_COLS + 2 * GM_WIDTH, RW_COLS + 2 * GM_WIDTH + 3 * ATT_WIDTH

    kvp = [[] for _ in range(N_DIL)]
    kvs = [[] for _ in range(N_DIL)]
    st_p, st_s, sh_p, sh_s, gmv_s = [], [], [], [], []
    for l in range(depth):
        h = _rmsnorm(x, norm1[l], TM)
        mm_in = functools.partial(_matmul, h, w_in, l, tm=TM * 2, tn=256, tk=D_MODEL)
        p_rw = mm_in(col_off=0, n_cols=RW_COLS, name="proj_rwkv")
        p_gm = mm_in(col_off=c_rw, n_cols=2 * GM_WIDTH, name="proj_gmlp")
        p_at = mm_in(col_off=c_gm, n_cols=3 * ATT_WIDTH, name="proj_attn")
        p_gate = mm_in(col_off=c_at, n_cols=N_BRANCH * D_MODEL, name="proj_gates")

        zpad = jnp.zeros((DECAY_LORA, RW_WIDTH), F32)
        wup_pad = jnp.concatenate([rw_w_up[l], zpad], axis=0)
        aup_pad = jnp.concatenate([zpad, rw_a_up[l]], axis=0)
        prep = functools.partial(_rwkv_prep, p_rw, mu=rw_mu[l], w0=rw_w0[l], wup_pad=wup_pad, a0=rw_a0[l],
                                 aup_pad=aup_pad, gup=rw_g_up[l], k_k=rw_k_k[l].reshape(-1), k_a=rw_k_a[l].reshape(-1))
        scan = functools.partial(_rwkv_scan, ln_w=rw_ln_w[l].reshape(-1), ln_b=rw_ln_b[l].reshape(-1),
                                 r_k=rw_r_k[l].reshape(-1), C=SCAN_C)
        arrs_p = prep(row_off=0, B=BP, T=TP, tm=128, t_out=128, prev=zeros_shift)
        y_rw_p, sT_p = scan(arrs_p, zeros_state)
        arrs_s = prep(row_off=MP, B=BS, T=TS, tm=TS, t_out=SCAN_C, prev=state_rwkv_shift[l])
        y_rw_s, sT_s = scan(arrs_s, _state_to_blockdiag(state_rwkv[l]))
        y_rw_s = y_rw_s.reshape(BS, SCAN_C, RW_WIDTH)[:, :TS].reshape(MS, RW_WIDTH)
        y_rw = jnp.concatenate([y_rw_p, y_rw_s], axis=0)
        st_p.append(_blockdiag_to_state(sT_p))
        st_s.append(_blockdiag_to_state(sT_s))
        sh_p.append(p_rw[:MP].reshape(BP, TP, RW_COLS)[:, -1])
        sh_s.append(p_rw[MP:].reshape(BS, TS, RW_COLS)[:, -1])

        bs_b = jnp.broadcast_to(gm_bs[l][:, :, None], (GM_GROUPS, GM_CHUNK, GM_GROUP_DIM))
        y_gm_p, _ = _gmlp(p_gm, 0, MP, 256, gm_ln_g[l], gm_ln_b[l], gm_ws, l, bs_b, BF16)
        y_gm_s, vn_s = _gmlp(p_gm, MP, MS, TS, gm_ln_g[l], gm_ln_b[l], gm_ws, l, bs_b, F32)
        y_gm = jnp.concatenate([y_gm_p, y_gm_s.astype(BF16)], axis=0)
        gmv_s.append(vn_s.reshape(BS, TS, GM_WIDTH))

        q_rot, k_rot = _attn_prep(p_at, cos_t, sin_t, att_q_gain[l], att_k_gain[l], TM)
        v_all = p_at[:, 2 * ATT_WIDTH:]
        os_, ls_ = [], []
        for g, (window, dil) in enumerate(DIL_PAIRS):
            o, lse = _band_attn(_streams(q_rot, BP, TP, g, dil), _streams(k_rot, BP, TP, g, dil),
                                _streams(p_at, BP, TP, g, dil, 2 * ATT_WIDTH),
                                ncb=1 if dil == 1 else ATT_GROUP_HEADS)
            os_.append(o.reshape(MP, ATT_OUT))
            ls_.append(lse.reshape(MP, ATT_OUT))
            keep = min(window, TP)
            kg = k_rot[:MP].reshape(BP, TP, N_DIL, ATT_GROUP_HEADS, ATT_HEAD)[:, TP - keep:, g]
            vg = v_all[:MP].reshape(BP, TP, N_DIL, ATT_GROUP_HEADS, ATT_HEAD)[:, TP - keep:, g]
            kvp[g].append(jnp.stack([kg, vg], axis=2))
            kgs = k_rot[MP:].reshape(BS, TS, N_DIL, ATT_GROUP_HEADS, ATT_HEAD)[:, :, g]
            vgs = v_all[MP:].reshape(BS, TS, N_DIL, ATT_GROUP_HEADS, ATT_HEAD)[:, :, g]
            kvs[g].append(jnp.stack([kgs, vgs], axis=2))
        y_at_p = _attn_merge(os_, ls_, 1024)
        y_at_s = _attn_sample(q_rot, k_rot, p_at, MP, BS, TS, caches_all, l)
        y_at = jnp.concatenate([y_at_p, y_at_s.astype(BF16)], axis=0)

        merged = _branch(y_rw, y_gm, y_at, w_br_rwkv, w_br_gmlp, w_br_attn, l, p_gate, TM * 2, 256)
        x = _matmul(merged, w_out, l, tm=TM * 2, tn=256, tk=D_MODEL, epilogue="residual", res=x, name="out_proj")
        h2 = _rmsnorm(x, norm2[l], TM)
        act = _matmul(h2, w_ff1, l, tm=TM * 2, tn=256, tk=D_MODEL, epilogue="relu2", out_dtype=BF16, name="ffn_up")
        x = _matmul(act, w_ff2, l, tm=TM * 2, tn=1024, tk=1024, epilogue="residual", res=x, name="ffn_down")

    return (x[:MP].reshape(BP, TP, D_MODEL), x[MP:].reshape(BS, TS, D_MODEL),
            jnp.stack(kvp[0], 0), jnp.stack(kvp[1], 0), jnp.stack(kvp[2], 0),
            jnp.stack(kvs[0], 0), jnp.stack(kvs[1], 0), jnp.stack(kvs[2], 0),
            jnp.stack(st_p, 0), jnp.stack(st_s, 0), jnp.stack(sh_p, 0), jnp.stack(sh_s, 0),
            jnp.stack(gmv_s, 0))
```

```python
import functools

import jax
import jax.numpy as jnp
from jax import lax
from jax.experimental import pallas as pl
from jax.experimental.pallas import tpu as pltpu

F32 = jnp.float32
BF16 = jnp.bfloat16

LANES = 128
V7X_VMEM_LIMIT = 56 * 1024 * 1024

D_MODEL = 4096
RW_HEADS = 24
RW_HEAD = 64
RW_WIDTH = RW_HEADS * RW_HEAD
DECAY_LORA = 64
AAA_LORA = 64
GATE_LORA = 128
RW_COLS = 3 * RW_WIDTH + DECAY_LORA + AAA_LORA + GATE_LORA
GN_EPS = 64e-5
GM_CHUNK = 128
GM_GROUPS = 12
GM_GROUP_DIM = 128
GM_WIDTH = GM_GROUPS * GM_GROUP_DIM
LN_EPS = 1e-5
DIL_PAIRS = ((128, 1), (512, 4), (2048, 16))
N_DIL = 3
ATT_GROUP_HEADS = 4
ATT_HEAD = 128
ATT_WIDTH = N_DIL * ATT_GROUP_HEADS * ATT_HEAD
ATT_OUT = ATT_GROUP_HEADS * ATT_HEAD
ATT_BLK = 128
ATTN_BATCH = 8
ROPE_THETA = 10000.0
N_BRANCH = 3
D_FF = 4 * D_MODEL
NORM_EPS = 1e-6
NEG_INF = -1e30
PAST_LEN = 8192

ROW_TILE = 688

HPG = 4
GW = HPG * RW_HEAD
NG = RW_HEADS // HPG
SCAN_C = 64

NN = (((1,), (0,)), ((), ()))
NT = (((1,), (1,)), ((), ()))


def _dot(a, b, dims=NN):
    return lax.dot_general(a, b, dims, preferred_element_type=F32)


def _dot1(a, b, dims=NN):
    return _dot(a.astype(BF16), b.astype(BF16), dims)


def _split3(a):
    hi = a.astype(BF16)
    r1 = a - hi.astype(F32)
    mid = r1.astype(BF16)
    return hi, mid, (r1 - mid.astype(F32)).astype(BF16)


def _dot_exact_rhs(a, e):
    hi, mid, lo = _split3(a)
    return _dot(hi, e) + (_dot(mid, e) + _dot(lo, e))


def _dot_exact_lhs(e, a):
    hi, mid, lo = _split3(a)
    return _dot(e, hi) + (_dot(e, mid) + _dot(e, lo))


def _params(sem, vmem=None):
    return pltpu.CompilerParams(dimension_semantics=sem, vmem_limit_bytes=vmem)


def _sigmoid(x):
    return 1.0 / (1.0 + jnp.exp(-x))


def _mm_kernel(*refs, nk, epilogue):
    if epilogue == "residual":
        a_ref, b_ref, res_ref, o_ref = refs
    else:
        a_ref, b_ref, o_ref = refs
    dot = lambda: _dot(a_ref[...], b_ref[...].astype(BF16))
    if nk == 1:
        part = dot()
        if epilogue == "relu2":
            part = jnp.square(jnp.maximum(part, 0.0))
        elif epilogue == "residual":
            part = part + res_ref[...]
        o_ref[...] = part.astype(o_ref.dtype)
    else:
        assert epilogue == "residual"
        k = pl.program_id(2)

        @pl.when(k == 0)
        def _():
            o_ref[...] = res_ref[...] + dot()

        @pl.when(k > 0)
        def _():
            o_ref[...] = o_ref[...] + dot()


def _matmul(a, b, layer, *, col_off=0, n_cols=None, tm, tn, tk, epilogue="none", res=None, out_dtype=F32,
            name="matmul"):
    M, K = a.shape
    n_cols = b.shape[2] if n_cols is None else n_cols
    assert M % tm == 0 and n_cols % tn == 0 and K % tk == 0 and col_off % tn == 0
    nk = K // tk
    assert nk == 1 or out_dtype == F32
    off = col_off // tn
    in_specs = [pl.BlockSpec((tm, tk), lambda i, j, k: (i, k)),
                pl.BlockSpec((None, tk, tn), lambda i, j, k: (layer, k, j + off))]
    args = [a, b]
    if epilogue == "residual":
        in_specs.append(pl.BlockSpec((tm, tn), lambda i, j, k: (i, j)))
        args.append(res)
    return pl.pallas_call(
        functools.partial(_mm_kernel, nk=nk, epilogue=epilogue),
        out_shape=jax.ShapeDtypeStruct((M, n_cols), out_dtype),
        grid=(M // tm, n_cols // tn, nk),
        in_specs=in_specs,
        out_specs=pl.BlockSpec((tm, tn), lambda i, j, k: (i, j)),
        compiler_params=_params(("parallel", "parallel", "arbitrary"), V7X_VMEM_LIMIT),
        name=name,
    )(*args)


def _rmsnorm_kernel(x_ref, g_ref, o_ref):
    x = x_ref[...]
    y = x * lax.rsqrt(jnp.mean(x * x, axis=-1, keepdims=True) + NORM_EPS)
    o_ref[...] = (y * g_ref[...]).astype(o_ref.dtype)


def _rmsnorm(x, g, tm):
    M, D = x.shape
    return pl.pallas_call(
        _rmsnorm_kernel,
        name="rmsnorm",
        out_shape=jax.ShapeDtypeStruct((M, D), BF16),
        grid=(M // tm,),
        in_specs=[pl.BlockSpec((tm, D), lambda i: (i, 0)), pl.BlockSpec((1, D), lambda i: (0, 0))],
        out_specs=pl.BlockSpec((tm, D), lambda i: (i, 0)),
        compiler_params=_params(("parallel",), V7X_VMEM_LIMIT),
    )(x, g.reshape(1, D))


def _head_ones(width, head):
    r = lax.broadcasted_iota(jnp.int32, (width, width), 0) // head
    c = lax.broadcasted_iota(jnp.int32, (width, width), 1) // head
    return jnp.where(r == c, 1.0, 0.0).astype(BF16)


def _rwkv_prep_kernel(p_ref, prev_ref, mu_ref, w0_ref, wup_ref, a0_ref, aup_ref, gup_ref, kk_ref, ka_ref,
                      r_o, lw_o, k_o, v_o, kkn_o, b_o, g_o, carry_ref, *, tm, t_out):
    j = pl.program_id(1)

    @pl.when(j == 0)
    def _():
        carry_ref[...] = prev_ref[0]

    p = p_ref[...]
    row = lax.broadcasted_iota(jnp.int32, p.shape, 0)
    p_prev = jnp.where(row == 0, carry_ref[...], pltpu.roll(p, 1, axis=0))
    carry_ref[...] = p[tm - 1:tm, :]
    ps = p + mu_ref[...] * (p_prev - p)

    W = RW_WIDTH
    r = ps[:, 0:W]
    k = ps[:, W:2 * W]
    v = ps[:, 2 * W:3 * W]
    wa = ps[:, 3 * W:3 * W + LANES]
    g_in = ps[:, 3 * W + LANES:3 * W + 2 * LANES]

    z = -(w0_ref[...] + _dot(jnp.tanh(wa).astype(BF16), wup_ref[...].astype(BF16)))
    softplus = jnp.maximum(z, 0.0) + jnp.log1p(jnp.exp(-jnp.abs(z)))
    lw = -jnp.exp(-softplus - 0.5)
    a = _sigmoid(a0_ref[...] + _dot(wa.astype(BF16), aup_ref[...].astype(BF16)))
    g = _dot(_sigmoid(g_in).astype(BF16), gup_ref[...].astype(BF16))

    kk = k * kk_ref[...]
    ones = _head_ones(GW, RW_HEAD)
    kmod = k * (1.0 + (a - 1.0) * ka_ref[...])
    outs = (r_o, lw_o, k_o, v_o, kkn_o, b_o, g_o)
    if t_out > tm:
        for o in outs:
            o[...] = jnp.zeros(o.shape, o.dtype)
    for q in range(NG):
        sl = slice(q * GW, (q + 1) * GW)
        kq = kk[:, sl]
        ssq = _dot_exact_rhs(kq * kq, ones)
        kn = kq * lax.rsqrt(jnp.maximum(ssq, 1e-24))
        vals = (r[:, sl], lw[:, sl], kmod[:, sl], v[:, sl], kn, kn * a[:, sl], g[:, sl])
        for o, val in zip(outs, vals):
            o[0, q, 0:tm, :] = val


def _rwkv_prep(p_rw, row_off, B, T, tm, t_out, prev, mu, w0, wup_pad, a0, aup_pad, gup, k_k, k_a):
    nt = T // tm
    off = row_off // tm
    assert row_off % tm == 0 and T % tm == 0 and t_out >= tm
    row = lambda n: pl.BlockSpec((1, n), lambda b, j: (0, 0))
    out_spec = pl.BlockSpec((1, NG, t_out, GW), lambda b, j: (b, 0, j, 0))
    out = jax.ShapeDtypeStruct((B, NG, nt * t_out, GW), F32)
    return pl.pallas_call(
        functools.partial(_rwkv_prep_kernel, tm=tm, t_out=t_out),
        name="rwkv_prep",
        out_shape=[out] * 7,
        grid=(B, nt),
        in_specs=[pl.BlockSpec((tm, RW_COLS), lambda b, j: (off + b * nt + j, 0)),
                  pl.BlockSpec((1, 1, RW_COLS), lambda b, j: (b, 0, 0)),
                  row(RW_COLS), row(RW_WIDTH),
                  pl.BlockSpec((LANES, RW_WIDTH), lambda b, j: (0, 0)),
                  row(RW_WIDTH),
                  pl.BlockSpec((LANES, RW_WIDTH), lambda b, j: (0, 0)),
                  pl.BlockSpec((GATE_LORA, RW_WIDTH), lambda b, j: (0, 0)),
                  row(RW_WIDTH), row(RW_WIDTH)],
        out_specs=[out_spec] * 7,
        scratch_shapes=[pltpu.VMEM((1, RW_COLS), F32)],
        compiler_params=_params(("parallel", "arbitrary"), V7X_VMEM_LIMIT),
    )(p_rw, prev.reshape(B, 1, RW_COLS), mu.reshape(1, -1), w0.reshape(1, -1), wup_pad, a0.reshape(1, -1),
      aup_pad, gup, k_k.reshape(1, -1), k_a.reshape(1, -1))


def _stack_heads(x, lane_head):
    return jnp.concatenate([jnp.where(lane_head == h, x, 0.0) for h in range(HPG)], axis=0)


def _rwkv_scan_kernel(r_ref, lw_ref, k_ref, v_ref, kk_ref, b_ref, g_ref, s0_ref, lnw_ref, lnb_ref, rk_ref,
                      y_ref, st_ref, s_scr, *, C):
    j = pl.program_id(1)

    @pl.when(j == 0)
    def _():
        s_scr[...] = s0_ref[0]

    R = HPG * C
    G = range(NG)
    lane_head = lax.broadcasted_iota(jnp.int32, (1, GW), 1) // RW_HEAD
    ii = lax.broadcasted_iota(jnp.int32, (R, R), 0)
    jj = lax.broadcasted_iota(jnp.int32, (R, R), 1)
    ti = lax.broadcasted_iota(jnp.int32, (C, C), 0)
    tj = lax.broadcasted_iota(jnp.int32, (C, C), 1)
    tril = jnp.where(ti >= tj, 1.0, 0.0).astype(BF16)
    ones = _head_ones(GW, RW_HEAD)
    strict = ii > jj
    incl = ii >= jj
    sls = [slice(q * GW, (q + 1) * GW) for q in G]

    r = [r_ref[0, q] for q in G]
    k = [k_ref[0, q] for q in G]
    v = [v_ref[0, q] for q in G]
    lw_all = jnp.concatenate([lw_ref[0, q] for q in G], axis=1)
    cum_all = _dot_exact_lhs(tril, lw_all)
    cum = [cum_all[:, sl] for sl in sls]
    pc = [jnp.exp(c) for c in cum]
    pinv = [jnp.exp(-c) for c in cum]
    pend = [p[C - 1:C, :] for p in pc]
    ar = [jnp.concatenate([_stack_heads(-kk_ref[0, q] * jnp.exp(cum[q] - lw_ref[0, q]), lane_head),
                           _stack_heads(r[q] * pc[q], lane_head)], axis=0).astype(BF16) for q in G]
    bk = [jnp.concatenate([_stack_heads(b_ref[0, q] * pinv[q], lane_head),
                           _stack_heads(k[q] * pinv[q], lane_head)], axis=0) for q in G]
    gram = [_dot(ar[q], bk[q].astype(BF16), NT) for q in G]
    a_ab = [jnp.where(strict, gram[q][:R, :R], 0.0) for q in G]
    a_kr = [jnp.concatenate([jnp.where(strict, gram[q][:R, R:], 0.0),
                             jnp.where(incl, gram[q][R:, R:], 0.0)], axis=0).astype(BF16) for q in G]
    a_rb = [jnp.where(incl, gram[q][R:, :R], 0.0).astype(BF16) for q in G]

    eye = jnp.where(ii == jj, 1.0, 0.0)
    t = [eye + jnp.where((ii >> 1) == (jj >> 1), a_ab[q], 0.0) for q in G]
    size = 2
    while size < C:
        sh = size.bit_length() - 1
        sel = ((ii >> (sh + 1)) == (jj >> (sh + 1))) & ((ii >> sh) != (jj >> sh))
        tb = [t[q].astype(BF16) for q in G]
        mid = [_dot(jnp.where(sel, a_ab[q], 0.0).astype(BF16), tb[q]) for q in G]
        t = [t[q] + _dot(tb[q], mid[q].astype(BF16)) for q in G]
        size *= 2

    s = [s_scr[q] for q in G]
    v_st = [_stack_heads(v[q], lane_head) for q in G]
    from_state = [_dot(ar[q], s[q].astype(BF16), NT) for q in G]
    from_v = [_dot(a_kr[q], v_st[q].astype(BF16)) for q in G]
    u = [_dot1(t[q], from_state[q][:R] + from_v[q][:R]) for q in G]
    y_st = [from_state[q][R:] + from_v[q][R:] + _dot(a_rb[q], u[q].astype(BF16)) for q in G]
    for q in G:
        uv = jnp.concatenate([u[q], v_st[q]], axis=0)
        s_scr[q] = s[q] * pend[q] + _dot1(uv.T, bk[q] * pend[q])

    for q in G:
        y = y_st[q][0:C]
        for h in range(1, HPG):
            y = y + y_st[q][h * C:(h + 1) * C]
        mean = _dot_exact_rhs(y, ones) * (1.0 / RW_HEAD)
        yc = y - mean
        var = _dot_exact_rhs(yc * yc, ones) * (1.0 / RW_HEAD)
        sl = sls[q]
        yn = yc * lax.rsqrt(var + GN_EPS) * lnw_ref[:, sl] + lnb_ref[:, sl]
        bonus = _dot_exact_rhs(r[q] * k[q] * rk_ref[:, sl], ones) * v[q]
        y_ref[:, sl] = ((yn + bonus) * g_ref[0, q]).astype(y_ref.dtype)

    @pl.when(j == pl.num_programs(1) - 1)
    def _():
        st_ref[0] = s_scr[...]


def _rwkv_scan(arrs, s0, ln_w, ln_b, r_k, C):
    B, _, T, _ = arrs[0].shape
    nt = T // C
    in_spec = pl.BlockSpec((1, NG, C, GW), lambda b, j: (b, 0, j, 0))
    st_spec = pl.BlockSpec((1, NG, GW, GW), lambda b, j: (b, 0, 0, 0))
    row = pl.BlockSpec((1, RW_WIDTH), lambda b, j: (0, 0))
    return pl.pallas_call(
        functools.partial(_rwkv_scan_kernel, C=C),
        name="rwkv_scan",
        out_shape=[jax.ShapeDtypeStruct((B * T, RW_WIDTH), BF16),
                   jax.ShapeDtypeStruct((B, NG, GW, GW), F32)],
        grid=(B, nt),
        in_specs=[in_spec] * 7 + [st_spec, row, row, row],
        out_specs=[pl.BlockSpec((C, RW_WIDTH), lambda b, j: (b * nt + j, 0)), st_spec],
        scratch_shapes=[pltpu.VMEM((NG, GW, GW), F32)],
        compiler_params=_params(("parallel", "arbitrary"), V7X_VMEM_LIMIT),
    )(*arrs, s0, ln_w.reshape(1, -1), ln_b.reshape(1, -1), r_k.reshape(1, -1))


def _state_to_blockdiag(s):
    B = s.shape[0]
    s = s.reshape(B, NG, HPG, RW_HEAD, RW_HEAD)
    eye = jnp.eye(HPG, dtype=s.dtype)
    bd = s[:, :, :, :, None, :] * eye[None, None, :, None, :, None]
    return bd.reshape(B, NG, GW, GW)


def _blockdiag_to_state(bd):
    B = bd.shape[0]
    x = bd.reshape(B, NG, HPG, RW_HEAD, HPG, RW_HEAD)
    return jnp.stack([x[:, :, h, :, h, :] for h in range(HPG)], axis=2).reshape(B, RW_HEADS, RW_HEAD, RW_HEAD)


def _gelu(x):
    return 0.5 * x * (1.0 + lax.erf(x * (2.0 ** -0.5)))


def _gmlp_kernel(u_ref, v_ref, lng_ref, lnb_ref, ws_ref, bs_ref, y_ref, vn_ref, *, tm):
    u = _gelu(u_ref[...])
    vf = _gelu(v_ref[...])
    mean = jnp.mean(vf, axis=-1, keepdims=True)
    vc = vf - mean
    var = jnp.mean(vc * vc, axis=-1, keepdims=True)
    vn = vc * lax.rsqrt(var + LN_EPS) * lng_ref[...] + lnb_ref[...]
    vn_ref[...] = vn
    ti = lax.broadcasted_iota(jnp.int32, (GM_CHUNK, GM_CHUNK), 0)
    tj = lax.broadcasted_iota(jnp.int32, (GM_CHUNK, GM_CHUNK), 1)
    causal = ti >= tj
    rows = min(tm, GM_CHUNK)
    for g in range(GM_GROUPS):
        wm = jnp.where(causal, ws_ref[g], 0.0).astype(BF16)
        sl = slice(g * GM_GROUP_DIM, (g + 1) * GM_GROUP_DIM)
        for c in range(max(tm // GM_CHUNK, 1)):
            rs = slice(c * GM_CHUNK, c * GM_CHUNK + rows)
            vg = vn[rs, sl].astype(BF16)
            if rows < GM_CHUNK:
                vg = jnp.concatenate([vg, jnp.zeros((GM_CHUNK - rows, GM_GROUP_DIM), BF16)], axis=0)
            mixed = (_dot(wm, vg) + bs_ref[g])[:rows]
            y_ref[rs, sl] = (u[rs, sl] * mixed).astype(y_ref.dtype)


def _gmlp(p_gm, row_off, rows, tm, ln_g, ln_b, ws, layer, bs_b, y_dtype):
    off = row_off // tm
    assert row_off % tm == 0 and rows % tm == 0
    row = pl.BlockSpec((1, GM_WIDTH), lambda i: (0, 0))
    full3 = pl.BlockSpec((GM_GROUPS, GM_CHUNK, GM_CHUNK), lambda i: (0, 0, 0))
    ws_spec = pl.BlockSpec((None, GM_GROUPS, GM_CHUNK, GM_CHUNK), lambda i: (layer, 0, 0, 0))
    return pl.pallas_call(
        functools.partial(_gmlp_kernel, tm=tm),
        name="gmlp",
        out_shape=[jax.ShapeDtypeStruct((rows, GM_WIDTH), y_dtype), jax.ShapeDtypeStruct((rows, GM_WIDTH), F32)],
        grid=(rows // tm,),
        in_specs=[pl.BlockSpec((tm, GM_WIDTH), lambda i: (off + i, 0)),
                  pl.BlockSpec((tm, GM_WIDTH), lambda i: (off + i, 1)),
                  row, row, ws_spec, full3],
        out_specs=[pl.BlockSpec((tm, GM_WIDTH), lambda i: (i, 0))] * 2,
        compiler_params=_params(("parallel",), V7X_VMEM_LIMIT),
    )(p_gm, p_gm, ln_g.reshape(1, -1), ln_b.reshape(1, -1), ws, bs_b)


def _attn_prep_kernel(q_ref, k_ref, cos_ref, sin_ref, qg_ref, kg_ref, qo_ref, ko_ref):
    cos = cos_ref[...]
    sin = sin_ref[...]

    def norm_rope(x, gain):
        y = x * lax.rsqrt(jnp.mean(x * x, axis=-1, keepdims=True) + NORM_EPS) * gain
        return y * cos + pltpu.roll(y, ATT_HEAD // 2, axis=1) * sin

    for h in range(N_DIL * ATT_GROUP_HEADS):
        sl = slice(h * ATT_HEAD, (h + 1) * ATT_HEAD)
        qo_ref[:, sl] = norm_rope(q_ref[:, sl], qg_ref[...]).astype(qo_ref.dtype)
        ko_ref[:, sl] = norm_rope(k_ref[:, sl], kg_ref[...]).astype(ko_ref.dtype)


def _attn_prep(p_at, cos, sin, q_gain, k_gain, tm):
    M = p_at.shape[0]
    blk = lambda c: pl.BlockSpec((tm, ATT_WIDTH), lambda i: (i, c))
    tab = pl.BlockSpec((tm, ATT_HEAD), lambda i: (i, 0))
    gain = pl.BlockSpec((1, ATT_HEAD), lambda i: (0, 0))
    return pl.pallas_call(
        _attn_prep_kernel,
        name="attn_prep",
        out_shape=[jax.ShapeDtypeStruct((M, ATT_WIDTH), F32), jax.ShapeDtypeStruct((M, ATT_WIDTH), F32)],
        grid=(M // tm,),
        in_specs=[blk(0), blk(1), tab, tab, gain, gain],
        out_specs=[pl.BlockSpec((tm, ATT_WIDTH), lambda i: (i, 0))] * 2,
        compiler_params=_params(("parallel",), V7X_VMEM_LIMIT),
    )(p_at, p_at, cos, sin, q_gain.reshape(1, -1), k_gain.reshape(1, -1))


def _attn_prompt_kernel(q0, q1, q2, k0, k1, k2, v0, v1, v2, y_ref, o_scr, lse_scr, *, T):
    scale = ATT_HEAD ** -0.5
    qi = lax.broadcasted_iota(jnp.int32, (ATT_BLK, ATT_BLK), 0)
    kj = lax.broadcasted_iota(jnp.int32, (ATT_BLK, ATT_BLK), 1)
    refs = ((q0, k0, v0), (q1, k1, v1), (q2, k2, v2))
    for g, (window, dil) in enumerate(DIL_PAIRS):
        q_ref, k_ref, v_ref = refs[g]
        nb = T // dil // ATT_BLK

        def rows(r, n, dil=dil):
            if dil == 1:
                return pl.ds(n * ATT_BLK, ATT_BLK)
            return pl.ds(r + n * ATT_BLK * dil, ATT_BLK, stride=dil)

        blocks = [(rows(r, n), rows(r, n - 1) if n > 0 else None) for r in range(dil) for n in range(nb)]
        for b0 in range(0, len(blocks), ATTN_BATCH):
            batch = blocks[b0:b0 + ATTN_BATCH]
            q = [q_ref[rs, :].astype(BF16) for rs, _ in batch]
            s_c = [jnp.where(kj <= qi, _dot(q[i], k_ref[rs, :].astype(BF16), NT) * scale, NEG_INF)
                   for i, (rs, _) in enumerate(batch)]
            s_p = [None if ps_ is None else
                   jnp.where(kj >= qi, _dot(q[i], k_ref[ps_, :].astype(BF16), NT) * scale, NEG_INF)
                   for i, (_, ps_) in enumerate(batch)]
            m = [jnp.max(s, axis=-1, keepdims=True) for s in s_c]
            m = [mc if sp is None else jnp.maximum(mc, jnp.max(sp, axis=-1, keepdims=True))
                 for mc, sp in zip(m, s_p)]
            e_c = [jnp.exp(s - mm) for s, mm in zip(s_c, m)]
            e_p = [None if sp is None else jnp.exp(sp - mm) for sp, mm in zip(s_p, m)]
            den = [jnp.sum(e, axis=-1, keepdims=True) for e in e_c]
            den = [d if e is None else d + jnp.sum(e, axis=-1, keepdims=True) for d, e in zip(den, e_p)]
            acc = [_dot(e.astype(BF16), v_ref[rs, :].astype(BF16)) for e, (rs, _) in zip(e_c, batch)]
            acc = [a if e is None else a + _dot(e.astype(BF16), v_ref[ps_, :].astype(BF16))
                   for a, e, (_, ps_) in zip(acc, e_p, batch)]
            for i, (rs, _) in enumerate(batch):
                o_scr[g, rs, :] = acc[i] / den[i]
                lse_scr[g, rs, :] = jnp.broadcast_to(m[i] + jnp.log(den[i]), (ATT_BLK, ATT_HEAD))
    a, b, c = lse_scr[0], lse_scr[1], lse_scr[2]
    m = jnp.maximum(jnp.maximum(a, b), c)
    wa, wb, wc = jnp.exp(a - m), jnp.exp(b - m), jnp.exp(c - m)
    y = (wa * o_scr[0] + wb * o_scr[1] + wc * o_scr[2]) / (wa + wb + wc)
    y_ref[...] = y.astype(y_ref.dtype)


def _attn_prompt(q_rot, k_rot, p_at, B, T):
    nh = ATT_GROUP_HEADS
    col = lambda g, base: pl.BlockSpec((T, ATT_HEAD), lambda b, h: (b, base + g * nh + h))
    vbase = 2 * ATT_WIDTH // ATT_HEAD
    return pl.pallas_call(
        functools.partial(_attn_prompt_kernel, T=T),
        name="attn_prompt",
        out_shape=jax.ShapeDtypeStruct((B * T, ATT_OUT), BF16),
        grid=(B, nh),
        in_specs=[col(g, 0) for g in range(N_DIL)] * 2 + [col(g, vbase) for g in range(N_DIL)],
        out_specs=pl.BlockSpec((T, ATT_HEAD), lambda b, h: (b, h)),
        scratch_shapes=[pltpu.VMEM((N_DIL, T, ATT_HEAD), F32)] * 2,
        compiler_params=_params(("parallel", "parallel"), V7X_VMEM_LIMIT),
    )(q_rot, q_rot, q_rot, k_rot, k_rot, k_rot, p_at, p_at, p_at)


def _attn_sample_kernel(q_ref, k_ref, v_ref, c0_ref, c1_ref, c2_ref, y_ref, *, T):
    scale = ATT_HEAD ** -0.5
    caches = (c0_ref, c1_ref, c2_ref)
    zpad = jnp.zeros((LANES - T, ATT_HEAD), F32)
    outs = [[None] * N_DIL for _ in range(ATT_GROUP_HEADS)]
    lses = [[None] * N_DIL for _ in range(ATT_GROUP_HEADS)]
    for g, (window, dil) in enumerate(DIL_PAIRS):
        cref = caches[g]
        wb = cref.shape[1]
        nkeys = wb + LANES
        t = lax.broadcasted_iota(jnp.int32, (T, nkeys), 0)
        jrow = lax.broadcasted_iota(jnp.int32, (T, nkeys), 1)
        dist = wb + t - jrow
        valid = (dist >= 0) & (dist <= window) & ((dist & (dil - 1)) == 0)
        for h in range(ATT_GROUP_HEADS):
            sl = slice((g * ATT_GROUP_HEADS + h) * ATT_HEAD, (g * ATT_GROUP_HEADS + h + 1) * ATT_HEAD)
            q = q_ref[:, sl].astype(BF16)
            kcat = jnp.concatenate([cref[0, :, h * ATT_HEAD:(h + 1) * ATT_HEAD],
                                    k_ref[:, sl], zpad], axis=0).astype(BF16)
            vcat = jnp.concatenate([cref[0, :, ATT_OUT + h * ATT_HEAD:ATT_OUT + (h + 1) * ATT_HEAD],
                                    v_ref[:, sl], zpad], axis=0).astype(BF16)
            s = jnp.where(valid, _dot(q, kcat, NT) * scale, NEG_INF)
            m = jnp.max(s, axis=-1, keepdims=True)
            lse = m + jnp.log(jnp.sum(jnp.exp(s - m), axis=-1, keepdims=True))
            outs[h][g] = _dot(jnp.exp(s - lse).astype(BF16), vcat)
            lses[h][g] = lse
    for h in range(ATT_GROUP_HEADS):
        m = jnp.maximum(jnp.maximum(lses[h][0], lses[h][1]), lses[h][2])
        w = [jnp.exp(l - m) for l in lses[h]]
        y = (w[0] * outs[h][0] + w[1] * outs[h][1] + w[2] * outs[h][2]) / (w[0] + w[1] + w[2])
        y_ref[:, h * ATT_HEAD:(h + 1) * ATT_HEAD] = y.astype(y_ref.dtype)


def _attn_sample(q_rot, k_rot, p_at, row_off, B, T, caches, layer):
    off = row_off // T
    blk = lambda c: pl.BlockSpec((T, ATT_WIDTH), lambda b: (off + b, c))
    cspec = lambda a: pl.BlockSpec((None, 1) + a.shape[2:], lambda b: (layer, b, 0, 0))
    return pl.pallas_call(
        functools.partial(_attn_sample_kernel, T=T),
        name="attn_sample",
        out_shape=jax.ShapeDtypeStruct((B * T, ATT_OUT), F32),
        grid=(B,),
        in_specs=[blk(0), blk(0), blk(2)] + [cspec(c) for c in caches],
        out_specs=pl.BlockSpec((T, ATT_OUT), lambda b: (b, 0)),
        compiler_params=_params(("parallel",), V7X_VMEM_LIMIT),
    )(q_rot, k_rot, p_at, *caches)


def _branch_kernel(yr_ref, yg_ref, ya_ref, wr_ref, wg_ref, wa_ref, g0_ref, g1_ref, g2_ref, o_ref):
    acc = _sigmoid(g0_ref[...]) * _dot(yr_ref[...], wr_ref[...].astype(BF16))
    acc = acc + _sigmoid(g1_ref[...]) * _dot(yg_ref[...], wg_ref[...].astype(BF16))
    acc = acc + _sigmoid(g2_ref[...]) * _dot(ya_ref[...], wa_ref[...].astype(BF16))
    o_ref[...] = acc.astype(o_ref.dtype)


def _branch(y_rw, y_gm, y_at, w_rw, w_gm, w_at, layer, p_gate, tm, tn):
    M = y_rw.shape[0]
    nb = D_MODEL // tn
    lhs = lambda kdim: pl.BlockSpec((tm, kdim), lambda i, j: (i, 0))
    rhs = lambda kdim: pl.BlockSpec((None, kdim, tn), lambda i, j: (layer, 0, j))
    gate = lambda br: pl.BlockSpec((tm, tn), lambda i, j: (i, br * nb + j))
    return pl.pallas_call(
        _branch_kernel,
        name="branch_merge",
        out_shape=jax.ShapeDtypeStruct((M, D_MODEL), BF16),
        grid=(M // tm, nb),
        in_specs=[lhs(RW_WIDTH), lhs(GM_WIDTH), lhs(ATT_OUT), rhs(RW_WIDTH), rhs(GM_WIDTH), rhs(ATT_OUT),
                  gate(0), gate(1), gate(2)],
        out_specs=pl.BlockSpec((tm, tn), lambda i, j: (i, j)),
        compiler_params=_params(("parallel", "parallel"), V7X_VMEM_LIMIT),
    )(y_rw, y_gm, y_at, w_rw, w_gm, w_at, p_gate, p_gate, p_gate)


def _rope_tables(pos):
    half = ATT_HEAD // 2
    inv = ROPE_THETA ** (-jnp.arange(half, dtype=F32) / half)
    ang = pos.astype(F32)[:, None] * inv[None, :]
    cos, sin = jnp.cos(ang), jnp.sin(ang)
    return jnp.concatenate([cos, cos], -1), jnp.concatenate([-sin, sin], -1)


def _kv_rows(k_rot, p_at, row0, B, T, keep, g):
    def last(x, c):
        x = lax.slice(x, (row0, c), (row0 + B * T, c + ATT_OUT)).reshape(B, T, ATT_OUT)[:, T - keep:]
        return x.reshape(B, keep, ATT_GROUP_HEADS, ATT_HEAD)
    return jnp.stack([last(k_rot, g * ATT_OUT), last(p_at, 2 * ATT_WIDTH + g * ATT_OUT)], axis=2)


def kernel(x_prompt, x_sample, cache_kv_w128, cache_kv_w512, cache_kv_w2048, state_rwkv, state_rwkv_shift, norm1, w_in, rw_mu, rw_w0, rw_w_up, rw_a0, rw_a_up, rw_g_up, rw_k_k, rw_k_a, rw_r_k, rw_ln_w, rw_ln_b, gm_ln_g, gm_ln_b, gm_ws, gm_bs, att_q_gain, att_k_gain, w_br_rwkv, w_br_gmlp, w_br_attn, w_out, norm2, w_ff1, w_ff2):
    BP, TP, _ = x_prompt.shape
    BS, TS, _ = x_sample.shape
    depth = w_in.shape[0]
    MP, MS = BP * TP, BS * TS
    M = MP + MS
    TM = ROW_TILE
    assert M % (2 * TM) == 0 and MP % 1024 == 0 and TP % SCAN_C == 0 and TS <= SCAN_C
    x = jnp.concatenate([x_prompt.reshape(MP, D_MODEL), x_sample.reshape(MS, D_MODEL)], axis=0)

    pos = jnp.concatenate([jnp.tile(jnp.arange(TP), BP), jnp.tile(PAST_LEN + jnp.arange(TS), BS)])
    cos_t, sin_t = _rope_tables(pos)
    caches_all = [c.reshape(depth, BS, c.shape[2], 2 * ATT_OUT) for c in (cache_kv_w128, cache_kv_w512, cache_kv_w2048)]
    zeros_state = jnp.zeros((BP, NG, GW, GW), F32)
    zeros_shift = jnp.zeros((BP, RW_COLS), F32)
    c_rw, c_gm, c_at = RW_COLS, RW_COLS + 2 * GM_WIDTH, RW_COLS + 2 * GM_WIDTH + 3 * ATT_WIDTH

    kvp = [[] for _ in range(N_DIL)]
    kvs = [[] for _ in range(N_DIL)]
    st_p, st_s, sh_p, sh_s, gmv_s = [], [], [], [], []
    for l in range(depth):
        h = _rmsnorm(x, norm1[l], TM)
        mm_in = functools.partial(_matmul, h, w_in, l, tm=TM * 2, tn=256, tk=D_MODEL)
        p_rw = mm_in(col_off=0, n_cols=RW_COLS, name="proj_rwkv")
        p_gm = mm_in(col_off=c_rw, n_cols=2 * GM_WIDTH, name="proj_gmlp")
        p_at = mm_in(col_off=c_gm, n_cols=3 * ATT_WIDTH, name="proj_attn")
        p_gate = mm_in(col_off=c_at, n_cols=N_BRANCH * D_MODEL, name="proj_gates")

        zpad = jnp.zeros((DECAY_LORA, RW_WIDTH), F32)
        wup_pad = jnp.concatenate([rw_w_up[l], zpad], axis=0)
        aup_pad = jnp.concatenate([zpad, rw_a_up[l]], axis=0)
        prep = functools.partial(_rwkv_prep, p_rw, mu=rw_mu[l], w0=rw_w0[l], wup_pad=wup_pad, a0=rw_a0[l],
                                 aup_pad=aup_pad, gup=rw_g_up[l], k_k=rw_k_k[l].reshape(-1), k_a=rw_k_a[l].reshape(-1))
        scan = functools.partial(_rwkv_scan, ln_w=rw_ln_w[l].reshape(-1), ln_b=rw_ln_b[l].reshape(-1),
                                 r_k=rw_r_k[l].reshape(-1), C=SCAN_C)
        arrs_p = prep(row_off=0, B=BP, T=TP, tm=128, t_out=128, prev=zeros_shift)
        y_rw_p, sT_p = scan(arrs_p, zeros_state)
        arrs_s = prep(row_off=MP, B=BS, T=TS, tm=TS, t_out=SCAN_C, prev=state_rwkv_shift[l])
        y_rw_s, sT_s = scan(arrs_s, _state_to_blockdiag(state_rwkv[l]))
        y_rw_s = y_rw_s.reshape(BS, SCAN_C, RW_WIDTH)[:, :TS].reshape(MS, RW_WIDTH)
        y_rw = jnp.concatenate([y_rw_p, y_rw_s], axis=0)
        st_p.append(_blockdiag_to_state(sT_p))
        st_s.append(_blockdiag_to_state(sT_s))
        sh_p.append(p_rw[TP - 1:MP:TP])
        sh_s.append(p_rw[MP + TS - 1::TS])

        bs_b = jnp.broadcast_to(gm_bs[l][:, :, None], (GM_GROUPS, GM_CHUNK, GM_GROUP_DIM))
        y_gm_p, _ = _gmlp(p_gm, 0, MP, 256, gm_ln_g[l], gm_ln_b[l], gm_ws, l, bs_b, BF16)
        y_gm_s, vn_s = _gmlp(p_gm, MP, MS, TS, gm_ln_g[l], gm_ln_b[l], gm_ws, l, bs_b, F32)
        y_gm = jnp.concatenate([y_gm_p, y_gm_s.astype(BF16)], axis=0)
        gmv_s.append(vn_s.reshape(BS, TS, GM_WIDTH))

        q_rot, k_rot = _attn_prep(p_at, cos_t, sin_t, att_q_gain[l], att_k_gain[l], TM)
        for g, (window, dil) in enumerate(DIL_PAIRS):
            keep = min(window, TP)
            kvp[g].append(_kv_rows(k_rot, p_at, 0, BP, TP, keep, g))
            kvs[g].append(_kv_rows(k_rot, p_at, MP, BS, TS, TS, g))
        y_at_p = _attn_prompt(q_rot, k_rot, p_at, BP, TP)
        y_at_s = _attn_sample(q_rot, k_rot, p_at, MP, BS, TS, caches_all, l)
        y_at = jnp.concatenate([y_at_p, y_at_s.astype(BF16)], axis=0)

        merged = _branch(y_rw, y_gm, y_at, w_br_rwkv, w_br_gmlp, w_br_attn, l, p_gate, TM * 2, 256)
        x = _matmul(merged, w_out, l, tm=TM * 2, tn=256, tk=D_MODEL, epilogue="residual", res=x, name="out_proj")
        h2 = _rmsnorm(x, norm2[l], TM)
        act = _matmul(h2, w_ff1, l, tm=TM * 2, tn=256, tk=D_MODEL, epilogue="relu2", out_dtype=BF16, name="ffn_up")
        x = _matmul(act, w_ff2, l, tm=TM * 2, tn=1024, tk=1024, epilogue="residual", res=x, name="ffn_down")

    return (x[:MP].reshape(BP, TP, D_MODEL), x[MP:].reshape(BS, TS, D_MODEL),
            jnp.stack(kvp[0], 0), jnp.stack(kvp[1], 0), jnp.stack(kvp[2], 0),
            jnp.stack(kvs[0], 0), jnp.stack(kvs[1], 0), jnp.stack(kvs[2], 0),
            jnp.stack(st_p, 0), jnp.stack(st_s, 0), jnp.stack(sh_p, 0), jnp.stack(sh_s, 0),
            jnp.stack(gmv_s, 0))
```

```python
import functools

import jax
import jax.numpy as jnp
from jax import lax
from jax.experimental import pallas as pl
from jax.experimental.pallas import tpu as pltpu

F32 = jnp.float32
BF16 = jnp.bfloat16

LANES = 128
V7X_VMEM_LIMIT = 56 * 1024 * 1024

D_MODEL = 4096
RW_HEADS = 24
RW_HEAD = 64
RW_WIDTH = RW_HEADS * RW_HEAD
DECAY_LORA = 64
AAA_LORA = 64
GATE_LORA = 128
RW_COLS = 3 * RW_WIDTH + DECAY_LORA + AAA_LORA + GATE_LORA
GN_EPS = 64e-5
GM_CHUNK = 128
GM_GROUPS = 12
GM_GROUP_DIM = 128
GM_WIDTH = GM_GROUPS * GM_GROUP_DIM
LN_EPS = 1e-5
DIL_PAIRS = ((128, 1), (512, 4), (2048, 16))
N_DIL = 3
ATT_GROUP_HEADS = 4
ATT_HEAD = 128
ATT_WIDTH = N_DIL * ATT_GROUP_HEADS * ATT_HEAD
ATT_OUT = ATT_GROUP_HEADS * ATT_HEAD
ATT_BLK = 128
ATTN_BATCH = 8
ROPE_THETA = 10000.0
N_BRANCH = 3
D_FF = 4 * D_MODEL
NORM_EPS = 1e-6
NEG_INF = -1e30
PAST_LEN = 8192

ROW_TILE = 688

HPG = 4
GW = HPG * RW_HEAD
NG = RW_HEADS // HPG
SCAN_C = 64

NN = (((1,), (0,)), ((), ()))
NT = (((1,), (1,)), ((), ()))


def _dot(a, b, dims=NN):
    return lax.dot_general(a, b, dims, preferred_element_type=F32)


def _dot1(a, b, dims=NN):
    return _dot(a.astype(BF16), b.astype(BF16), dims)


def _split3(a):
    hi = a.astype(BF16)
    r1 = a - hi.astype(F32)
    mid = r1.astype(BF16)
    return hi, mid, (r1 - mid.astype(F32)).astype(BF16)


def _dot_exact_rhs(a, e):
    hi, mid, lo = _split3(a)
    return _dot(hi, e) + (_dot(mid, e) + _dot(lo, e))


def _dot_exact_lhs(e, a):
    hi, mid, lo = _split3(a)
    return _dot(e, hi) + (_dot(e, mid) + _dot(e, lo))


def _params(sem, vmem=None):
    return pltpu.CompilerParams(dimension_semantics=sem, vmem_limit_bytes=vmem)


def _sigmoid(x):
    return 1.0 / (1.0 + jnp.exp(-x))


def _mm_kernel(*refs, nk, epilogue, scaled, normed):
    refs = list(refs)
    a_ref, b_ref = refs.pop(0), refs.pop(0)
    ssq_ref = refs.pop(0) if scaled else None
    res_ref = refs.pop(0) if epilogue == "residual" else None
    gain_ref = refs.pop(0) if normed else None
    o_ref = refs.pop(0)
    dot = lambda: _dot(a_ref[...], b_ref[...].astype(BF16))
    j = pl.program_id(1)

    def emit_normed(x_new):
        xg_ref, ssq_out = refs
        xg_ref[...] = (x_new * gain_ref[...]).astype(xg_ref.dtype)
        row = jnp.broadcast_to(jnp.sum(x_new * x_new, axis=-1, keepdims=True), ssq_out.shape)

        @pl.when(j == 0)
        def _():
            ssq_out[...] = row

        @pl.when(j > 0)
        def _():
            ssq_out[...] += row

    if nk == 1:
        part = dot()
        if scaled:
            part = part * lax.rsqrt(ssq_ref[:, 0:1] * (1.0 / D_MODEL) + NORM_EPS)
        if epilogue == "relu2":
            part = jnp.square(jnp.maximum(part, 0.0))
        elif epilogue == "residual":
            part = part + res_ref[...]
        o_ref[...] = part.astype(o_ref.dtype)
        if normed:
            emit_normed(part)
    else:
        assert epilogue == "residual" and not scaled
        k = pl.program_id(2)

        @pl.when(k == 0)
        def _():
            o_ref[...] = res_ref[...] + dot()

        @pl.when(k > 0)
        def _():
            o_ref[...] = o_ref[...] + dot()

        if normed:
            @pl.when(k == nk - 1)
            def _():
                emit_normed(o_ref[...])


def _matmul(a, b, layer, *, col_off=0, n_cols=None, tm, tn, tk, epilogue="none", res=None, out_dtype=F32,
            row_ssq=None, next_gain=None, name="matmul"):
    M, K = a.shape
    n_cols = b.shape[2] if n_cols is None else n_cols
    assert M % tm == 0 and n_cols % tn == 0 and K % tk == 0 and col_off % tn == 0
    nk = K // tk
    assert nk == 1 or out_dtype == F32
    off = col_off // tn
    scaled, normed = row_ssq is not None, next_gain is not None
    in_specs = [pl.BlockSpec((tm, tk), lambda i, j, k: (i, k)),
                pl.BlockSpec((None, tk, tn), lambda i, j, k: (layer, k, j + off))]
    args = [a, b]
    tile = pl.BlockSpec((tm, tn), lambda i, j, k: (i, j))
    stat = pl.BlockSpec((tm, LANES), lambda i, j, k: (i, 0))
    if scaled:
        in_specs.append(stat)
        args.append(row_ssq)
    if epilogue == "residual":
        in_specs.append(tile)
        args.append(res)
    out_shape = [jax.ShapeDtypeStruct((M, n_cols), out_dtype)]
    out_specs = [tile]
    if normed:
        in_specs.append(pl.BlockSpec((1, tn), lambda i, j, k: (0, j)))
        args.append(next_gain.reshape(1, n_cols))
        out_shape += [jax.ShapeDtypeStruct((M, n_cols), BF16), jax.ShapeDtypeStruct((M, LANES), F32)]
        out_specs += [tile, stat]
    outs = pl.pallas_call(
        functools.partial(_mm_kernel, nk=nk, epilogue=epilogue, scaled=scaled, normed=normed),
        out_shape=out_shape,
        grid=(M // tm, n_cols // tn, nk),
        in_specs=in_specs,
        out_specs=out_specs,
        compiler_params=_params(("parallel", "arbitrary" if normed else "parallel", "arbitrary"), V7X_VMEM_LIMIT),
        name=name,
    )(*args)
    return outs if normed else outs[0]


def _norm_operand_kernel(x_ref, g_ref, xg_ref, ssq_ref):
    x = x_ref[...]
    xg_ref[...] = (x * g_ref[...]).astype(xg_ref.dtype)
    ssq_ref[...] = jnp.broadcast_to(jnp.sum(x * x, axis=-1, keepdims=True), ssq_ref.shape)


def _norm_operand(x, g, tm):
    M, D = x.shape
    return pl.pallas_call(
        _norm_operand_kernel,
        name="norm_operand",
        out_shape=[jax.ShapeDtypeStruct((M, D), BF16), jax.ShapeDtypeStruct((M, LANES), F32)],
        grid=(M // tm,),
        in_specs=[pl.BlockSpec((tm, D), lambda i: (i, 0)), pl.BlockSpec((1, D), lambda i: (0, 0))],
        out_specs=[pl.BlockSpec((tm, D), lambda i: (i, 0)), pl.BlockSpec((tm, LANES), lambda i: (i, 0))],
        compiler_params=_params(("parallel",), V7X_VMEM_LIMIT),
    )(x, g.reshape(1, D))


def _head_ones(width, head):
    r = lax.broadcasted_iota(jnp.int32, (width, width), 0) // head
    c = lax.broadcasted_iota(jnp.int32, (width, width), 1) // head
    return jnp.where(r == c, 1.0, 0.0).astype(BF16)


def _rwkv_prep_kernel(p_ref, prev_ref, mu_ref, w0_ref, wup_ref, a0_ref, aup_ref, gup_ref, kk_ref, ka_ref,
                      r_o, lw_o, k_o, v_o, kkn_o, b_o, g_o, carry_ref, *, tm, t_out):
    j = pl.program_id(1)

    @pl.when(j == 0)
    def _():
        carry_ref[...] = prev_ref[0]

    p = p_ref[...]
    row = lax.broadcasted_iota(jnp.int32, p.shape, 0)
    p_prev = jnp.where(row == 0, carry_ref[...], pltpu.roll(p, 1, axis=0))
    carry_ref[...] = p[tm - 1:tm, :]
    ps = p + mu_ref[...] * (p_prev - p)

    W = RW_WIDTH
    r = ps[:, 0:W]
    k = ps[:, W:2 * W]
    v = ps[:, 2 * W:3 * W]
    wa = ps[:, 3 * W:3 * W + LANES]
    g_in = ps[:, 3 * W + LANES:3 * W + 2 * LANES]

    z = -(w0_ref[...] + _dot(jnp.tanh(wa).astype(BF16), wup_ref[...].astype(BF16)))
    softplus = jnp.maximum(z, 0.0) + jnp.log1p(jnp.exp(-jnp.abs(z)))
    lw = -jnp.exp(-softplus - 0.5)
    a = _sigmoid(a0_ref[...] + _dot(wa.astype(BF16), aup_ref[...].astype(BF16)))
    g = _dot(_sigmoid(g_in).astype(BF16), gup_ref[...].astype(BF16))

    kk = k * kk_ref[...]
    ones = _head_ones(GW, RW_HEAD)
    kmod = k * (1.0 + (a - 1.0) * ka_ref[...])
    outs = (r_o, lw_o, k_o, v_o, kkn_o, b_o, g_o)
    if t_out > tm:
        for o in outs:
            o[...] = jnp.zeros(o.shape, o.dtype)
    for q in range(NG):
        sl = slice(q * GW, (q + 1) * GW)
        kq = kk[:, sl]
        ssq = _dot_exact_rhs(kq * kq, ones)
        kn = kq * lax.rsqrt(jnp.maximum(ssq, 1e-24))
        vals = (r[:, sl], lw[:, sl], kmod[:, sl], v[:, sl], kn, kn * a[:, sl], g[:, sl])
        for o, val in zip(outs, vals):
            o[0, q, 0:tm, :] = val


def _rwkv_prep(p_rw, row_off, B, T, tm, t_out, prev, mu, w0, wup_pad, a0, aup_pad, gup, k_k, k_a):
    nt = T // tm
    off = row_off // tm
    assert row_off % tm == 0 and T % tm == 0 and t_out >= tm
    row = lambda n: pl.BlockSpec((1, n), lambda b, j: (0, 0))
    out_spec = pl.BlockSpec((1, NG, t_out, GW), lambda b, j: (b, 0, j, 0))
    out = jax.ShapeDtypeStruct((B, NG, nt * t_out, GW), F32)
    return pl.pallas_call(
        functools.partial(_rwkv_prep_kernel, tm=tm, t_out=t_out),
        name="rwkv_prep",
        out_shape=[out] * 7,
        grid=(B, nt),
        in_specs=[pl.BlockSpec((tm, RW_COLS), lambda b, j: (off + b * nt + j, 0)),
                  pl.BlockSpec((1, 1, RW_COLS), lambda b, j: (b, 0, 0)),
                  row(RW_COLS), row(RW_WIDTH),
                  pl.BlockSpec((LANES, RW_WIDTH), lambda b, j: (0, 0)),
                  row(RW_WIDTH),
                  pl.BlockSpec((LANES, RW_WIDTH), lambda b, j: (0, 0)),
                  pl.BlockSpec((GATE_LORA, RW_WIDTH), lambda b, j: (0, 0)),
                  row(RW_WIDTH), row(RW_WIDTH)],
        out_specs=[out_spec] * 7,
        scratch_shapes=[pltpu.VMEM((1, RW_COLS), F32)],
        compiler_params=_params(("parallel", "arbitrary"), V7X_VMEM_LIMIT),
    )(p_rw, prev.reshape(B, 1, RW_COLS), mu.reshape(1, -1), w0.reshape(1, -1), wup_pad, a0.reshape(1, -1),
      aup_pad, gup, k_k.reshape(1, -1), k_a.reshape(1, -1))


def _stack_heads(x, lane_head):
    return jnp.concatenate([jnp.where(lane_head == h, x, 0.0) for h in range(HPG)], axis=0)


def _rwkv_scan_kernel(r_ref, lw_ref, k_ref, v_ref, kk_ref, b_ref, g_ref, s0_ref, lnw_ref, lnb_ref, rk_ref,
                      y_ref, st_ref, s_scr, *, C):
    j = pl.program_id(1)

    @pl.when(j == 0)
    def _():
        s_scr[...] = s0_ref[0]

    R = HPG * C
    G = range(NG)
    lane_head = lax.broadcasted_iota(jnp.int32, (1, GW), 1) // RW_HEAD
    ii = lax.broadcasted_iota(jnp.int32, (R, R), 0)
    jj = lax.broadcasted_iota(jnp.int32, (R, R), 1)
    ti = lax.broadcasted_iota(jnp.int32, (C, C), 0)
    tj = lax.broadcasted_iota(jnp.int32, (C, C), 1)
    tril = jnp.where(ti >= tj, 1.0, 0.0).astype(BF16)
    ones = _head_ones(GW, RW_HEAD)
    strict = ii > jj
    incl = ii >= jj
    sls = [slice(q * GW, (q + 1) * GW) for q in G]

    r = [r_ref[0, q] for q in G]
    k = [k_ref[0, q] for q in G]
    v = [v_ref[0, q] for q in G]
    lw_all = jnp.concatenate([lw_ref[0, q] for q in G], axis=1)
    cum_all = _dot_exact_lhs(tril, lw_all)
    cum = [cum_all[:, sl] for sl in sls]
    pc = [jnp.exp(c) for c in cum]
    pinv = [jnp.exp(-c) for c in cum]
    pend = [p[C - 1:C, :] for p in pc]
    ar = [jnp.concatenate([_stack_heads(-kk_ref[0, q] * jnp.exp(cum[q] - lw_ref[0, q]), lane_head),
                           _stack_heads(r[q] * pc[q], lane_head)], axis=0).astype(BF16) for q in G]
    bk = [jnp.concatenate([_stack_heads(b_ref[0, q] * pinv[q], lane_head),
                           _stack_heads(k[q] * pinv[q], lane_head)], axis=0) for q in G]
    gram = [_dot(ar[q], bk[q].astype(BF16), NT) for q in G]
    a_ab = [jnp.where(strict, gram[q][:R, :R], 0.0) for q in G]
    a_kr = [jnp.concatenate([jnp.where(strict, gram[q][:R, R:], 0.0),
                             jnp.where(incl, gram[q][R:, R:], 0.0)], axis=0).astype(BF16) for q in G]
    a_rb = [jnp.where(incl, gram[q][R:, :R], 0.0).astype(BF16) for q in G]

    eye = jnp.where(ii == jj, 1.0, 0.0)
    t = [eye + jnp.where((ii >> 1) == (jj >> 1), a_ab[q], 0.0) for q in G]
    size = 2
    while size < C:
        sh = size.bit_length() - 1
        sel = ((ii >> (sh + 1)) == (jj >> (sh + 1))) & ((ii >> sh) != (jj >> sh))
        tb = [t[q].astype(BF16) for q in G]
        mid = [_dot(jnp.where(sel, a_ab[q], 0.0).astype(BF16), tb[q]) for q in G]
        t = [t[q] + _dot(tb[q], mid[q].astype(BF16)) for q in G]
        size *= 2

    s = [s_scr[q] for q in G]
    v_st = [_stack_heads(v[q], lane_head) for q in G]
    from_state = [_dot(ar[q], s[q].astype(BF16), NT) for q in G]
    from_v = [_dot(a_kr[q], v_st[q].astype(BF16)) for q in G]
    u = [_dot1(t[q], from_state[q][:R] + from_v[q][:R]) for q in G]
    y_st = [from_state[q][R:] + from_v[q][R:] + _dot(a_rb[q], u[q].astype(BF16)) for q in G]
    for q in G:
        uv = jnp.concatenate([u[q], v_st[q]], axis=0)
        s_scr[q] = s[q] * pend[q] + _dot1(uv.T, bk[q] * pend[q])

    for q in G:
        y = y_st[q][0:C]
        for h in range(1, HPG):
            y = y + y_st[q][h * C:(h + 1) * C]
        mean = _dot_exact_rhs(y, ones) * (1.0 / RW_HEAD)
        yc = y - mean
        var = _dot_exact_rhs(yc * yc, ones) * (1.0 / RW_HEAD)
        sl = sls[q]
        yn = yc * lax.rsqrt(var + GN_EPS) * lnw_ref[:, sl] + lnb_ref[:, sl]
        bonus = _dot_exact_rhs(r[q] * k[q] * rk_ref[:, sl], ones) * v[q]
        y_ref[:, sl] = ((yn + bonus) * g_ref[0, q]).astype(y_ref.dtype)

    @pl.when(j == pl.num_programs(1) - 1)
    def _():
        st_ref[0] = s_scr[...]


def _rwkv_scan(arrs, s0, ln_w, ln_b, r_k, C):
    B, _, T, _ = arrs[0].shape
    nt = T // C
    in_spec = pl.BlockSpec((1, NG, C, GW), lambda b, j: (b, 0, j, 0))
    st_spec = pl.BlockSpec((1, NG, GW, GW), lambda b, j: (b, 0, 0, 0))
    row = pl.BlockSpec((1, RW_WIDTH), lambda b, j: (0, 0))
    return pl.pallas_call(
        functools.partial(_rwkv_scan_kernel, C=C),
        name="rwkv_scan",
        out_shape=[jax.ShapeDtypeStruct((B * T, RW_WIDTH), BF16),
                   jax.ShapeDtypeStruct((B, NG, GW, GW), F32)],
        grid=(B, nt),
        in_specs=[in_spec] * 7 + [st_spec, row, row, row],
        out_specs=[pl.BlockSpec((C, RW_WIDTH), lambda b, j: (b * nt + j, 0)), st_spec],
        scratch_shapes=[pltpu.VMEM((NG, GW, GW), F32)],
        compiler_params=_params(("parallel", "arbitrary"), V7X_VMEM_LIMIT),
    )(*arrs, s0, ln_w.reshape(1, -1), ln_b.reshape(1, -1), r_k.reshape(1, -1))


def _state_to_blockdiag(s):
    B = s.shape[0]
    s = s.reshape(B, NG, HPG, RW_HEAD, RW_HEAD)
    eye = jnp.eye(HPG, dtype=s.dtype)
    bd = s[:, :, :, :, None, :] * eye[None, None, :, None, :, None]
    return bd.reshape(B, NG, GW, GW)


def _blockdiag_to_state(bd):
    B = bd.shape[0]
    x = bd.reshape(B, NG, HPG, RW_HEAD, HPG, RW_HEAD)
    return jnp.stack([x[:, :, h, :, h, :] for h in range(HPG)], axis=2).reshape(B, RW_HEADS, RW_HEAD, RW_HEAD)


def _gelu(x):
    return 0.5 * x * (1.0 + lax.erf(x * (2.0 ** -0.5)))


def _gmlp_kernel(u_ref, v_ref, lng_ref, lnb_ref, ws_ref, bs_ref, y_ref, vn_ref, *, tm):
    u = _gelu(u_ref[...])
    vf = _gelu(v_ref[...])
    mean = jnp.mean(vf, axis=-1, keepdims=True)
    vc = vf - mean
    var = jnp.mean(vc * vc, axis=-1, keepdims=True)
    vn = vc * lax.rsqrt(var + LN_EPS) * lng_ref[...] + lnb_ref[...]
    vn_ref[...] = vn
    ti = lax.broadcasted_iota(jnp.int32, (GM_CHUNK, GM_CHUNK), 0)
    tj = lax.broadcasted_iota(jnp.int32, (GM_CHUNK, GM_CHUNK), 1)
    causal = ti >= tj
    rows = min(tm, GM_CHUNK)
    for g in range(GM_GROUPS):
        wm = jnp.where(causal, ws_ref[g], 0.0).astype(BF16)
        sl = slice(g * GM_GROUP_DIM, (g + 1) * GM_GROUP_DIM)
        for c in range(max(tm // GM_CHUNK, 1)):
            rs = slice(c * GM_CHUNK, c * GM_CHUNK + rows)
            vg = vn[rs, sl].astype(BF16)
            if rows < GM_CHUNK:
                vg = jnp.concatenate([vg, jnp.zeros((GM_CHUNK - rows, GM_GROUP_DIM), BF16)], axis=0)
            mixed = (_dot(wm, vg) + bs_ref[g])[:rows]
            y_ref[rs, sl] = (u[rs, sl] * mixed).astype(y_ref.dtype)


def _gmlp(p_gm, row_off, rows, tm, ln_g, ln_b, ws, layer, bs_b, y_dtype):
    off = row_off // tm
    assert row_off % tm == 0 and rows % tm == 0
    row = pl.BlockSpec((1, GM_WIDTH), lambda i: (0, 0))
    full3 = pl.BlockSpec((GM_GROUPS, GM_CHUNK, GM_CHUNK), lambda i: (0, 0, 0))
    ws_spec = pl.BlockSpec((None, GM_GROUPS, GM_CHUNK, GM_CHUNK), lambda i: (layer, 0, 0, 0))
    return pl.pallas_call(
        functools.partial(_gmlp_kernel, tm=tm),
        name="gmlp",
        out_shape=[jax.ShapeDtypeStruct((rows, GM_WIDTH), y_dtype), jax.ShapeDtypeStruct((rows, GM_WIDTH), F32)],
        grid=(rows // tm,),
        in_specs=[pl.BlockSpec((tm, GM_WIDTH), lambda i: (off + i, 0)),
                  pl.BlockSpec((tm, GM_WIDTH), lambda i: (off + i, 1)),
                  row, row, ws_spec, full3],
        out_specs=[pl.BlockSpec((tm, GM_WIDTH), lambda i: (i, 0))] * 2,
        compiler_params=_params(("parallel",), V7X_VMEM_LIMIT),
    )(p_gm, p_gm, ln_g.reshape(1, -1), ln_b.reshape(1, -1), ws, bs_b)


def _attn_prep_kernel(q_ref, k_ref, cos_ref, sin_ref, qg_ref, kg_ref, qo_ref, ko_ref):
    cos = cos_ref[...]
    sin = sin_ref[...]

    def norm_rope(x, gain):
        y = x * lax.rsqrt(jnp.mean(x * x, axis=-1, keepdims=True) + NORM_EPS) * gain
        return y * cos + pltpu.roll(y, ATT_HEAD // 2, axis=1) * sin

    for h in range(N_DIL * ATT_GROUP_HEADS):
        sl = slice(h * ATT_HEAD, (h + 1) * ATT_HEAD)
        qo_ref[:, sl] = norm_rope(q_ref[:, sl], qg_ref[...]).astype(qo_ref.dtype)
        ko_ref[:, sl] = norm_rope(k_ref[:, sl], kg_ref[...]).astype(ko_ref.dtype)


def _attn_prep(p_at, cos, sin, q_gain, k_gain, tm):
    M = p_at.shape[0]
    blk = lambda c: pl.BlockSpec((tm, ATT_WIDTH), lambda i: (i, c))
    tab = pl.BlockSpec((tm, ATT_HEAD), lambda i: (i, 0))
    gain = pl.BlockSpec((1, ATT_HEAD), lambda i: (0, 0))
    return pl.pallas_call(
        _attn_prep_kernel,
        name="attn_prep",
        out_shape=[jax.ShapeDtypeStruct((M, ATT_WIDTH), F32), jax.ShapeDtypeStruct((M, ATT_WIDTH), F32)],
        grid=(M // tm,),
        in_specs=[blk(0), blk(1), tab, tab, gain, gain],
        out_specs=[pl.BlockSpec((tm, ATT_WIDTH), lambda i: (i, 0))] * 2,
        compiler_params=_params(("parallel",), V7X_VMEM_LIMIT),
    )(p_at, p_at, cos, sin, q_gain.reshape(1, -1), k_gain.reshape(1, -1))


def _attn_prompt_kernel(q0, q1, q2, k0, k1, k2, v0, v1, v2, y_ref, o_scr, lse_scr, *, T):
    scale = ATT_HEAD ** -0.5
    qi = lax.broadcasted_iota(jnp.int32, (ATT_BLK, ATT_BLK), 0)
    kj = lax.broadcasted_iota(jnp.int32, (ATT_BLK, ATT_BLK), 1)
    refs = ((q0, k0, v0), (q1, k1, v1), (q2, k2, v2))
    for g, (window, dil) in enumerate(DIL_PAIRS):
        q_ref, k_ref, v_ref = refs[g]
        nb = T // dil // ATT_BLK

        def rows(r, n, dil=dil):
            if dil == 1:
                return pl.ds(n * ATT_BLK, ATT_BLK)
            return pl.ds(r + n * ATT_BLK * dil, ATT_BLK, stride=dil)

        blocks = [(rows(r, n), rows(r, n - 1) if n > 0 else None) for r in range(dil) for n in range(nb)]
        for b0 in range(0, len(blocks), ATTN_BATCH):
            batch = blocks[b0:b0 + ATTN_BATCH]
            q = [q_ref[rs, :].astype(BF16) for rs, _ in batch]
            s_c = [jnp.where(kj <= qi, _dot(q[i], k_ref[rs, :].astype(BF16), NT) * scale, NEG_INF)
                   for i, (rs, _) in enumerate(batch)]
            s_p = [None if ps_ is None else
                   jnp.where(kj >= qi, _dot(q[i], k_ref[ps_, :].astype(BF16), NT) * scale, NEG_INF)
                   for i, (_, ps_) in enumerate(batch)]
            m = [jnp.max(s, axis=-1, keepdims=True) for s in s_c]
            m = [mc if sp is None else jnp.maximum(mc, jnp.max(sp, axis=-1, keepdims=True))
                 for mc, sp in zip(m, s_p)]
            e_c = [jnp.exp(s - mm) for s, mm in zip(s_c, m)]
            e_p = [None if sp is None else jnp.exp(sp - mm) for sp, mm in zip(s_p, m)]
            den = [jnp.sum(e, axis=-1, keepdims=True) for e in e_c]
            den = [d if e is None else d + jnp.sum(e, axis=-1, keepdims=True) for d, e in zip(den, e_p)]
            acc = [_dot(e.astype(BF16), v_ref[rs, :].astype(BF16)) for e, (rs, _) in zip(e_c, batch)]
            acc = [a if e is None else a + _dot(e.astype(BF16), v_ref[ps_, :].astype(BF16))
                   for a, e, (_, ps_) in zip(acc, e_p, batch)]
            for i, (rs, _) in enumerate(batch):
                o_scr[g, rs, :] = acc[i] / den[i]
                lse_scr[g, rs, :] = jnp.broadcast_to(m[i] + jnp.log(den[i]), (ATT_BLK, ATT_HEAD))
    a, b, c = lse_scr[0], lse_scr[1], lse_scr[2]
    m = jnp.maximum(jnp.maximum(a, b), c)
    wa, wb, wc = jnp.exp(a - m), jnp.exp(b - m), jnp.exp(c - m)
    y = (wa * o_scr[0] + wb * o_scr[1] + wc * o_scr[2]) / (wa + wb + wc)
    y_ref[...] = y.astype(y_ref.dtype)


def _attn_prompt(q_rot, k_rot, p_at, B, T):
    nh = ATT_GROUP_HEADS
    col = lambda g, base: pl.BlockSpec((T, ATT_HEAD), lambda b, h: (b, base + g * nh + h))
    vbase = 2 * ATT_WIDTH // ATT_HEAD
    return pl.pallas_call(
        functools.partial(_attn_prompt_kernel, T=T),
        name="attn_prompt",
        out_shape=jax.ShapeDtypeStruct((B * T, ATT_OUT), BF16),
        grid=(B, nh),
        in_specs=[col(g, 0) for g in range(N_DIL)] * 2 + [col(g, vbase) for g in range(N_DIL)],
        out_specs=pl.BlockSpec((T, ATT_HEAD), lambda b, h: (b, h)),
        scratch_shapes=[pltpu.VMEM((N_DIL, T, ATT_HEAD), F32)] * 2,
        compiler_params=_params(("parallel", "parallel"), V7X_VMEM_LIMIT),
    )(q_rot, q_rot, q_rot, k_rot, k_rot, k_rot, p_at, p_at, p_at)


def _attn_sample_kernel(q_ref, k_ref, v_ref, c0_ref, c1_ref, c2_ref, y_ref, *, T):
    scale = ATT_HEAD ** -0.5
    caches = (c0_ref, c1_ref, c2_ref)
    zpad = jnp.zeros((LANES - T, ATT_HEAD), F32)
    outs = [[None] * N_DIL for _ in range(ATT_GROUP_HEADS)]
    lses = [[None] * N_DIL for _ in range(ATT_GROUP_HEADS)]
    for g, (window, dil) in enumerate(DIL_PAIRS):
        cref = caches[g]
        wb = cref.shape[1]
        nkeys = wb + LANES
        t = lax.broadcasted_iota(jnp.int32, (T, nkeys), 0)
        jrow = lax.broadcasted_iota(jnp.int32, (T, nkeys), 1)
        dist = wb + t - jrow
        valid = (dist >= 0) & (dist <= window) & ((dist & (dil - 1)) == 0)
        for h in range(ATT_GROUP_HEADS):
            sl = slice((g * ATT_GROUP_HEADS + h) * ATT_HEAD, (g * ATT_GROUP_HEADS + h + 1) * ATT_HEAD)
            q = q_ref[:, sl].astype(BF16)
            kcat = jnp.concatenate([cref[0, :, h * ATT_HEAD:(h + 1) * ATT_HEAD],
                                    k_ref[:, sl], zpad], axis=0).astype(BF16)
            vcat = jnp.concatenate([cref[0, :, ATT_OUT + h * ATT_HEAD:ATT_OUT + (h + 1) * ATT_HEAD],
                                    v_ref[:, sl], zpad], axis=0).astype(BF16)
            s = jnp.where(valid, _dot(q, kcat, NT) * scale, NEG_INF)
            m = jnp.max(s, axis=-1, keepdims=True)
            lse = m + jnp.log(jnp.sum(jnp.exp(s - m), axis=-1, keepdims=True))
            outs[h][g] = _dot(jnp.exp(s - lse).astype(BF16), vcat)
            lses[h][g] = lse
    for h in range(ATT_GROUP_HEADS):
        m = jnp.maximum(jnp.maximum(lses[h][0], lses[h][1]), lses[h][2])
        w = [jnp.exp(l - m) for l in lses[h]]
        y = (w[0] * outs[h][0] + w[1] * outs[h][1] + w[2] * outs[h][2]) / (w[0] + w[1] + w[2])
        y_ref[:, h * ATT_HEAD:(h + 1) * ATT_HEAD] = y.astype(y_ref.dtype)


def _attn_sample(q_rot, k_rot, p_at, row_off, B, T, caches, layer):
    off = row_off // T
    blk = lambda c: pl.BlockSpec((T, ATT_WIDTH), lambda b: (off + b, c))
    cspec = lambda a: pl.BlockSpec((None, 1) + a.shape[2:], lambda b: (layer, b, 0, 0))
    return pl.pallas_call(
        functools.partial(_attn_sample_kernel, T=T),
        name="attn_sample",
        out_shape=jax.ShapeDtypeStruct((B * T, ATT_OUT), F32),
        grid=(B,),
        in_specs=[blk(0), blk(0), blk(2)] + [cspec(c) for c in caches],
        out_specs=pl.BlockSpec((T, ATT_OUT), lambda b: (b, 0)),
        compiler_params=_params(("parallel",), V7X_VMEM_LIMIT),
    )(q_rot, k_rot, p_at, *caches)


def _branch_kernel(yr_ref, yg_ref, ya_ref, wr_ref, wg_ref, wa_ref, g0_ref, g1_ref, g2_ref, o_ref):
    acc = _sigmoid(g0_ref[...]) * _dot(yr_ref[...], wr_ref[...].astype(BF16))
    acc = acc + _sigmoid(g1_ref[...]) * _dot(yg_ref[...], wg_ref[...].astype(BF16))
    acc = acc + _sigmoid(g2_ref[...]) * _dot(ya_ref[...], wa_ref[...].astype(BF16))
    o_ref[...] = acc.astype(o_ref.dtype)


def _branch(y_rw, y_gm, y_at, w_rw, w_gm, w_at, layer, p_gate, tm, tn):
    M = y_rw.shape[0]
    nb = D_MODEL // tn
    lhs = lambda kdim: pl.BlockSpec((tm, kdim), lambda i, j: (i, 0))
    rhs = lambda kdim: pl.BlockSpec((None, kdim, tn), lambda i, j: (layer, 0, j))
    gate = lambda br: pl.BlockSpec((tm, tn), lambda i, j: (i, br * nb + j))
    return pl.pallas_call(
        _branch_kernel,
        name="branch_merge",
        out_shape=jax.ShapeDtypeStruct((M, D_MODEL), BF16),
        grid=(M // tm, nb),
        in_specs=[lhs(RW_WIDTH), lhs(GM_WIDTH), lhs(ATT_OUT), rhs(RW_WIDTH), rhs(GM_WIDTH), rhs(ATT_OUT),
                  gate(0), gate(1), gate(2)],
        out_specs=pl.BlockSpec((tm, tn), lambda i, j: (i, j)),
        compiler_params=_params(("parallel", "parallel"), V7X_VMEM_LIMIT),
    )(y_rw, y_gm, y_at, w_rw, w_gm, w_at, p_gate, p_gate, p_gate)


def _rope_tables(pos):
    half = ATT_HEAD // 2
    inv = ROPE_THETA ** (-jnp.arange(half, dtype=F32) / half)
    ang = pos.astype(F32)[:, None] * inv[None, :]
    cos, sin = jnp.cos(ang), jnp.sin(ang)
    return jnp.concatenate([cos, cos], -1), jnp.concatenate([-sin, sin], -1)


def _kv_rows(k_rot, p_at, row0, B, T, keep, g):
    def last(x, c):
        x = lax.slice(x, (row0, c), (row0 + B * T, c + ATT_OUT)).reshape(B, T, ATT_OUT)[:, T - keep:]
        return x.reshape(B, keep, ATT_GROUP_HEADS, ATT_HEAD)
    return jnp.stack([last(k_rot, g * ATT_OUT), last(p_at, 2 * ATT_WIDTH + g * ATT_OUT)], axis=2)


def kernel(x_prompt, x_sample, cache_kv_w128, cache_kv_w512, cache_kv_w2048, state_rwkv, state_rwkv_shift, norm1, w_in, rw_mu, rw_w0, rw_w_up, rw_a0, rw_a_up, rw_g_up, rw_k_k, rw_k_a, rw_r_k, rw_ln_w, rw_ln_b, gm_ln_g, gm_ln_b, gm_ws, gm_bs, att_q_gain, att_k_gain, w_br_rwkv, w_br_gmlp, w_br_attn, w_out, norm2, w_ff1, w_ff2):
    BP, TP, _ = x_prompt.shape
    BS, TS, _ = x_sample.shape
    depth = w_in.shape[0]
    MP, MS = BP * TP, BS * TS
    M = MP + MS
    TM = ROW_TILE
    assert M % (2 * TM) == 0 and MP % 1024 == 0 and TP % SCAN_C == 0 and TS <= SCAN_C
    x = jnp.concatenate([x_prompt.reshape(MP, D_MODEL), x_sample.reshape(MS, D_MODEL)], axis=0)

    pos = jnp.concatenate([jnp.tile(jnp.arange(TP), BP), jnp.tile(PAST_LEN + jnp.arange(TS), BS)])
    cos_t, sin_t = _rope_tables(pos)
    caches_all = [c.reshape(depth, BS, c.shape[2], 2 * ATT_OUT) for c in (cache_kv_w128, cache_kv_w512, cache_kv_w2048)]
    zeros_state = jnp.zeros((BP, NG, GW, GW), F32)
    zeros_shift = jnp.zeros((BP, RW_COLS), F32)
    c_rw, c_gm, c_at = RW_COLS, RW_COLS + 2 * GM_WIDTH, RW_COLS + 2 * GM_WIDTH + 3 * ATT_WIDTH

    kvp = [[] for _ in range(N_DIL)]
    kvs = [[] for _ in range(N_DIL)]
    st_p, st_s, sh_p, sh_s, gmv_s = [], [], [], [], []
    for l in range(depth):
        if l == 0:
            xg, ssq = _norm_operand(x, norm1[0], TM)
        mm_in = functools.partial(_matmul, xg, w_in, l, tm=TM * 2, tn=256, tk=D_MODEL, row_ssq=ssq)
        p_rw = mm_in(col_off=0, n_cols=RW_COLS, name="proj_rwkv")
        p_gm = mm_in(col_off=c_rw, n_cols=2 * GM_WIDTH, name="proj_gmlp")
        p_at = mm_in(col_off=c_gm, n_cols=3 * ATT_WIDTH, name="proj_attn")
        p_gate = mm_in(col_off=c_at, n_cols=N_BRANCH * D_MODEL, name="proj_gates")

        zpad = jnp.zeros((DECAY_LORA, RW_WIDTH), F32)
        wup_pad = jnp.concatenate([rw_w_up[l], zpad], axis=0)
        aup_pad = jnp.concatenate([zpad, rw_a_up[l]], axis=0)
        prep = functools.partial(_rwkv_prep, p_rw, mu=rw_mu[l], w0=rw_w0[l], wup_pad=wup_pad, a0=rw_a0[l],
                                 aup_pad=aup_pad, gup=rw_g_up[l], k_k=rw_k_k[l].reshape(-1), k_a=rw_k_a[l].reshape(-1))
        scan = functools.partial(_rwkv_scan, ln_w=rw_ln_w[l].reshape(-1), ln_b=rw_ln_b[l].reshape(-1),
                                 r_k=rw_r_k[l].reshape(-1), C=SCAN_C)
        arrs_p = prep(row_off=0, B=BP, T=TP, tm=128, t_out=128, prev=zeros_shift)
        y_rw_p, sT_p = scan(arrs_p, zeros_state)
        arrs_s = prep(row_off=MP, B=BS, T=TS, tm=TS, t_out=SCAN_C, prev=state_rwkv_shift[l])
        y_rw_s, sT_s = scan(arrs_s, _state_to_blockdiag(state_rwkv[l]))
        y_rw_s = y_rw_s.reshape(BS, SCAN_C, RW_WIDTH)[:, :TS].reshape(MS, RW_WIDTH)
        y_rw = jnp.concatenate([y_rw_p, y_rw_s], axis=0)
        st_p.append(_blockdiag_to_state(sT_p))
        st_s.append(_blockdiag_to_state(sT_s))
        sh_p.append(p_rw[TP - 1:MP:TP])
        sh_s.append(p_rw[MP + TS - 1::TS])

        bs_b = jnp.broadcast_to(gm_bs[l][:, :, None], (GM_GROUPS, GM_CHUNK, GM_GROUP_DIM))
        y_gm_p, _ = _gmlp(p_gm, 0, MP, 256, gm_ln_g[l], gm_ln_b[l], gm_ws, l, bs_b, BF16)
        y_gm_s, vn_s = _gmlp(p_gm, MP, MS, TS, gm_ln_g[l], gm_ln_b[l], gm_ws, l, bs_b, F32)
        y_gm = jnp.concatenate([y_gm_p, y_gm_s.astype(BF16)], axis=0)
        gmv_s.append(vn_s.reshape(BS, TS, GM_WIDTH))

        q_rot, k_rot = _attn_prep(p_at, cos_t, sin_t, att_q_gain[l], att_k_gain[l], TM)
        for g, (window, dil) in enumerate(DIL_PAIRS):
            keep = min(window, TP)
            kvp[g].append(_kv_rows(k_rot, p_at, 0, BP, TP, keep, g))
            kvs[g].append(_kv_rows(k_rot, p_at, MP, BS, TS, TS, g))
        y_at_p = _attn_prompt(q_rot, k_rot, p_at, BP, TP)
        y_at_s = _attn_sample(q_rot, k_rot, p_at, MP, BS, TS, caches_all, l)
        y_at = jnp.concatenate([y_at_p, y_at_s.astype(BF16)], axis=0)

        merged = _branch(y_rw, y_gm, y_at, w_br_rwkv, w_br_gmlp, w_br_attn, l, p_gate, TM * 2, 256)
        x, xg2, ssq2 = _matmul(merged, w_out, l, tm=TM * 2, tn=256, tk=D_MODEL, epilogue="residual", res=x,
                               next_gain=norm2[l], name="out_proj")
        act = _matmul(xg2, w_ff1, l, tm=TM * 2, tn=512, tk=D_MODEL, epilogue="relu2", out_dtype=BF16,
                      row_ssq=ssq2, name="ffn_up")
        ffn_down = functools.partial(_matmul, act, w_ff2, l, tm=TM * 2, tn=1024, tk=1024, epilogue="residual",
                                     res=x, name="ffn_down")
        if l + 1 < depth:
            x, xg, ssq = ffn_down(next_gain=norm1[l + 1])
        else:
            x = ffn_down()

    return (x[:MP].reshape(BP, TP, D_MODEL), x[MP:].reshape(BS, TS, D_MODEL),
            jnp.stack(kvp[0], 0), jnp.stack(kvp[1], 0), jnp.stack(kvp[2], 0),
            jnp.stack(kvs[0], 0), jnp.stack(kvs[1], 0), jnp.stack(kvs[2], 0),
            jnp.stack(st_p, 0), jnp.stack(st_s, 0), jnp.stack(sh_p, 0), jnp.stack(sh_s, 0),
            jnp.stack(gmv_s, 0))
```

```python
import functools

import jax
import jax.numpy as jnp
from jax import lax
from jax.experimental import pallas as pl
from jax.experimental.pallas import tpu as pltpu

F32 = jnp.float32
BF16 = jnp.bfloat16

LANES = 128
V7X_VMEM_LIMIT = 56 * 1024 * 1024

D_MODEL = 4096
RW_HEADS = 24
RW_HEAD = 64
RW_WIDTH = RW_HEADS * RW_HEAD
DECAY_LORA = 64
AAA_LORA = 64
GATE_LORA = 128
RW_COLS = 3 * RW_WIDTH + DECAY_LORA + AAA_LORA + GATE_LORA
GN_EPS = 64e-5
GM_CHUNK = 128
GM_GROUPS = 12
GM_GROUP_DIM = 128
GM_WIDTH = GM_GROUPS * GM_GROUP_DIM
LN_EPS = 1e-5
DIL_PAIRS = ((128, 1), (512, 4), (2048, 16))
N_DIL = 3
ATT_GROUP_HEADS = 4
ATT_HEAD = 128
ATT_WIDTH = N_DIL * ATT_GROUP_HEADS * ATT_HEAD
ATT_OUT = ATT_GROUP_HEADS * ATT_HEAD
ATT_BLK = 128
ATTN_BATCH = 8
ROPE_THETA = 10000.0
N_BRANCH = 3
D_FF = 4 * D_MODEL
NORM_EPS = 1e-6
NEG_INF = -1e30
PAST_LEN = 8192

ROW_TILE = 688

HPG = 4
GW = HPG * RW_HEAD
NG = RW_HEADS // HPG
SCAN_C = 64

NN = (((1,), (0,)), ((), ()))
NT = (((1,), (1,)), ((), ()))


def _dot(a, b, dims=NN):
    return lax.dot_general(a, b, dims, preferred_element_type=F32)


def _dot1(a, b, dims=NN):
    return _dot(a.astype(BF16), b.astype(BF16), dims)


def _split2(a):
    hi = a.astype(BF16)
    return hi, (a - hi.astype(F32)).astype(BF16)


def _dot_exact_rhs(a, e):
    hi, lo = _split2(a)
    return _dot(hi, e) + _dot(lo, e)


def _dot_exact_lhs(e, a):
    hi, lo = _split2(a)
    return _dot(e, hi) + _dot(e, lo)


def _params(sem, vmem=None):
    return pltpu.CompilerParams(dimension_semantics=sem, vmem_limit_bytes=vmem)


def _sigmoid(x):
    return 1.0 / (1.0 + jnp.exp(-x))


def _mm_kernel(*refs, nk, epilogue, scaled, normed):
    refs = list(refs)
    a_ref, b_ref = refs.pop(0), refs.pop(0)
    ssq_ref = refs.pop(0) if scaled else None
    res_ref = refs.pop(0) if epilogue == "residual" else None
    gain_ref = refs.pop(0) if normed else None
    o_ref = refs.pop(0)
    dot = lambda: _dot(a_ref[...], b_ref[0].astype(BF16))
    j = pl.program_id(1)

    def emit_normed(x_new):
        xg_ref, ssq_out = refs
        xg_ref[...] = (x_new * gain_ref[...]).astype(xg_ref.dtype)
        row = jnp.broadcast_to(jnp.sum(x_new * x_new, axis=-1, keepdims=True), ssq_out.shape)

        @pl.when(j == 0)
        def _():
            ssq_out[...] = row

        @pl.when(j > 0)
        def _():
            ssq_out[...] += row

    if nk == 1:
        part = dot()
        if scaled:
            part = part * lax.rsqrt(ssq_ref[:, 0:1] * (1.0 / D_MODEL) + NORM_EPS)
        if epilogue == "relu2":
            part = jnp.square(jnp.maximum(part, 0.0))
        elif epilogue == "residual":
            part = part + res_ref[...]
        o_ref[...] = part.astype(o_ref.dtype)
        if normed:
            emit_normed(part)
    else:
        assert epilogue == "residual" and not scaled
        k = pl.program_id(2)

        @pl.when(k == 0)
        def _():
            o_ref[...] = res_ref[...] + dot()

        @pl.when(k > 0)
        def _():
            o_ref[...] = o_ref[...] + dot()

        if normed:
            @pl.when(k == nk - 1)
            def _():
                emit_normed(o_ref[...])


def _matmul(a, b, layer, *, col_off=0, n_cols=None, tm, tn, tk, epilogue="none", res=None, out_dtype=F32,
            row_ssq=None, next_gain=None, name="matmul"):
    M, K = a.shape
    n_cols = b.shape[2] if n_cols is None else n_cols
    assert M % tm == 0 and n_cols % tn == 0 and K % tk == 0 and col_off % LANES == 0
    nk = K // tk
    assert nk == 1 or out_dtype == F32
    scaled, normed = row_ssq is not None, next_gain is not None
    in_specs = [pl.BlockSpec((tm, tk), lambda i, j, k: (i, k)),
                pl.BlockSpec((pl.Element(1), pl.Element(tk), pl.Element(tn)),
                             lambda i, j, k: (layer, pl.multiple_of(k * tk, tk),
                                              pl.multiple_of(col_off + j * tn, LANES)))]
    args = [a, b]
    tile = pl.BlockSpec((tm, tn), lambda i, j, k: (i, j))
    stat = pl.BlockSpec((tm, LANES), lambda i, j, k: (i, 0))
    if scaled:
        in_specs.append(stat)
        args.append(row_ssq)
    if epilogue == "residual":
        in_specs.append(tile)
        args.append(res)
    out_shape = [jax.ShapeDtypeStruct((M, n_cols), out_dtype)]
    out_specs = [tile]
    if normed:
        in_specs.append(pl.BlockSpec((1, tn), lambda i, j, k: (0, j)))
        args.append(next_gain.reshape(1, n_cols))
        out_shape += [jax.ShapeDtypeStruct((M, n_cols), BF16), jax.ShapeDtypeStruct((M, LANES), F32)]
        out_specs += [tile, stat]
    outs = pl.pallas_call(
        functools.partial(_mm_kernel, nk=nk, epilogue=epilogue, scaled=scaled, normed=normed),
        out_shape=out_shape,
        grid=(M // tm, n_cols // tn, nk),
        in_specs=in_specs,
        out_specs=out_specs,
        compiler_params=_params(("parallel", "arbitrary" if normed else "parallel", "arbitrary"), V7X_VMEM_LIMIT),
        name=name,
    )(*args)
    return outs if normed else outs[0]


def _norm_operand_kernel(x_ref, g_ref, xg_ref, ssq_ref):
    x = x_ref[...]
    xg_ref[...] = (x * g_ref[...]).astype(xg_ref.dtype)
    ssq_ref[...] = jnp.broadcast_to(jnp.sum(x * x, axis=-1, keepdims=True), ssq_ref.shape)


def _norm_operand(x, g, tm):
    M, D = x.shape
    return pl.pallas_call(
        _norm_operand_kernel,
        name="norm_operand",
        out_shape=[jax.ShapeDtypeStruct((M, D), BF16), jax.ShapeDtypeStruct((M, LANES), F32)],
        grid=(M // tm,),
        in_specs=[pl.BlockSpec((tm, D), lambda i: (i, 0)), pl.BlockSpec((1, D), lambda i: (0, 0))],
        out_specs=[pl.BlockSpec((tm, D), lambda i: (i, 0)), pl.BlockSpec((tm, LANES), lambda i: (i, 0))],
        compiler_params=_params(("parallel",), V7X_VMEM_LIMIT),
    )(x, g.reshape(1, D))


def _head_ones(width, head):
    r = lax.broadcasted_iota(jnp.int32, (width, width), 0) // head
    c = lax.broadcasted_iota(jnp.int32, (width, width), 1) // head
    return jnp.where(r == c, 1.0, 0.0).astype(BF16)


def _rwkv_prep_kernel(p_ref, prev_ref, mu_ref, w0_ref, wup_ref, a0_ref, aup_ref, gup_ref, kk_ref, ka_ref,
                      r_o, lw_o, k_o, v_o, kkn_o, b_o, g_o, carry_ref, *, tm, t_out):
    j = pl.program_id(1)

    @pl.when(j == 0)
    def _():
        carry_ref[...] = prev_ref[0]

    p = p_ref[...]
    row = lax.broadcasted_iota(jnp.int32, p.shape, 0)
    p_prev = jnp.where(row == 0, carry_ref[...], pltpu.roll(p, 1, axis=0))
    carry_ref[...] = p[tm - 1:tm, :]
    ps = p + mu_ref[...] * (p_prev - p)

    W = RW_WIDTH
    r = ps[:, 0:W]
    k = ps[:, W:2 * W]
    v = ps[:, 2 * W:3 * W]
    wa = ps[:, 3 * W:3 * W + LANES]
    g_in = ps[:, 3 * W + LANES:3 * W + 2 * LANES]

    z = -(w0_ref[...] + _dot(jnp.tanh(wa).astype(BF16), wup_ref[...].astype(BF16)))
    softplus = jnp.maximum(z, 0.0) + jnp.log1p(jnp.exp(-jnp.abs(z)))
    lw = -jnp.exp(-softplus - 0.5)
    a = _sigmoid(a0_ref[...] + _dot(wa.astype(BF16), aup_ref[...].astype(BF16)))
    g = _dot(_sigmoid(g_in).astype(BF16), gup_ref[...].astype(BF16))

    kk = k * kk_ref[...]
    ones = _head_ones(GW, RW_HEAD)
    kmod = k * (1.0 + (a - 1.0) * ka_ref[...])
    outs = (r_o, lw_o, k_o, v_o, kkn_o, b_o, g_o)
    if t_out > tm:
        for o in outs:
            o[...] = jnp.zeros(o.shape, o.dtype)
    for q in range(NG):
        sl = slice(q * GW, (q + 1) * GW)
        kq = kk[:, sl]
        ssq = _dot_exact_rhs(kq * kq, ones)
        kn = kq * lax.rsqrt(jnp.maximum(ssq, 1e-24))
        vals = (r[:, sl], lw[:, sl], kmod[:, sl], v[:, sl], kn, kn * a[:, sl], g[:, sl])
        for o, val in zip(outs, vals):
            o[0, q, 0:tm, :] = val


def _rwkv_prep(p_rw, row_off, B, T, tm, t_out, prev, mu, w0, wup_pad, a0, aup_pad, gup, k_k, k_a):
    nt = T // tm
    off = row_off // tm
    assert row_off % tm == 0 and T % tm == 0 and t_out >= tm
    row = lambda n: pl.BlockSpec((1, n), lambda b, j: (0, 0))
    out_spec = pl.BlockSpec((1, NG, t_out, GW), lambda b, j: (b, 0, j, 0))
    out = jax.ShapeDtypeStruct((B, NG, nt * t_out, GW), F32)
    return pl.pallas_call(
        functools.partial(_rwkv_prep_kernel, tm=tm, t_out=t_out),
        name="rwkv_prep",
        out_shape=[out] * 7,
        grid=(B, nt),
        in_specs=[pl.BlockSpec((tm, RW_COLS), lambda b, j: (off + b * nt + j, 0)),
                  pl.BlockSpec((1, 1, RW_COLS), lambda b, j: (b, 0, 0)),
                  row(RW_COLS), row(RW_WIDTH),
                  pl.BlockSpec((LANES, RW_WIDTH), lambda b, j: (0, 0)),
                  row(RW_WIDTH),
                  pl.BlockSpec((LANES, RW_WIDTH), lambda b, j: (0, 0)),
                  pl.BlockSpec((GATE_LORA, RW_WIDTH), lambda b, j: (0, 0)),
                  row(RW_WIDTH), row(RW_WIDTH)],
        out_specs=[out_spec] * 7,
        scratch_shapes=[pltpu.VMEM((1, RW_COLS), F32)],
        compiler_params=_params(("parallel", "arbitrary"), V7X_VMEM_LIMIT),
    )(p_rw, prev.reshape(B, 1, RW_COLS), mu.reshape(1, -1), w0.reshape(1, -1), wup_pad, a0.reshape(1, -1),
      aup_pad, gup, k_k.reshape(1, -1), k_a.reshape(1, -1))


def _stack_heads(x, lane_head):
    return jnp.concatenate([jnp.where(lane_head == h, x, 0.0) for h in range(HPG)], axis=0)


def _rwkv_scan_kernel(r_ref, lw_ref, k_ref, v_ref, kk_ref, b_ref, g_ref, s0_ref, lnw_ref, lnb_ref, rk_ref,
                      y_ref, st_ref, s_scr, *, C):
    j = pl.program_id(1)

    @pl.when(j == 0)
    def _():
        s_scr[...] = s0_ref[0]

    R = HPG * C
    G = range(NG)
    lane_head = lax.broadcasted_iota(jnp.int32, (1, GW), 1) // RW_HEAD
    ii = lax.broadcasted_iota(jnp.int32, (R, R), 0)
    jj = lax.broadcasted_iota(jnp.int32, (R, R), 1)
    ti = lax.broadcasted_iota(jnp.int32, (C, C), 0)
    tj = lax.broadcasted_iota(jnp.int32, (C, C), 1)
    tril = jnp.where(ti >= tj, 1.0, 0.0).astype(BF16)
    ones = _head_ones(GW, RW_HEAD)
    strict = ii > jj
    incl = ii >= jj
    sls = [slice(q * GW, (q + 1) * GW) for q in G]

    r = [r_ref[0, q] for q in G]
    k = [k_ref[0, q] for q in G]
    v = [v_ref[0, q] for q in G]
    lw_all = jnp.concatenate([lw_ref[0, q] for q in G], axis=1)
    cum_all = _dot_exact_lhs(tril, lw_all)
    cum = [cum_all[:, sl] for sl in sls]
    pc = [jnp.exp(c) for c in cum]
    pinv = [jnp.exp(-c) for c in cum]
    pend = [p[C - 1:C, :] for p in pc]
    ar = [jnp.concatenate([_stack_heads(-kk_ref[0, q] * jnp.exp(cum[q] - lw_ref[0, q]), lane_head),
                           _stack_heads(r[q] * pc[q], lane_head)], axis=0).astype(BF16) for q in G]
    bk = [jnp.concatenate([_stack_heads(b_ref[0, q] * pinv[q], lane_head),
                           _stack_heads(k[q] * pinv[q], lane_head)], axis=0) for q in G]
    gram = [_dot(ar[q], bk[q].astype(BF16), NT) for q in G]
    a_ab = [jnp.where(strict, gram[q][:R, :R], 0.0) for q in G]
    a_kr = [jnp.concatenate([jnp.where(strict, gram[q][:R, R:], 0.0),
                             jnp.where(incl, gram[q][R:, R:], 0.0)], axis=0).astype(BF16) for q in G]
    a_rb = [jnp.where(incl, gram[q][R:, :R], 0.0).astype(BF16) for q in G]

    eye = jnp.where(ii == jj, 1.0, 0.0)
    t = [eye + jnp.where((ii >> 1) == (jj >> 1), a_ab[q], 0.0) for q in G]
    size = 2
    while size < C:
        sh = size.bit_length() - 1
        sel = ((ii >> (sh + 1)) == (jj >> (sh + 1))) & ((ii >> sh) != (jj >> sh))
        tb = [t[q].astype(BF16) for q in G]
        mid = [_dot(jnp.where(sel, a_ab[q], 0.0).astype(BF16), tb[q]) for q in G]
        t = [t[q] + _dot(tb[q], mid[q].astype(BF16)) for q in G]
        size *= 2

    s = [s_scr[q] for q in G]
    v_st = [_stack_heads(v[q], lane_head) for q in G]
    from_state = [_dot(ar[q], s[q].astype(BF16), NT) for q in G]
    from_v = [_dot(a_kr[q], v_st[q].astype(BF16)) for q in G]
    u = [_dot1(t[q], from_state[q][:R] + from_v[q][:R]) for q in G]
    y_st = [from_state[q][R:] + from_v[q][R:] + _dot(a_rb[q], u[q].astype(BF16)) for q in G]
    for q in G:
        uv = jnp.concatenate([u[q], v_st[q]], axis=0)
        s_scr[q] = s[q] * pend[q] + _dot1(uv.T, bk[q] * pend[q])

    for q in G:
        y = y_st[q][0:C]
        for h in range(1, HPG):
            y = y + y_st[q][h * C:(h + 1) * C]
        mean = _dot_exact_rhs(y, ones) * (1.0 / RW_HEAD)
        yc = y - mean
        var = _dot_exact_rhs(yc * yc, ones) * (1.0 / RW_HEAD)
        sl = sls[q]
        yn = yc * lax.rsqrt(var + GN_EPS) * lnw_ref[:, sl] + lnb_ref[:, sl]
        bonus = _dot_exact_rhs(r[q] * k[q] * rk_ref[:, sl], ones) * v[q]
        y_ref[:, sl] = ((yn + bonus) * g_ref[0, q]).astype(y_ref.dtype)

    @pl.when(j == pl.num_programs(1) - 1)
    def _():
        st_ref[0] = s_scr[...]


def _rwkv_scan(arrs, s0, ln_w, ln_b, r_k, C):
    B, _, T, _ = arrs[0].shape
    nt = T // C
    in_spec = pl.BlockSpec((1, NG, C, GW), lambda b, j: (b, 0, j, 0))
    st_spec = pl.BlockSpec((1, NG, GW, GW), lambda b, j: (b, 0, 0, 0))
    row = pl.BlockSpec((1, RW_WIDTH), lambda b, j: (0, 0))
    return pl.pallas_call(
        functools.partial(_rwkv_scan_kernel, C=C),
        name="rwkv_scan",
        out_shape=[jax.ShapeDtypeStruct((B * T, RW_WIDTH), BF16),
                   jax.ShapeDtypeStruct((B, NG, GW, GW), F32)],
        grid=(B, nt),
        in_specs=[in_spec] * 7 + [st_spec, row, row, row],
        out_specs=[pl.BlockSpec((C, RW_WIDTH), lambda b, j: (b * nt + j, 0)), st_spec],
        scratch_shapes=[pltpu.VMEM((NG, GW, GW), F32)],
        compiler_params=_params(("parallel", "arbitrary"), V7X_VMEM_LIMIT),
    )(*arrs, s0, ln_w.reshape(1, -1), ln_b.reshape(1, -1), r_k.reshape(1, -1))


def _state_to_blockdiag(s):
    B = s.shape[0]
    s = s.reshape(B, NG, HPG, RW_HEAD, RW_HEAD)
    eye = jnp.eye(HPG, dtype=s.dtype)
    bd = s[:, :, :, :, None, :] * eye[None, None, :, None, :, None]
    return bd.reshape(B, NG, GW, GW)


def _blockdiag_to_state(bd):
    B = bd.shape[0]
    x = bd.reshape(B, NG, HPG, RW_HEAD, HPG, RW_HEAD)
    return jnp.stack([x[:, :, h, :, h, :] for h in range(HPG)], axis=2).reshape(B, RW_HEADS, RW_HEAD, RW_HEAD)


def _gelu(x):
    return 0.5 * x * (1.0 + lax.erf(x * (2.0 ** -0.5)))


def _gmlp_kernel(u_ref, v_ref, lng_ref, lnb_ref, ws_ref, bs_ref, y_ref, vn_ref, *, tm):
    u = _gelu(u_ref[...])
    vf = _gelu(v_ref[...])
    mean = jnp.mean(vf, axis=-1, keepdims=True)
    vc = vf - mean
    var = jnp.mean(vc * vc, axis=-1, keepdims=True)
    vn = vc * lax.rsqrt(var + LN_EPS) * lng_ref[...] + lnb_ref[...]
    vn_ref[...] = vn
    ti = lax.broadcasted_iota(jnp.int32, (GM_CHUNK, GM_CHUNK), 0)
    tj = lax.broadcasted_iota(jnp.int32, (GM_CHUNK, GM_CHUNK), 1)
    causal = ti >= tj
    rows = min(tm, GM_CHUNK)
    for g in range(GM_GROUPS):
        wm = jnp.where(causal, ws_ref[g], 0.0).astype(BF16)
        sl = slice(g * GM_GROUP_DIM, (g + 1) * GM_GROUP_DIM)
        for c in range(max(tm // GM_CHUNK, 1)):
            rs = slice(c * GM_CHUNK, c * GM_CHUNK + rows)
            vg = vn[rs, sl].astype(BF16)
            if rows < GM_CHUNK:
                vg = jnp.concatenate([vg, jnp.zeros((GM_CHUNK - rows, GM_GROUP_DIM), BF16)], axis=0)
            mixed = (_dot(wm, vg) + bs_ref[g])[:rows]
            y_ref[rs, sl] = (u[rs, sl] * mixed).astype(y_ref.dtype)


def _gmlp(p_gm, row_off, rows, tm, ln_g, ln_b, ws, layer, bs_b, y_dtype):
    off = row_off // tm
    assert row_off % tm == 0 and rows % tm == 0
    row = pl.BlockSpec((1, GM_WIDTH), lambda i: (0, 0))
    ws_spec = pl.BlockSpec((None, GM_GROUPS, GM_CHUNK, GM_CHUNK), lambda i: (layer, 0, 0, 0))
    return pl.pallas_call(
        functools.partial(_gmlp_kernel, tm=tm),
        name="gmlp",
        out_shape=[jax.ShapeDtypeStruct((rows, GM_WIDTH), y_dtype), jax.ShapeDtypeStruct((rows, GM_WIDTH), F32)],
        grid=(rows // tm,),
        in_specs=[pl.BlockSpec((tm, GM_WIDTH), lambda i: (off + i, 0)),
                  pl.BlockSpec((tm, GM_WIDTH), lambda i: (off + i, 1)),
                  row, row, ws_spec, ws_spec],
        out_specs=[pl.BlockSpec((tm, GM_WIDTH), lambda i: (i, 0))] * 2,
        compiler_params=_params(("parallel",), V7X_VMEM_LIMIT),
    )(p_gm, p_gm, ln_g.reshape(1, -1), ln_b.reshape(1, -1), ws, bs_b)


def _attn_prep_kernel(q_ref, k_ref, cos_ref, sin_ref, qg_ref, kg_ref, qo_ref, ko_ref):
    cos = cos_ref[...]
    sin = sin_ref[...]

    def norm_rope(x, gain):
        y = x * lax.rsqrt(jnp.mean(x * x, axis=-1, keepdims=True) + NORM_EPS) * gain
        return y * cos + pltpu.roll(y, ATT_HEAD // 2, axis=1) * sin

    for h in range(N_DIL * ATT_GROUP_HEADS):
        sl = slice(h * ATT_HEAD, (h + 1) * ATT_HEAD)
        qo_ref[:, sl] = norm_rope(q_ref[:, sl], qg_ref[...]).astype(qo_ref.dtype)
        ko_ref[:, sl] = norm_rope(k_ref[:, sl], kg_ref[...]).astype(ko_ref.dtype)


def _attn_prep(p_at, cos, sin, q_gain, k_gain, tm):
    M = p_at.shape[0]
    blk = lambda c: pl.BlockSpec((tm, ATT_WIDTH), lambda i: (i, c))
    tab = pl.BlockSpec((tm, ATT_HEAD), lambda i: (i, 0))
    gain = pl.BlockSpec((1, ATT_HEAD), lambda i: (0, 0))
    return pl.pallas_call(
        _attn_prep_kernel,
        name="attn_prep",
        out_shape=[jax.ShapeDtypeStruct((M, ATT_WIDTH), F32), jax.ShapeDtypeStruct((M, ATT_WIDTH), F32)],
        grid=(M // tm,),
        in_specs=[blk(0), blk(1), tab, tab, gain, gain],
        out_specs=[pl.BlockSpec((tm, ATT_WIDTH), lambda i: (i, 0))] * 2,
        compiler_params=_params(("parallel",), V7X_VMEM_LIMIT),
    )(p_at, p_at, cos, sin, q_gain.reshape(1, -1), k_gain.reshape(1, -1))


def _attn_prompt_kernel(q0, q1, q2, k0, k1, k2, v0, v1, v2, y_ref, o_scr, lse_scr, *, T):
    scale = ATT_HEAD ** -0.5
    qi = lax.broadcasted_iota(jnp.int32, (ATT_BLK, ATT_BLK), 0)
    kj = lax.broadcasted_iota(jnp.int32, (ATT_BLK, ATT_BLK), 1)
    refs = ((q0, k0, v0), (q1, k1, v1), (q2, k2, v2))
    for g, (window, dil) in enumerate(DIL_PAIRS):
        q_ref, k_ref, v_ref = refs[g]
        nb = T // dil // ATT_BLK

        def rows(r, n, dil=dil):
            if dil == 1:
                return pl.ds(n * ATT_BLK, ATT_BLK)
            return pl.ds(r + n * ATT_BLK * dil, ATT_BLK, stride=dil)

        blocks = [(rows(r, n), rows(r, n - 1) if n > 0 else None) for r in range(dil) for n in range(nb)]
        for b0 in range(0, len(blocks), ATTN_BATCH):
            batch = blocks[b0:b0 + ATTN_BATCH]
            q = [q_ref[rs, :].astype(BF16) for rs, _ in batch]
            s_c = [jnp.where(kj <= qi, _dot(q[i], k_ref[rs, :].astype(BF16), NT) * scale, NEG_INF)
                   for i, (rs, _) in enumerate(batch)]
            s_p = [None if ps_ is None else
                   jnp.where(kj >= qi, _dot(q[i], k_ref[ps_, :].astype(BF16), NT) * scale, NEG_INF)
                   for i, (_, ps_) in enumerate(batch)]
            m = [jnp.max(s, axis=-1, keepdims=True) for s in s_c]
            m = [mc if sp is None else jnp.maximum(mc, jnp.max(sp, axis=-1, keepdims=True))
                 for mc, sp in zip(m, s_p)]
            e_c = [jnp.exp(s - mm) for s, mm in zip(s_c, m)]
            e_p = [None if sp is None else jnp.exp(sp - mm) for sp, mm in zip(s_p, m)]
            den = [jnp.sum(e, axis=-1, keepdims=True) for e in e_c]
            den = [d if e is None else d + jnp.sum(e, axis=-1, keepdims=True) for d, e in zip(den, e_p)]
            acc = [_dot(e.astype(BF16), v_ref[rs, :].astype(BF16)) for e, (rs, _) in zip(e_c, batch)]
            acc = [a if e is None else a + _dot(e.astype(BF16), v_ref[ps_, :].astype(BF16))
                   for a, e, (_, ps_) in zip(acc, e_p, batch)]
            for i, (rs, _) in enumerate(batch):
                o_scr[g, rs, :] = acc[i] / den[i]
                lse_scr[g, rs, :] = jnp.broadcast_to(m[i] + jnp.log(den[i]), (ATT_BLK, ATT_HEAD))
    a, b, c = lse_scr[0], lse_scr[1], lse_scr[2]
    m = jnp.maximum(jnp.maximum(a, b), c)
    wa, wb, wc = jnp.exp(a - m), jnp.exp(b - m), jnp.exp(c - m)
    y = (wa * o_scr[0] + wb * o_scr[1] + wc * o_scr[2]) / (wa + wb + wc)
    y_ref[...] = y.astype(y_ref.dtype)


def _attn_prompt(q_rot, k_rot, p_at, B, T):
    nh = ATT_GROUP_HEADS
    col = lambda g, base: pl.BlockSpec((T, ATT_HEAD), lambda b, h: (b, base + g * nh + h))
    vbase = 2 * ATT_WIDTH // ATT_HEAD
    return pl.pallas_call(
        functools.partial(_attn_prompt_kernel, T=T),
        name="attn_prompt",
        out_shape=jax.ShapeDtypeStruct((B * T, ATT_OUT), BF16),
        grid=(B, nh),
        in_specs=[col(g, 0) for g in range(N_DIL)] * 2 + [col(g, vbase) for g in range(N_DIL)],
        out_specs=pl.BlockSpec((T, ATT_HEAD), lambda b, h: (b, h)),
        scratch_shapes=[pltpu.VMEM((N_DIL, T, ATT_HEAD), F32)] * 2,
        compiler_params=_params(("parallel", "parallel"), V7X_VMEM_LIMIT),
    )(q_rot, q_rot, q_rot, k_rot, k_rot, k_rot, p_at, p_at, p_at)


def _attn_sample_kernel(q_ref, k_ref, v_ref, c0_ref, c1_ref, c2_ref, y_ref, *, T):
    scale = ATT_HEAD ** -0.5
    caches = (c0_ref, c1_ref, c2_ref)
    zpad = jnp.zeros((LANES - T, ATT_HEAD), F32)
    outs = [[None] * N_DIL for _ in range(ATT_GROUP_HEADS)]
    lses = [[None] * N_DIL for _ in range(ATT_GROUP_HEADS)]
    for g, (window, dil) in enumerate(DIL_PAIRS):
        cref = caches[g]
        wb = cref.shape[1]
        nkeys = wb + LANES
        t = lax.broadcasted_iota(jnp.int32, (T, nkeys), 0)
        jrow = lax.broadcasted_iota(jnp.int32, (T, nkeys), 1)
        dist = wb + t - jrow
        valid = (dist >= 0) & (dist <= window) & ((dist & (dil - 1)) == 0)
        for h in range(ATT_GROUP_HEADS):
            sl = slice((g * ATT_GROUP_HEADS + h) * ATT_HEAD, (g * ATT_GROUP_HEADS + h + 1) * ATT_HEAD)
            q = q_ref[:, sl].astype(BF16)
            kcat = jnp.concatenate([cref[0, :, h * ATT_HEAD:(h + 1) * ATT_HEAD],
                                    k_ref[:, sl], zpad], axis=0).astype(BF16)
            vcat = jnp.concatenate([cref[0, :, ATT_OUT + h * ATT_HEAD:ATT_OUT + (h + 1) * ATT_HEAD],
                                    v_ref[:, sl], zpad], axis=0).astype(BF16)
            s = jnp.where(valid, _dot(q, kcat, NT) * scale, NEG_INF)
            m = jnp.max(s, axis=-1, keepdims=True)
            lse = m + jnp.log(jnp.sum(jnp.exp(s - m), axis=-1, keepdims=True))
            outs[h][g] = _dot(jnp.exp(s - lse).astype(BF16), vcat)
            lses[h][g] = lse
    for h in range(ATT_GROUP_HEADS):
        m = jnp.maximum(jnp.maximum(lses[h][0], lses[h][1]), lses[h][2])
        w = [jnp.exp(l - m) for l in lses[h]]
        y = (w[0] * outs[h][0] + w[1] * outs[h][1] + w[2] * outs[h][2]) / (w[0] + w[1] + w[2])
        y_ref[:, h * ATT_HEAD:(h + 1) * ATT_HEAD] = y.astype(y_ref.dtype)


def _attn_sample(q_rot, k_rot, p_at, row_off, B, T, caches, layer):
    off = row_off // T
    blk = lambda c: pl.BlockSpec((T, ATT_WIDTH), lambda b: (off + b, c))
    cspec = lambda a: pl.BlockSpec((None, 1) + a.shape[2:], lambda b: (layer, b, 0, 0))
    return pl.pallas_call(
        functools.partial(_attn_sample_kernel, T=T),
        name="attn_sample",
        out_shape=jax.ShapeDtypeStruct((B * T, ATT_OUT), F32),
        grid=(B,),
        in_specs=[blk(0), blk(0), blk(2)] + [cspec(c) for c in caches],
        out_specs=pl.BlockSpec((T, ATT_OUT), lambda b: (b, 0)),
        compiler_params=_params(("parallel",), V7X_VMEM_LIMIT),
    )(q_rot, k_rot, p_at, *caches)


def _branch_kernel(yr_ref, yg_ref, ya_ref, wr_ref, wg_ref, wa_ref, g0_ref, g1_ref, g2_ref, o_ref):
    acc = _sigmoid(g0_ref[...]) * _dot(yr_ref[...], wr_ref[...].astype(BF16))
    acc = acc + _sigmoid(g1_ref[...]) * _dot(yg_ref[...], wg_ref[...].astype(BF16))
    acc = acc + _sigmoid(g2_ref[...]) * _dot(ya_ref[...], wa_ref[...].astype(BF16))
    o_ref[...] = acc.astype(o_ref.dtype)


def _branch(y_rw, y_gm, y_at, w_rw, w_gm, w_at, layer, p_gate, tm, tn):
    M = y_rw.shape[0]
    nb = D_MODEL // tn
    lhs = lambda kdim: pl.BlockSpec((tm, kdim), lambda i, j: (i, 0))
    rhs = lambda kdim: pl.BlockSpec((None, kdim, tn), lambda i, j: (layer, 0, j))
    gate = lambda br: pl.BlockSpec((tm, tn), lambda i, j: (i, br * nb + j))
    return pl.pallas_call(
        _branch_kernel,
        name="branch_merge",
        out_shape=jax.ShapeDtypeStruct((M, D_MODEL), BF16),
        grid=(M // tm, nb),
        in_specs=[lhs(RW_WIDTH), lhs(GM_WIDTH), lhs(ATT_OUT), rhs(RW_WIDTH), rhs(GM_WIDTH), rhs(ATT_OUT),
                  gate(0), gate(1), gate(2)],
        out_specs=pl.BlockSpec((tm, tn), lambda i, j: (i, j)),
        compiler_params=_params(("parallel", "parallel"), V7X_VMEM_LIMIT),
    )(y_rw, y_gm, y_at, w_rw, w_gm, w_at, p_gate, p_gate, p_gate)


def _rope_tables(pos):
    half = ATT_HEAD // 2
    inv = ROPE_THETA ** (-jnp.arange(half, dtype=F32) / half)
    ang = pos.astype(F32)[:, None] * inv[None, :]
    cos, sin = jnp.cos(ang), jnp.sin(ang)
    return jnp.concatenate([cos, cos], -1), jnp.concatenate([-sin, sin], -1)


def _kv_rows(k_rot, p_at, row0, B, T, keep, g):
    def last(x, c):
        x = x[row0:row0 + B * T].reshape(B, T, x.shape[1])[:, T - keep:]
        return jnp.stack([x[:, :, c + h * ATT_HEAD:c + (h + 1) * ATT_HEAD] for h in range(ATT_GROUP_HEADS)], axis=2)
    return jnp.stack([last(k_rot, g * ATT_OUT), last(p_at, 2 * ATT_WIDTH + g * ATT_OUT)], axis=2)


def kernel(x_prompt, x_sample, cache_kv_w128, cache_kv_w512, cache_kv_w2048, state_rwkv, state_rwkv_shift, norm1, w_in, rw_mu, rw_w0, rw_w_up, rw_a0, rw_a_up, rw_g_up, rw_k_k, rw_k_a, rw_r_k, rw_ln_w, rw_ln_b, gm_ln_g, gm_ln_b, gm_ws, gm_bs, att_q_gain, att_k_gain, w_br_rwkv, w_br_gmlp, w_br_attn, w_out, norm2, w_ff1, w_ff2):
    BP, TP, _ = x_prompt.shape
    BS, TS, _ = x_sample.shape
    depth = w_in.shape[0]
    MP, MS = BP * TP, BS * TS
    M = MP + MS
    TM = ROW_TILE
    assert M % (2 * TM) == 0 and MP % 1024 == 0 and TP % SCAN_C == 0 and TS <= SCAN_C
    x = jnp.concatenate([x_prompt.reshape(MP, D_MODEL), x_sample.reshape(MS, D_MODEL)], axis=0)

    pos = jnp.concatenate([jnp.tile(jnp.arange(TP), BP), jnp.tile(PAST_LEN + jnp.arange(TS), BS)])
    cos_t, sin_t = _rope_tables(pos)
    caches_all = [c.reshape(depth, BS, c.shape[2], 2 * ATT_OUT) for c in (cache_kv_w128, cache_kv_w512, cache_kv_w2048)]
    zeros_state = jnp.zeros((BP, NG, GW, GW), F32)
    zeros_shift = jnp.zeros((BP, RW_COLS), F32)
    bs_b = jnp.broadcast_to(gm_bs[:, :, :, None], (depth, GM_GROUPS, GM_CHUNK, GM_GROUP_DIM))
    c_rw, c_gm, c_at = RW_COLS, RW_COLS + 2 * GM_WIDTH, RW_COLS + 2 * GM_WIDTH + 3 * ATT_WIDTH

    kvp = [[] for _ in range(N_DIL)]
    kvs = [[] for _ in range(N_DIL)]
    st_p, st_s, sh_p, sh_s, gmv_s = [], [], [], [], []
    for l in range(depth):
        if l == 0:
            xg, ssq = _norm_operand(x, norm1[0], TM)
        mm_in = functools.partial(_matmul, xg, w_in, l, tm=TM * 2, tk=D_MODEL, row_ssq=ssq)
        p_rw = mm_in(col_off=0, n_cols=RW_COLS, tn=256, name="proj_rwkv")
        p_gm = mm_in(col_off=c_rw, n_cols=2 * GM_WIDTH, tn=512, name="proj_gmlp")
        p_at = mm_in(col_off=c_gm, n_cols=3 * ATT_WIDTH, tn=512, name="proj_attn")
        p_gate = mm_in(col_off=c_at, n_cols=N_BRANCH * D_MODEL, tn=512, name="proj_gates")

        zpad = jnp.zeros((DECAY_LORA, RW_WIDTH), F32)
        wup_pad = jnp.concatenate([rw_w_up[l], zpad], axis=0)
        aup_pad = jnp.concatenate([zpad, rw_a_up[l]], axis=0)
        prep = functools.partial(_rwkv_prep, p_rw, mu=rw_mu[l], w0=rw_w0[l], wup_pad=wup_pad, a0=rw_a0[l],
                                 aup_pad=aup_pad, gup=rw_g_up[l], k_k=rw_k_k[l].reshape(-1), k_a=rw_k_a[l].reshape(-1))
        scan = functools.partial(_rwkv_scan, ln_w=rw_ln_w[l].reshape(-1), ln_b=rw_ln_b[l].reshape(-1),
                                 r_k=rw_r_k[l].reshape(-1), C=SCAN_C)
        arrs_p = prep(row_off=0, B=BP, T=TP, tm=128, t_out=128, prev=zeros_shift)
        y_rw_p, sT_p = scan(arrs_p, zeros_state)
        arrs_s = prep(row_off=MP, B=BS, T=TS, tm=TS, t_out=SCAN_C, prev=state_rwkv_shift[l])
        y_rw_s, sT_s = scan(arrs_s, _state_to_blockdiag(state_rwkv[l]))
        y_rw_s = y_rw_s.reshape(BS, SCAN_C, RW_WIDTH)[:, :TS].reshape(MS, RW_WIDTH)
        y_rw = jnp.concatenate([y_rw_p, y_rw_s], axis=0)
        st_p.append(_blockdiag_to_state(sT_p))
        st_s.append(_blockdiag_to_state(sT_s))
        sh_p.append(p_rw[TP - 1:MP:TP])
        sh_s.append(p_rw[MP + TS - 1::TS])

        y_gm_p, _ = _gmlp(p_gm, 0, MP, 256, gm_ln_g[l], gm_ln_b[l], gm_ws, l, bs_b, BF16)
        y_gm_s, vn_s = _gmlp(p_gm, MP, MS, TS, gm_ln_g[l], gm_ln_b[l], gm_ws, l, bs_b, F32)
        y_gm = jnp.concatenate([y_gm_p, y_gm_s.astype(BF16)], axis=0)
        gmv_s.append(vn_s.reshape(BS, TS, GM_WIDTH))

        q_rot, k_rot = _attn_prep(p_at, cos_t, sin_t, att_q_gain[l], att_k_gain[l], TM)
        for g, (window, dil) in enumerate(DIL_PAIRS):
            keep = min(window, TP)
            kvp[g].append(_kv_rows(k_rot, p_at, 0, BP, TP, keep, g))
            kvs[g].append(_kv_rows(k_rot, p_at, MP, BS, TS, TS, g))
        y_at_p = _attn_prompt(q_rot, k_rot, p_at, BP, TP)
        y_at_s = _attn_sample(q_rot, k_rot, p_at, MP, BS, TS, caches_all, l)
        y_at = jnp.concatenate([y_at_p, y_at_s.astype(BF16)], axis=0)

        merged = _branch(y_rw, y_gm, y_at, w_br_rwkv, w_br_gmlp, w_br_attn, l, p_gate, TM * 2, 256)
        x, xg2, ssq2 = _matmul(merged, w_out, l, tm=TM * 2, tn=256, tk=D_MODEL, epilogue="residual", res=x,
                               next_gain=norm2[l], name="out_proj")
        act = _matmul(xg2, w_ff1, l, tm=TM * 2, tn=512, tk=D_MODEL, epilogue="relu2", out_dtype=BF16,
                      row_ssq=ssq2, name="ffn_up")
        ffn_down = functools.partial(_matmul, act, w_ff2, l, tm=TM * 2, tn=1024, tk=1024, epilogue="residual",
                                     res=x, name="ffn_down")
        if l + 1 < depth:
            x, xg, ssq = ffn_down(next_gain=norm1[l + 1])
        else:
            x = ffn_down()

    return (x[:MP].reshape(BP, TP, D_MODEL), x[MP:].reshape(BS, TS, D_MODEL),
            jnp.stack(kvp[0], 0), jnp.stack(kvp[1], 0), jnp.stack(kvp[2], 0),
            jnp.stack(kvs[0], 0), jnp.stack(kvs[1], 0), jnp.stack(kvs[2], 0),
            jnp.stack(st_p, 0), jnp.stack(st_s, 0), jnp.stack(sh_p, 0), jnp.stack(sh_s, 0),
            jnp.stack(gmv_s, 0))
```

```python
import functools

import jax
import jax.numpy as jnp
from jax import lax
from jax.experimental import pallas as pl
from jax.experimental.pallas import tpu as pltpu

F32 = jnp.float32
BF16 = jnp.bfloat16

LANES = 128
V7X_VMEM_LIMIT = 56 * 1024 * 1024

D_MODEL = 4096
RW_HEADS = 24
RW_HEAD = 64
RW_WIDTH = RW_HEADS * RW_HEAD
DECAY_LORA = 64
AAA_LORA = 64
GATE_LORA = 128
RW_COLS = 3 * RW_WIDTH + DECAY_LORA + AAA_LORA + GATE_LORA
GN_EPS = 64e-5
GM_CHUNK = 128
GM_GROUPS = 12
GM_GROUP_DIM = 128
GM_WIDTH = GM_GROUPS * GM_GROUP_DIM
LN_EPS = 1e-5
DIL_PAIRS = ((128, 1), (512, 4), (2048, 16))
N_DIL = 3
ATT_GROUP_HEADS = 4
ATT_HEAD = 128
ATT_WIDTH = N_DIL * ATT_GROUP_HEADS * ATT_HEAD
ATT_OUT = ATT_GROUP_HEADS * ATT_HEAD
ATT_BLK = 128
ATTN_BATCH = 8
ROPE_THETA = 10000.0
N_BRANCH = 3
D_FF = 4 * D_MODEL
NORM_EPS = 1e-6
NEG_INF = -1e30
PAST_LEN = 8192

ROW_TILE = 688

HPG = 4
GW = HPG * RW_HEAD
NG = RW_HEADS // HPG
SCAN_C = 64

NN = (((1,), (0,)), ((), ()))
NT = (((1,), (1,)), ((), ()))


def _dot(a, b, dims=NN):
    return lax.dot_general(a, b, dims, preferred_element_type=F32)


def _dot1(a, b, dims=NN):
    return _dot(a.astype(BF16), b.astype(BF16), dims)


def _split2(a):
    hi = a.astype(BF16)
    return hi, (a - hi.astype(F32)).astype(BF16)


def _dot_exact_rhs(a, e):
    hi, lo = _split2(a)
    return _dot(hi, e) + _dot(lo, e)


def _dot_exact_lhs(e, a):
    hi, lo = _split2(a)
    return _dot(e, hi) + _dot(e, lo)


def _params(sem, vmem=None):
    return pltpu.CompilerParams(dimension_semantics=sem, vmem_limit_bytes=vmem)


def _sigmoid(x):
    return 1.0 / (1.0 + jnp.exp(-x))


def _mm_kernel(*refs, nk, epilogue, scaled, normed):
    refs = list(refs)
    a_ref, b_ref = refs.pop(0), refs.pop(0)
    ssq_ref = refs.pop(0) if scaled else None
    res_ref = refs.pop(0) if epilogue == "residual" else None
    gain_ref = refs.pop(0) if normed else None
    o_ref = refs.pop(0)
    dot = lambda: _dot(a_ref[...], b_ref[0].astype(BF16))
    j = pl.program_id(1)

    def emit_normed(x_new):
        xg_ref, ssq_out = refs
        xg_ref[...] = (x_new * gain_ref[...]).astype(xg_ref.dtype)
        row = jnp.broadcast_to(jnp.sum(x_new * x_new, axis=-1, keepdims=True), ssq_out.shape)

        @pl.when(j == 0)
        def _():
            ssq_out[...] = row

        @pl.when(j > 0)
        def _():
            ssq_out[...] += row

    if nk == 1:
        part = dot()
        if scaled:
            part = part * lax.rsqrt(ssq_ref[:, 0:1] * (1.0 / D_MODEL) + NORM_EPS)
        if epilogue == "relu2":
            part = jnp.square(jnp.maximum(part, 0.0))
        elif epilogue == "residual":
            part = part + res_ref[...]
        o_ref[...] = part.astype(o_ref.dtype)
        if normed:
            emit_normed(part)
    else:
        assert epilogue == "residual" and not scaled
        k = pl.program_id(2)

        @pl.when(k == 0)
        def _():
            o_ref[...] = res_ref[...] + dot()

        @pl.when(k > 0)
        def _():
            o_ref[...] = o_ref[...] + dot()

        if normed:
            @pl.when(k == nk - 1)
            def _():
                emit_normed(o_ref[...])


def _matmul(a, b, layer, *, col_off=0, n_cols=None, tm, tn, tk, epilogue="none", res=None, out_dtype=F32,
            row_ssq=None, next_gain=None, name="matmul"):
    M, K = a.shape
    n_cols = b.shape[2] if n_cols is None else n_cols
    assert M % tm == 0 and n_cols % tn == 0 and K % tk == 0 and col_off % LANES == 0
    nk = K // tk
    assert nk == 1 or out_dtype == F32
    scaled, normed = row_ssq is not None, next_gain is not None
    in_specs = [pl.BlockSpec((tm, tk), lambda i, j, k: (i, k)),
                pl.BlockSpec((pl.Element(1), pl.Element(tk), pl.Element(tn)),
                             lambda i, j, k: (layer, pl.multiple_of(k * tk, tk),
                                              pl.multiple_of(col_off + j * tn, LANES)))]
    args = [a, b]
    tile = pl.BlockSpec((tm, tn), lambda i, j, k: (i, j))
    stat = pl.BlockSpec((tm, LANES), lambda i, j, k: (i, 0))
    if scaled:
        in_specs.append(stat)
        args.append(row_ssq)
    if epilogue == "residual":
        in_specs.append(tile)
        args.append(res)
    out_shape = [jax.ShapeDtypeStruct((M, n_cols), out_dtype)]
    out_specs = [tile]
    if normed:
        in_specs.append(pl.BlockSpec((1, tn), lambda i, j, k: (0, j)))
        args.append(next_gain.reshape(1, n_cols))
        out_shape += [jax.ShapeDtypeStruct((M, n_cols), BF16), jax.ShapeDtypeStruct((M, LANES), F32)]
        out_specs += [tile, stat]
    outs = pl.pallas_call(
        functools.partial(_mm_kernel, nk=nk, epilogue=epilogue, scaled=scaled, normed=normed),
        out_shape=out_shape,
        grid=(M // tm, n_cols // tn, nk),
        in_specs=in_specs,
        out_specs=out_specs,
        compiler_params=_params(("parallel", "arbitrary" if normed else "parallel", "arbitrary"), V7X_VMEM_LIMIT),
        name=name,
    )(*args)
    return outs if normed else outs[0]


def _norm_operand_kernel(x_ref, g_ref, xg_ref, ssq_ref):
    x = x_ref[...]
    xg_ref[...] = (x * g_ref[...]).astype(xg_ref.dtype)
    ssq_ref[...] = jnp.broadcast_to(jnp.sum(x * x, axis=-1, keepdims=True), ssq_ref.shape)


def _norm_operand(x, g, tm):
    M, D = x.shape
    return pl.pallas_call(
        _norm_operand_kernel,
        name="norm_operand",
        out_shape=[jax.ShapeDtypeStruct((M, D), BF16), jax.ShapeDtypeStruct((M, LANES), F32)],
        grid=(M // tm,),
        in_specs=[pl.BlockSpec((tm, D), lambda i: (i, 0)), pl.BlockSpec((1, D), lambda i: (0, 0))],
        out_specs=[pl.BlockSpec((tm, D), lambda i: (i, 0)), pl.BlockSpec((tm, LANES), lambda i: (i, 0))],
        compiler_params=_params(("parallel",), V7X_VMEM_LIMIT),
    )(x, g.reshape(1, D))


def _head_ones(width, head):
    r = lax.broadcasted_iota(jnp.int32, (width, width), 0) // head
    c = lax.broadcasted_iota(jnp.int32, (width, width), 1) // head
    return jnp.where(r == c, 1.0, 0.0).astype(BF16)


def _rwkv_prep_rows(p, carry_ref, mu_ref, w0_ref, wup_ref, a0_ref, aup_ref, gup_ref, kk_ref, ka_ref, pad_to):
    tm = p.shape[0]
    row = lax.broadcasted_iota(jnp.int32, p.shape, 0)
    p_prev = jnp.where(row == 0, carry_ref[...], pltpu.roll(p, 1, axis=0))
    carry_ref[...] = p[tm - 1:tm, :]
    ps = p + mu_ref[...] * (p_prev - p)

    W = RW_WIDTH
    r = ps[:, 0:W]
    k = ps[:, W:2 * W]
    v = ps[:, 2 * W:3 * W]
    wa = ps[:, 3 * W:3 * W + LANES]
    g_in = ps[:, 3 * W + LANES:3 * W + 2 * LANES]

    z = -(w0_ref[...] + _dot(jnp.tanh(wa).astype(BF16), wup_ref[...].astype(BF16)))
    softplus = jnp.maximum(z, 0.0) + jnp.log1p(jnp.exp(-jnp.abs(z)))
    lw = -jnp.exp(-softplus - 0.5)
    a = _sigmoid(a0_ref[...] + _dot(wa.astype(BF16), aup_ref[...].astype(BF16)))
    g = _dot(_sigmoid(g_in).astype(BF16), gup_ref[...].astype(BF16))

    kk = k * kk_ref[...]
    ones = _head_ones(GW, RW_HEAD)
    kmod = k * (1.0 + (a - 1.0) * ka_ref[...])
    kn = []
    for q in range(NG):
        kq = kk[:, q * GW:(q + 1) * GW]
        ssq = _dot_exact_rhs(kq * kq, ones)
        kn.append(kq * lax.rsqrt(jnp.maximum(ssq, 1e-24)))
    kn = jnp.concatenate(kn, axis=1)
    vals = (r, lw, kmod, v, kn, kn * a, g)
    if pad_to > tm:
        zeros = jnp.zeros((pad_to - tm, RW_WIDTH), F32)
        vals = tuple(jnp.concatenate([x, zeros], axis=0) for x in vals)
    return vals


def _stack_heads(x, lane_head):
    return jnp.concatenate([jnp.where(lane_head == h, x, 0.0) for h in range(HPG)], axis=0)


def _rwkv_mix_kernel(p_ref, prev_ref, mu_ref, w0_ref, wup_ref, a0_ref, aup_ref, gup_ref, kk_ref, ka_ref,
                     s0_ref, lnw_ref, lnb_ref, rk_ref, y_ref, st_ref, s_scr, carry_ref, *, C):
    j = pl.program_id(1)

    @pl.when(j == 0)
    def _():
        s_scr[...] = s0_ref[0]
        carry_ref[...] = prev_ref[0]

    r_all, lw_all, k_all, v_all, kk_all, b_all, g_all = _rwkv_prep_rows(
        p_ref[...], carry_ref, mu_ref, w0_ref, wup_ref, a0_ref, aup_ref, gup_ref, kk_ref, ka_ref, C)

    R = HPG * C
    G = range(NG)
    lane_head = lax.broadcasted_iota(jnp.int32, (1, GW), 1) // RW_HEAD
    ii = lax.broadcasted_iota(jnp.int32, (R, R), 0)
    jj = lax.broadcasted_iota(jnp.int32, (R, R), 1)
    ti = lax.broadcasted_iota(jnp.int32, (C, C), 0)
    tj = lax.broadcasted_iota(jnp.int32, (C, C), 1)
    tril = jnp.where(ti >= tj, 1.0, 0.0).astype(BF16)
    ones = _head_ones(GW, RW_HEAD)
    strict = ii > jj
    incl = ii >= jj
    sls = [slice(q * GW, (q + 1) * GW) for q in G]

    r = [r_all[:, sl] for sl in sls]
    k = [k_all[:, sl] for sl in sls]
    v = [v_all[:, sl] for sl in sls]
    cum_all = _dot_exact_lhs(tril, lw_all)
    pc_all = jnp.exp(cum_all)
    pinv_all = jnp.exp(-cum_all)
    at_all = -kk_all * jnp.exp(cum_all - lw_all)
    bt_all = b_all * pinv_all
    kt_all = k_all * pinv_all
    rt_all = r_all * pc_all
    pend = [pc_all[C - 1:C, sl] for sl in sls]
    ar = [jnp.concatenate([_stack_heads(at_all[:, sl], lane_head),
                           _stack_heads(rt_all[:, sl], lane_head)], axis=0).astype(BF16) for sl in sls]
    bk = [jnp.concatenate([_stack_heads(bt_all[:, sl], lane_head),
                           _stack_heads(kt_all[:, sl], lane_head)], axis=0) for sl in sls]
    gram = [_dot(ar[q], bk[q].astype(BF16), NT) for q in G]
    a_ab = [jnp.where(strict, gram[q][:R, :R], 0.0) for q in G]
    a_kr = [jnp.concatenate([jnp.where(strict, gram[q][:R, R:], 0.0),
                             jnp.where(incl, gram[q][R:, R:], 0.0)], axis=0).astype(BF16) for q in G]
    a_rb = [jnp.where(incl, gram[q][R:, :R], 0.0).astype(BF16) for q in G]

    eye = jnp.where(ii == jj, 1.0, 0.0)
    t = [eye + jnp.where((ii >> 1) == (jj >> 1), a_ab[q], 0.0) for q in G]
    size = 2
    while size < C:
        sh = size.bit_length() - 1
        sel = ((ii >> (sh + 1)) == (jj >> (sh + 1))) & ((ii >> sh) != (jj >> sh))
        tb = [t[q].astype(BF16) for q in G]
        mid = [_dot(jnp.where(sel, a_ab[q], 0.0).astype(BF16), tb[q]) for q in G]
        t = [t[q] + _dot(tb[q], mid[q].astype(BF16)) for q in G]
        size *= 2

    s = [s_scr[q] for q in G]
    v_st = [_stack_heads(v[q], lane_head) for q in G]
    from_state = [_dot(ar[q], s[q].astype(BF16), NT) for q in G]
    from_v = [_dot(a_kr[q], v_st[q].astype(BF16)) for q in G]
    u = [_dot1(t[q], from_state[q][:R] + from_v[q][:R]) for q in G]
    y_st = [from_state[q][R:] + from_v[q][R:] + _dot(a_rb[q], u[q].astype(BF16)) for q in G]
    for q in G:
        uv = jnp.concatenate([u[q], v_st[q]], axis=0)
        s_scr[q] = s[q] * pend[q] + _dot1(uv.T, bk[q] * pend[q])

    for q in G:
        y = y_st[q][0:C]
        for h in range(1, HPG):
            y = y + y_st[q][h * C:(h + 1) * C]
        mean = _dot_exact_rhs(y, ones) * (1.0 / RW_HEAD)
        yc = y - mean
        var = _dot_exact_rhs(yc * yc, ones) * (1.0 / RW_HEAD)
        sl = sls[q]
        yn = yc * lax.rsqrt(var + GN_EPS) * lnw_ref[:, sl] + lnb_ref[:, sl]
        bonus = _dot_exact_rhs(r[q] * k[q] * rk_ref[:, sl], ones) * v[q]
        y_ref[:, sl] = ((yn + bonus) * g_all[:, sl]).astype(y_ref.dtype)

    @pl.when(j == pl.num_programs(1) - 1)
    def _():
        st_ref[0] = s_scr[...]


def _rwkv_mix(p_rw, row_off, B, T, tm, C, prev, s0, mu, w0, wup_pad, a0, aup_pad, gup, k_k, k_a, ln_w, ln_b, r_k):
    nt = T // tm
    off = row_off // tm
    assert row_off % tm == 0 and T % tm == 0 and tm <= C
    row = lambda n: pl.BlockSpec((1, n), lambda b, j: (0, 0))
    full = lambda a: pl.BlockSpec(a.shape, lambda b, j: (0, 0))
    st_spec = pl.BlockSpec((1, NG, GW, GW), lambda b, j: (b, 0, 0, 0))
    return pl.pallas_call(
        functools.partial(_rwkv_mix_kernel, C=C),
        name="rwkv_mix",
        out_shape=[jax.ShapeDtypeStruct((B * nt * C, RW_WIDTH), BF16),
                   jax.ShapeDtypeStruct((B, NG, GW, GW), F32)],
        grid=(B, nt),
        in_specs=[pl.BlockSpec((tm, RW_COLS), lambda b, j: (off + b * nt + j, 0)),
                  pl.BlockSpec((1, 1, RW_COLS), lambda b, j: (b, 0, 0)),
                  row(RW_COLS), row(RW_WIDTH), full(wup_pad), row(RW_WIDTH), full(aup_pad), full(gup),
                  row(RW_WIDTH), row(RW_WIDTH), st_spec, row(RW_WIDTH), row(RW_WIDTH), row(RW_WIDTH)],
        out_specs=[pl.BlockSpec((C, RW_WIDTH), lambda b, j: (b * nt + j, 0)), st_spec],
        scratch_shapes=[pltpu.VMEM((NG, GW, GW), F32), pltpu.VMEM((1, RW_COLS), F32)],
        compiler_params=_params(("parallel", "arbitrary"), V7X_VMEM_LIMIT),
    )(p_rw, prev.reshape(B, 1, RW_COLS), mu.reshape(1, -1), w0.reshape(1, -1), wup_pad, a0.reshape(1, -1),
      aup_pad, gup, k_k.reshape(1, -1), k_a.reshape(1, -1), s0, ln_w.reshape(1, -1), ln_b.reshape(1, -1),
      r_k.reshape(1, -1))


def _state_to_blockdiag(s):
    B = s.shape[0]
    s = s.reshape(B, NG, HPG, RW_HEAD, RW_HEAD)
    eye = jnp.eye(HPG, dtype=s.dtype)
    bd = s[:, :, :, :, None, :] * eye[None, None, :, None, :, None]
    return bd.reshape(B, NG, GW, GW)


def _blockdiag_to_state(bd):
    B = bd.shape[0]
    x = bd.reshape(B, NG, HPG, RW_HEAD, HPG, RW_HEAD)
    return jnp.stack([x[:, :, h, :, h, :] for h in range(HPG)], axis=2).reshape(B, RW_HEADS, RW_HEAD, RW_HEAD)


def _gelu(x):
    return 0.5 * x * (1.0 + lax.erf(x * (2.0 ** -0.5)))


def _gmlp_kernel(u_ref, v_ref, lng_ref, lnb_ref, ws_ref, bs_ref, y_ref, vn_ref, *, tm):
    u = _gelu(u_ref[...])
    vf = _gelu(v_ref[...])
    mean = jnp.mean(vf, axis=-1, keepdims=True)
    vc = vf - mean
    var = jnp.mean(vc * vc, axis=-1, keepdims=True)
    vn = vc * lax.rsqrt(var + LN_EPS) * lng_ref[...] + lnb_ref[...]
    vn_ref[...] = vn
    ti = lax.broadcasted_iota(jnp.int32, (GM_CHUNK, GM_CHUNK), 0)
    tj = lax.broadcasted_iota(jnp.int32, (GM_CHUNK, GM_CHUNK), 1)
    causal = ti >= tj
    rows = min(tm, GM_CHUNK)
    for g in range(GM_GROUPS):
        wm = jnp.where(causal, ws_ref[g], 0.0).astype(BF16)
        sl = slice(g * GM_GROUP_DIM, (g + 1) * GM_GROUP_DIM)
        for c in range(max(tm // GM_CHUNK, 1)):
            rs = slice(c * GM_CHUNK, c * GM_CHUNK + rows)
            vg = vn[rs, sl].astype(BF16)
            if rows < GM_CHUNK:
                vg = jnp.concatenate([vg, jnp.zeros((GM_CHUNK - rows, GM_GROUP_DIM), BF16)], axis=0)
            mixed = (_dot(wm, vg) + bs_ref[g])[:rows]
            y_ref[rs, sl] = (u[rs, sl] * mixed).astype(y_ref.dtype)


def _gmlp(p_gm, row_off, rows, tm, ln_g, ln_b, ws, layer, bs_b, y_dtype):
    off = row_off // tm
    assert row_off % tm == 0 and rows % tm == 0
    row = pl.BlockSpec((1, GM_WIDTH), lambda i: (0, 0))
    ws_spec = pl.BlockSpec((None, GM_GROUPS, GM_CHUNK, GM_CHUNK), lambda i: (layer, 0, 0, 0))
    return pl.pallas_call(
        functools.partial(_gmlp_kernel, tm=tm),
        name="gmlp",
        out_shape=[jax.ShapeDtypeStruct((rows, GM_WIDTH), y_dtype), jax.ShapeDtypeStruct((rows, GM_WIDTH), F32)],
        grid=(rows // tm,),
        in_specs=[pl.BlockSpec((tm, GM_WIDTH), lambda i: (off + i, 0)),
                  pl.BlockSpec((tm, GM_WIDTH), lambda i: (off + i, 1)),
                  row, row, ws_spec, ws_spec],
        out_specs=[pl.BlockSpec((tm, GM_WIDTH), lambda i: (i, 0))] * 2,
        compiler_params=_params(("parallel",), V7X_VMEM_LIMIT),
    )(p_gm, p_gm, ln_g.reshape(1, -1), ln_b.reshape(1, -1), ws, bs_b)


def _attn_prep_kernel(q_ref, k_ref, cos_ref, sin_ref, qg_ref, kg_ref, qo_ref, ko_ref):
    cos = cos_ref[...]
    sin = sin_ref[...]

    def norm_rope(x, gain):
        y = x * lax.rsqrt(jnp.mean(x * x, axis=-1, keepdims=True) + NORM_EPS) * gain
        return y * cos + pltpu.roll(y, ATT_HEAD // 2, axis=1) * sin

    for h in range(N_DIL * ATT_GROUP_HEADS):
        sl = slice(h * ATT_HEAD, (h + 1) * ATT_HEAD)
        qo_ref[:, sl] = norm_rope(q_ref[:, sl], qg_ref[...]).astype(qo_ref.dtype)
        ko_ref[:, sl] = norm_rope(k_ref[:, sl], kg_ref[...]).astype(ko_ref.dtype)


def _attn_prep(p_at, cos, sin, q_gain, k_gain, tm):
    M = p_at.shape[0]
    blk = lambda c: pl.BlockSpec((tm, ATT_WIDTH), lambda i: (i, c))
    tab = pl.BlockSpec((tm, ATT_HEAD), lambda i: (i, 0))
    gain = pl.BlockSpec((1, ATT_HEAD), lambda i: (0, 0))
    return pl.pallas_call(
        _attn_prep_kernel,
        name="attn_prep",
        out_shape=[jax.ShapeDtypeStruct((M, ATT_WIDTH), F32), jax.ShapeDtypeStruct((M, ATT_WIDTH), F32)],
        grid=(M // tm,),
        in_specs=[blk(0), blk(1), tab, tab, gain, gain],
        out_specs=[pl.BlockSpec((tm, ATT_WIDTH), lambda i: (i, 0))] * 2,
        compiler_params=_params(("parallel",), V7X_VMEM_LIMIT),
    )(p_at, p_at, cos, sin, q_gain.reshape(1, -1), k_gain.reshape(1, -1))


def _attn_prompt_kernel(q0, q1, q2, k0, k1, k2, v0, v1, v2, y_ref, o_scr, lse_scr, *, T):
    scale = ATT_HEAD ** -0.5
    qi = lax.broadcasted_iota(jnp.int32, (ATT_BLK, ATT_BLK), 0)
    kj = lax.broadcasted_iota(jnp.int32, (ATT_BLK, ATT_BLK), 1)
    refs = ((q0, k0, v0), (q1, k1, v1), (q2, k2, v2))
    for g, (window, dil) in enumerate(DIL_PAIRS):
        q_ref, k_ref, v_ref = refs[g]
        nb = T // dil // ATT_BLK

        def rows(r, n, dil=dil):
            if dil == 1:
                return pl.ds(n * ATT_BLK, ATT_BLK)
            return pl.ds(r + n * ATT_BLK * dil, ATT_BLK, stride=dil)

        blocks = [(rows(r, n), rows(r, n - 1) if n > 0 else None) for r in range(dil) for n in range(nb)]
        for b0 in range(0, len(blocks), ATTN_BATCH):
            batch = blocks[b0:b0 + ATTN_BATCH]
            q = [q_ref[rs, :].astype(BF16) for rs, _ in batch]
            s_c = [jnp.where(kj <= qi, _dot(q[i], k_ref[rs, :].astype(BF16), NT) * scale, NEG_INF)
                   for i, (rs, _) in enumerate(batch)]
            s_p = [None if ps_ is None else
                   jnp.where(kj >= qi, _dot(q[i], k_ref[ps_, :].astype(BF16), NT) * scale, NEG_INF)
                   for i, (_, ps_) in enumerate(batch)]
            m = [jnp.max(s, axis=-1, keepdims=True) for s in s_c]
            m = [mc if sp is None else jnp.maximum(mc, jnp.max(sp, axis=-1, keepdims=True))
                 for mc, sp in zip(m, s_p)]
            e_c = [jnp.exp(s - mm) for s, mm in zip(s_c, m)]
            e_p = [None if sp is None else jnp.exp(sp - mm) for sp, mm in zip(s_p, m)]
            den = [jnp.sum(e, axis=-1, keepdims=True) for e in e_c]
            den = [d if e is None else d + jnp.sum(e, axis=-1, keepdims=True) for d, e in zip(den, e_p)]
            acc = [_dot(e.astype(BF16), v_ref[rs, :].astype(BF16)) for e, (rs, _) in zip(e_c, batch)]
            acc = [a if e is None else a + _dot(e.astype(BF16), v_ref[ps_, :].astype(BF16))
                   for a, e, (_, ps_) in zip(acc, e_p, batch)]
            for i, (rs, _) in enumerate(batch):
                o_scr[g, rs, :] = acc[i] / den[i]
                lse_scr[g, rs, :] = jnp.broadcast_to(m[i] + jnp.log(den[i]), (ATT_BLK, ATT_HEAD))
    a, b, c = lse_scr[0], lse_scr[1], lse_scr[2]
    m = jnp.maximum(jnp.maximum(a, b), c)
    wa, wb, wc = jnp.exp(a - m), jnp.exp(b - m), jnp.exp(c - m)
    y = (wa * o_scr[0] + wb * o_scr[1] + wc * o_scr[2]) / (wa + wb + wc)
    y_ref[...] = y.astype(y_ref.dtype)


def _attn_prompt(q_rot, k_rot, p_at, B, T):
    nh = ATT_GROUP_HEADS
    col = lambda g, base: pl.BlockSpec((T, ATT_HEAD), lambda b, h: (b, base + g * nh + h))
    vbase = 2 * ATT_WIDTH // ATT_HEAD
    return pl.pallas_call(
        functools.partial(_attn_prompt_kernel, T=T),
        name="attn_prompt",
        out_shape=jax.ShapeDtypeStruct((B * T, ATT_OUT), BF16),
        grid=(B, nh),
        in_specs=[col(g, 0) for g in range(N_DIL)] * 2 + [col(g, vbase) for g in range(N_DIL)],
        out_specs=pl.BlockSpec((T, ATT_HEAD), lambda b, h: (b, h)),
        scratch_shapes=[pltpu.VMEM((N_DIL, T, ATT_HEAD), F32)] * 2,
        compiler_params=_params(("parallel", "parallel"), V7X_VMEM_LIMIT),
    )(q_rot, q_rot, q_rot, k_rot, k_rot, k_rot, p_at, p_at, p_at)


def _attn_sample_kernel(q_ref, k_ref, v_ref, c0_ref, c1_ref, c2_ref, y_ref, *, T):
    scale = ATT_HEAD ** -0.5
    caches = (c0_ref, c1_ref, c2_ref)
    zpad = jnp.zeros((LANES - T, ATT_HEAD), F32)
    outs = [[None] * N_DIL for _ in range(ATT_GROUP_HEADS)]
    lses = [[None] * N_DIL for _ in range(ATT_GROUP_HEADS)]
    for g, (window, dil) in enumerate(DIL_PAIRS):
        cref = caches[g]
        wb = cref.shape[1]
        nkeys = wb + LANES
        t = lax.broadcasted_iota(jnp.int32, (T, nkeys), 0)
        jrow = lax.broadcasted_iota(jnp.int32, (T, nkeys), 1)
        dist = wb + t - jrow
        valid = (dist >= 0) & (dist <= window) & ((dist & (dil - 1)) == 0)
        for h in range(ATT_GROUP_HEADS):
            sl = slice((g * ATT_GROUP_HEADS + h) * ATT_HEAD, (g * ATT_GROUP_HEADS + h + 1) * ATT_HEAD)
            q = q_ref[:, sl].astype(BF16)
            kcat = jnp.concatenate([cref[0, :, h * ATT_HEAD:(h + 1) * ATT_HEAD],
                                    k_ref[:, sl], zpad], axis=0).astype(BF16)
            vcat = jnp.concatenate([cref[0, :, ATT_OUT + h * ATT_HEAD:ATT_OUT + (h + 1) * ATT_HEAD],
                                    v_ref[:, sl], zpad], axis=0).astype(BF16)
            s = jnp.where(valid, _dot(q, kcat, NT) * scale, NEG_INF)
            m = jnp.max(s, axis=-1, keepdims=True)
            lse = m + jnp.log(jnp.sum(jnp.exp(s - m), axis=-1, keepdims=True))
            outs[h][g] = _dot(jnp.exp(s - lse).astype(BF16), vcat)
            lses[h][g] = lse
    for h in range(ATT_GROUP_HEADS):
        m = jnp.maximum(jnp.maximum(lses[h][0], lses[h][1]), lses[h][2])
        w = [jnp.exp(l - m) for l in lses[h]]
        y = (w[0] * outs[h][0] + w[1] * outs[h][1] + w[2] * outs[h][2]) / (w[0] + w[1] + w[2])
        y_ref[:, h * ATT_HEAD:(h + 1) * ATT_HEAD] = y.astype(y_ref.dtype)


def _attn_sample(q_rot, k_rot, p_at, row_off, B, T, caches, layer):
    off = row_off // T
    blk = lambda c: pl.BlockSpec((T, ATT_WIDTH), lambda b: (off + b, c))
    cspec = lambda a: pl.BlockSpec((None, 1) + a.shape[2:], lambda b: (layer, b, 0, 0))
    return pl.pallas_call(
        functools.partial(_attn_sample_kernel, T=T),
        name="attn_sample",
        out_shape=jax.ShapeDtypeStruct((B * T, ATT_OUT), F32),
        grid=(B,),
        in_specs=[blk(0), blk(0), blk(2)] + [cspec(c) for c in caches],
        out_specs=pl.BlockSpec((T, ATT_OUT), lambda b: (b, 0)),
        compiler_params=_params(("parallel",), V7X_VMEM_LIMIT),
    )(q_rot, k_rot, p_at, *caches)


def _branch_kernel(yr_ref, yg_ref, ya_ref, wr_ref, wg_ref, wa_ref, g0_ref, g1_ref, g2_ref, o_ref):
    acc = _sigmoid(g0_ref[...]) * _dot(yr_ref[...], wr_ref[...].astype(BF16))
    acc = acc + _sigmoid(g1_ref[...]) * _dot(yg_ref[...], wg_ref[...].astype(BF16))
    acc = acc + _sigmoid(g2_ref[...]) * _dot(ya_ref[...], wa_ref[...].astype(BF16))
    o_ref[...] = acc.astype(o_ref.dtype)


def _branch(y_rw, y_gm, y_at, w_rw, w_gm, w_at, layer, p_gate, tm, tn):
    M = y_rw.shape[0]
    nb = D_MODEL // tn
    lhs = lambda kdim: pl.BlockSpec((tm, kdim), lambda i, j: (i, 0))
    rhs = lambda kdim: pl.BlockSpec((None, kdim, tn), lambda i, j: (layer, 0, j))
    gate = lambda br: pl.BlockSpec((tm, tn), lambda i, j: (i, br * nb + j))
    return pl.pallas_call(
        _branch_kernel,
        name="branch_merge",
        out_shape=jax.ShapeDtypeStruct((M, D_MODEL), BF16),
        grid=(M // tm, nb),
        in_specs=[lhs(RW_WIDTH), lhs(GM_WIDTH), lhs(ATT_OUT), rhs(RW_WIDTH), rhs(GM_WIDTH), rhs(ATT_OUT),
                  gate(0), gate(1), gate(2)],
        out_specs=pl.BlockSpec((tm, tn), lambda i, j: (i, j)),
        compiler_params=_params(("parallel", "parallel"), V7X_VMEM_LIMIT),
    )(y_rw, y_gm, y_at, w_rw, w_gm, w_at, p_gate, p_gate, p_gate)


def _rope_tables(pos):
    half = ATT_HEAD // 2
    inv = ROPE_THETA ** (-jnp.arange(half, dtype=F32) / half)
    ang = pos.astype(F32)[:, None] * inv[None, :]
    cos, sin = jnp.cos(ang), jnp.sin(ang)
    return jnp.concatenate([cos, cos], -1), jnp.concatenate([-sin, sin], -1)


def _kv_rows(k_rot, p_at, row0, B, T, keep, g):
    def last(x, c):
        x = lax.slice(x, (row0, c), (row0 + B * T, c + ATT_OUT)).reshape(B, T, ATT_OUT)[:, T - keep:]
        return jnp.stack([x[:, :, h * ATT_HEAD:(h + 1) * ATT_HEAD] for h in range(ATT_GROUP_HEADS)], axis=2)
    return jnp.stack([last(k_rot, g * ATT_OUT), last(p_at, 2 * ATT_WIDTH + g * ATT_OUT)], axis=2)


def kernel(x_prompt, x_sample, cache_kv_w128, cache_kv_w512, cache_kv_w2048, state_rwkv, state_rwkv_shift, norm1, w_in, rw_mu, rw_w0, rw_w_up, rw_a0, rw_a_up, rw_g_up, rw_k_k, rw_k_a, rw_r_k, rw_ln_w, rw_ln_b, gm_ln_g, gm_ln_b, gm_ws, gm_bs, att_q_gain, att_k_gain, w_br_rwkv, w_br_gmlp, w_br_attn, w_out, norm2, w_ff1, w_ff2):
    BP, TP, _ = x_prompt.shape
    BS, TS, _ = x_sample.shape
    depth = w_in.shape[0]
    MP, MS = BP * TP, BS * TS
    M = MP + MS
    TM = ROW_TILE
    assert M % (2 * TM) == 0 and MP % 1024 == 0 and TP % SCAN_C == 0 and TS <= SCAN_C
    x = jnp.concatenate([x_prompt.reshape(MP, D_MODEL), x_sample.reshape(MS, D_MODEL)], axis=0)

    pos = jnp.concatenate([jnp.tile(jnp.arange(TP), BP), jnp.tile(PAST_LEN + jnp.arange(TS), BS)])
    cos_t, sin_t = _rope_tables(pos)
    caches_all = [c.reshape(depth, BS, c.shape[2], 2 * ATT_OUT) for c in (cache_kv_w128, cache_kv_w512, cache_kv_w2048)]
    zeros_state = jnp.zeros((BP, NG, GW, GW), F32)
    zeros_shift = jnp.zeros((BP, RW_COLS), F32)
    bs_b = jnp.broadcast_to(gm_bs[:, :, :, None], (depth, GM_GROUPS, GM_CHUNK, GM_GROUP_DIM))
    c_rw, c_gm, c_at = RW_COLS, RW_COLS + 2 * GM_WIDTH, RW_COLS + 2 * GM_WIDTH + 3 * ATT_WIDTH

    kvp = [[] for _ in range(N_DIL)]
    kvs = [[] for _ in range(N_DIL)]
    st_p, st_s, sh_p, sh_s, gmv_s = [], [], [], [], []
    for l in range(depth):
        if l == 0:
            xg, ssq = _norm_operand(x, norm1[0], TM)
        mm_in = functools.partial(_matmul, xg, w_in, l, tm=TM * 2, tk=D_MODEL, row_ssq=ssq)
        p_rw = mm_in(col_off=0, n_cols=RW_COLS, tn=256, name="proj_rwkv")
        p_gm = mm_in(col_off=c_rw, n_cols=2 * GM_WIDTH, tn=512, name="proj_gmlp")
        p_at = mm_in(col_off=c_gm, n_cols=3 * ATT_WIDTH, tn=512, name="proj_attn")
        p_gate = mm_in(col_off=c_at, n_cols=N_BRANCH * D_MODEL, tn=512, name="proj_gates")

        zpad = jnp.zeros((DECAY_LORA, RW_WIDTH), F32)
        wup_pad = jnp.concatenate([rw_w_up[l], zpad], axis=0)
        aup_pad = jnp.concatenate([zpad, rw_a_up[l]], axis=0)
        mix = functools.partial(_rwkv_mix, p_rw, C=SCAN_C, mu=rw_mu[l], w0=rw_w0[l], wup_pad=wup_pad, a0=rw_a0[l],
                                aup_pad=aup_pad, gup=rw_g_up[l], k_k=rw_k_k[l].reshape(-1), k_a=rw_k_a[l].reshape(-1),
                                ln_w=rw_ln_w[l].reshape(-1), ln_b=rw_ln_b[l].reshape(-1), r_k=rw_r_k[l].reshape(-1))
        y_rw_p, sT_p = mix(row_off=0, B=BP, T=TP, tm=SCAN_C, prev=zeros_shift, s0=zeros_state)
        y_rw_s, sT_s = mix(row_off=MP, B=BS, T=TS, tm=TS, prev=state_rwkv_shift[l],
                           s0=_state_to_blockdiag(state_rwkv[l]))
        y_rw_s = y_rw_s.reshape(BS, SCAN_C, RW_WIDTH)[:, :TS].reshape(MS, RW_WIDTH)
        y_rw = jnp.concatenate([y_rw_p, y_rw_s], axis=0)
        st_p.append(_blockdiag_to_state(sT_p))
        st_s.append(_blockdiag_to_state(sT_s))
        sh_p.append(p_rw[TP - 1:MP:TP])
        sh_s.append(p_rw[MP + TS - 1::TS])

        y_gm_p, _ = _gmlp(p_gm, 0, MP, 256, gm_ln_g[l], gm_ln_b[l], gm_ws, l, bs_b, BF16)
        y_gm_s, vn_s = _gmlp(p_gm, MP, MS, TS, gm_ln_g[l], gm_ln_b[l], gm_ws, l, bs_b, F32)
        y_gm = jnp.concatenate([y_gm_p, y_gm_s.astype(BF16)], axis=0)
        gmv_s.append(vn_s.reshape(BS, TS, GM_WIDTH))

        q_rot, k_rot = _attn_prep(p_at, cos_t, sin_t, att_q_gain[l], att_k_gain[l], TM)
        for g, (window, dil) in enumerate(DIL_PAIRS):
            keep = min(window, TP)
            kvp[g].append(_kv_rows(k_rot, p_at, 0, BP, TP, keep, g))
            kvs[g].append(_kv_rows(k_rot, p_at, MP, BS, TS, TS, g))
        y_at_p = _attn_prompt(q_rot, k_rot, p_at, BP, TP)
        y_at_s = _attn_sample(q_rot, k_rot, p_at, MP, BS, TS, caches_all, l)
        y_at = jnp.concatenate([y_at_p, y_at_s.astype(BF16)], axis=0)

        merged = _branch(y_rw, y_gm, y_at, w_br_rwkv, w_br_gmlp, w_br_attn, l, p_gate, TM * 2, 256)
        x, xg2, ssq2 = _matmul(merged, w_out, l, tm=TM * 2, tn=256, tk=D_MODEL, epilogue="residual", res=x,
                               next_gain=norm2[l], name="out_proj")
        act = _matmul(xg2, w_ff1, l, tm=TM * 2, tn=512, tk=D_MODEL, epilogue="relu2", out_dtype=BF16,
                      row_ssq=ssq2, name="ffn_up")
        ffn_down = functools.partial(_matmul, act, w_ff2, l, tm=TM * 2, tn=1024, tk=1024, epilogue="residual",
                                     res=x, name="ffn_down")
        if l + 1 < depth:
            x, xg, ssq = ffn_down(next_gain=norm1[l + 1])
        else:
            x = ffn_down()

    return (x[:MP].reshape(BP, TP, D_MODEL), x[MP:].reshape(BS, TS, D_MODEL),
            jnp.stack(kvp[0], 0), jnp.stack(kvp[1], 0), jnp.stack(kvp[2], 0),
            jnp.stack(kvs[0], 0), jnp.stack(kvs[1], 0), jnp.stack(kvs[2], 0),
            jnp.stack(st_p, 0), jnp.stack(st_s, 0), jnp.stack(sh_p, 0), jnp.stack(sh_s, 0),
            jnp.stack(gmv_s, 0))
```

```python
import functools

import jax
import jax.numpy as jnp
from jax import lax
from jax.experimental import pallas as pl
from jax.experimental.pallas import tpu as pltpu

F32 = jnp.float32
BF16 = jnp.bfloat16

LANES = 128
V7X_VMEM_LIMIT = 56 * 1024 * 1024

D_MODEL = 4096
RW_HEADS = 24
RW_HEAD = 64
RW_WIDTH = RW_HEADS * RW_HEAD
DECAY_LORA = 64
AAA_LORA = 64
GATE_LORA = 128
RW_COLS = 3 * RW_WIDTH + DECAY_LORA + AAA_LORA + GATE_LORA
GN_EPS = 64e-5
GM_CHUNK = 128
GM_GROUPS = 12
GM_GROUP_DIM = 128
GM_WIDTH = GM_GROUPS * GM_GROUP_DIM
LN_EPS = 1e-5
DIL_PAIRS = ((128, 1), (512, 4), (2048, 16))
N_DIL = 3
ATT_GROUP_HEADS = 4
ATT_HEAD = 128
ATT_WIDTH = N_DIL * ATT_GROUP_HEADS * ATT_HEAD
ATT_OUT = ATT_GROUP_HEADS * ATT_HEAD
ATT_BLK = 128
ATTN_BATCH = 8
ROPE_THETA = 10000.0
N_BRANCH = 3
D_FF = 4 * D_MODEL
NORM_EPS = 1e-6
NEG_INF = -1e30
PAST_LEN = 8192

ROW_TILE = 688

HPG = 4
GW = HPG * RW_HEAD
NG = RW_HEADS // HPG
SCAN_C = 64

NN = (((1,), (0,)), ((), ()))
NT = (((1,), (1,)), ((), ()))


def _dot(a, b, dims=NN):
    return lax.dot_general(a, b, dims, preferred_element_type=F32)


def _dot1(a, b, dims=NN):
    return _dot(a.astype(BF16), b.astype(BF16), dims)


def _split2(a):
    hi = a.astype(BF16)
    return hi, (a - hi.astype(F32)).astype(BF16)


def _dot_exact_rhs(a, e):
    hi, lo = _split2(a)
    return _dot(hi, e) + _dot(lo, e)


def _dot_exact_lhs(e, a):
    hi, lo = _split2(a)
    return _dot(e, hi) + _dot(e, lo)


def _params(sem, vmem=None):
    return pltpu.CompilerParams(dimension_semantics=sem, vmem_limit_bytes=vmem)


def _sigmoid(x):
    return 1.0 / (1.0 + jnp.exp(-x))


def _mm_kernel(*refs, nk, epilogue, scaled, normed):
    refs = list(refs)
    a_ref, b_ref = refs.pop(0), refs.pop(0)
    ssq_ref = refs.pop(0) if scaled else None
    res_ref = refs.pop(0) if epilogue == "residual" else None
    gain_ref = refs.pop(0) if normed else None
    o_ref = refs.pop(0)
    dot = lambda: _dot(a_ref[...], b_ref[0].astype(BF16))
    j = pl.program_id(1)

    def emit_normed(x_new):
        xg_ref, ssq_out = refs
        xg_ref[...] = (x_new * gain_ref[...]).astype(xg_ref.dtype)
        row = jnp.broadcast_to(jnp.sum(x_new * x_new, axis=-1, keepdims=True), ssq_out.shape)

        @pl.when(j == 0)
        def _():
            ssq_out[...] = row

        @pl.when(j > 0)
        def _():
            ssq_out[...] += row

    if nk == 1:
        part = dot()
        if scaled:
            part = part * lax.rsqrt(ssq_ref[:, 0:1] * (1.0 / D_MODEL) + NORM_EPS)
        if epilogue == "relu2":
            part = jnp.square(jnp.maximum(part, 0.0))
        elif epilogue == "residual":
            part = part + res_ref[...]
        o_ref[...] = part.astype(o_ref.dtype)
        if normed:
            emit_normed(part)
    else:
        assert epilogue == "residual" and not scaled
        k = pl.program_id(2)

        @pl.when(k == 0)
        def _():
            o_ref[...] = res_ref[...] + dot()

        @pl.when(k > 0)
        def _():
            o_ref[...] = o_ref[...] + dot()

        if normed:
            @pl.when(k == nk - 1)
            def _():
                emit_normed(o_ref[...])


def _matmul(a, b, layer, *, col_off=0, n_cols=None, tm, tn, tk, epilogue="none", res=None, out_dtype=F32,
            row_ssq=None, next_gain=None, name="matmul"):
    M, K = a.shape
    n_cols = b.shape[2] if n_cols is None else n_cols
    assert M % tm == 0 and n_cols % tn == 0 and K % tk == 0 and col_off % LANES == 0
    nk = K // tk
    assert nk == 1 or out_dtype == F32
    scaled, normed = row_ssq is not None, next_gain is not None
    in_specs = [pl.BlockSpec((tm, tk), lambda i, j, k: (i, k)),
                pl.BlockSpec((pl.Element(1), pl.Element(tk), pl.Element(tn)),
                             lambda i, j, k: (layer, pl.multiple_of(k * tk, tk),
                                              pl.multiple_of(col_off + j * tn, LANES)))]
    args = [a, b]
    tile = pl.BlockSpec((tm, tn), lambda i, j, k: (i, j))
    stat = pl.BlockSpec((tm, LANES), lambda i, j, k: (i, 0))
    if scaled:
        in_specs.append(stat)
        args.append(row_ssq)
    if epilogue == "residual":
        in_specs.append(tile)
        args.append(res)
    out_shape = [jax.ShapeDtypeStruct((M, n_cols), out_dtype)]
    out_specs = [tile]
    if normed:
        in_specs.append(pl.BlockSpec((1, tn), lambda i, j, k: (0, j)))
        args.append(next_gain.reshape(1, n_cols))
        out_shape += [jax.ShapeDtypeStruct((M, n_cols), BF16), jax.ShapeDtypeStruct((M, LANES), F32)]
        out_specs += [tile, stat]
    outs = pl.pallas_call(
        functools.partial(_mm_kernel, nk=nk, epilogue=epilogue, scaled=scaled, normed=normed),
        out_shape=out_shape,
        grid=(M // tm, n_cols // tn, nk),
        in_specs=in_specs,
        out_specs=out_specs,
        compiler_params=_params(("parallel", "arbitrary" if normed else "parallel", "arbitrary"), V7X_VMEM_LIMIT),
        name=name,
    )(*args)
    return outs if normed else outs[0]


def _norm_operand_kernel(x_ref, g_ref, xg_ref, ssq_ref):
    x = x_ref[...]
    xg_ref[...] = (x * g_ref[...]).astype(xg_ref.dtype)
    ssq_ref[...] = jnp.broadcast_to(jnp.sum(x * x, axis=-1, keepdims=True), ssq_ref.shape)


def _norm_operand(x, g, tm):
    M, D = x.shape
    return pl.pallas_call(
        _norm_operand_kernel,
        name="norm_operand",
        out_shape=[jax.ShapeDtypeStruct((M, D), BF16), jax.ShapeDtypeStruct((M, LANES), F32)],
        grid=(M // tm,),
        in_specs=[pl.BlockSpec((tm, D), lambda i: (i, 0)), pl.BlockSpec((1, D), lambda i: (0, 0))],
        out_specs=[pl.BlockSpec((tm, D), lambda i: (i, 0)), pl.BlockSpec((tm, LANES), lambda i: (i, 0))],
        compiler_params=_params(("parallel",), V7X_VMEM_LIMIT),
    )(x, g.reshape(1, D))


def _head_ones(width, head):
    r = lax.broadcasted_iota(jnp.int32, (width, width), 0) // head
    c = lax.broadcasted_iota(jnp.int32, (width, width), 1) // head
    return jnp.where(r == c, 1.0, 0.0).astype(BF16)


def _rwkv_prep_rows(p, carry_ref, mu_ref, w0_ref, wup_ref, a0_ref, aup_ref, gup_ref, kk_ref, ka_ref, pad_to):
    tm = p.shape[0]
    row = lax.broadcasted_iota(jnp.int32, p.shape, 0)
    p_prev = jnp.where(row == 0, carry_ref[...], pltpu.roll(p, 1, axis=0))
    carry_ref[...] = p[tm - 1:tm, :]
    ps = p + mu_ref[...] * (p_prev - p)

    W = RW_WIDTH
    r = ps[:, 0:W]
    k = ps[:, W:2 * W]
    v = ps[:, 2 * W:3 * W]
    wa = ps[:, 3 * W:3 * W + LANES]
    g_in = ps[:, 3 * W + LANES:3 * W + 2 * LANES]

    z = -(w0_ref[...] + _dot(jnp.tanh(wa).astype(BF16), wup_ref[...].astype(BF16)))
    softplus = jnp.maximum(z, 0.0) + jnp.log1p(jnp.exp(-jnp.abs(z)))
    lw = -jnp.exp(-softplus - 0.5)
    a = _sigmoid(a0_ref[...] + _dot(wa.astype(BF16), aup_ref[...].astype(BF16)))
    g = _dot(_sigmoid(g_in).astype(BF16), gup_ref[...].astype(BF16))

    kk = k * kk_ref[...]
    ones = _head_ones(GW, RW_HEAD)
    kmod = k * (1.0 + (a - 1.0) * ka_ref[...])
    kn = []
    for q in range(NG):
        kq = kk[:, q * GW:(q + 1) * GW]
        ssq = _dot_exact_rhs(kq * kq, ones)
        kn.append(kq * lax.rsqrt(jnp.maximum(ssq, 1e-24)))
    kn = jnp.concatenate(kn, axis=1)
    vals = (r, lw, kmod, v, kn, kn * a, g)
    if pad_to > tm:
        zeros = jnp.zeros((pad_to - tm, RW_WIDTH), F32)
        vals = tuple(jnp.concatenate([x, zeros], axis=0) for x in vals)
    return vals


def _stack_heads(x, lane_head):
    return jnp.concatenate([jnp.where(lane_head == h, x, 0.0) for h in range(HPG)], axis=0)


def _rwkv_mix_kernel(p_ref, prev_ref, mu_ref, w0_ref, wup_ref, a0_ref, aup_ref, gup_ref, kk_ref, ka_ref,
                     s0_ref, lnw_ref, lnb_ref, rk_ref, y_ref, st_ref, s_scr, carry_ref, *, C):
    j = pl.program_id(1)

    @pl.when(j == 0)
    def _():
        s_scr[...] = s0_ref[0]
        carry_ref[...] = prev_ref[0]

    r_all, lw_all, k_all, v_all, kk_all, b_all, g_all = _rwkv_prep_rows(
        p_ref[...], carry_ref, mu_ref, w0_ref, wup_ref, a0_ref, aup_ref, gup_ref, kk_ref, ka_ref, C)

    R = HPG * C
    G = range(NG)
    lane_head = lax.broadcasted_iota(jnp.int32, (1, GW), 1) // RW_HEAD
    ii = lax.broadcasted_iota(jnp.int32, (R, R), 0)
    jj = lax.broadcasted_iota(jnp.int32, (R, R), 1)
    ti = lax.broadcasted_iota(jnp.int32, (C, C), 0)
    tj = lax.broadcasted_iota(jnp.int32, (C, C), 1)
    tril = jnp.where(ti >= tj, 1.0, 0.0).astype(BF16)
    ones = _head_ones(GW, RW_HEAD)
    strict = ii > jj
    incl = ii >= jj
    sls = [slice(q * GW, (q + 1) * GW) for q in G]

    r = [r_all[:, sl] for sl in sls]
    k = [k_all[:, sl] for sl in sls]
    v = [v_all[:, sl] for sl in sls]
    cum_all = _dot_exact_lhs(tril, lw_all)
    pc_all = jnp.exp(cum_all)
    pinv_all = jnp.exp(-cum_all)
    at_all = -kk_all * jnp.exp(cum_all - lw_all)
    bt_all = b_all * pinv_all
    kt_all = k_all * pinv_all
    rt_all = r_all * pc_all
    pend = [pc_all[C - 1:C, sl] for sl in sls]
    ar = [jnp.concatenate([_stack_heads(at_all[:, sl], lane_head),
                           _stack_heads(rt_all[:, sl], lane_head)], axis=0).astype(BF16) for sl in sls]
    bk = [jnp.concatenate([_stack_heads(bt_all[:, sl], lane_head),
                           _stack_heads(kt_all[:, sl], lane_head)], axis=0) for sl in sls]
    gram = [_dot(ar[q], bk[q].astype(BF16), NT) for q in G]
    a_ab = [jnp.where(strict, gram[q][:R, :R], 0.0) for q in G]
    a_kr = [jnp.concatenate([jnp.where(strict, gram[q][:R, R:], 0.0),
                             jnp.where(incl, gram[q][R:, R:], 0.0)], axis=0).astype(BF16) for q in G]
    a_rb = [jnp.where(incl, gram[q][R:, :R], 0.0).astype(BF16) for q in G]

    eye = jnp.where(ii == jj, 1.0, 0.0)
    t = [eye + jnp.where((ii >> 1) == (jj >> 1), a_ab[q], 0.0) for q in G]
    size = 2
    while size < C:
        sh = size.bit_length() - 1
        sel = ((ii >> (sh + 1)) == (jj >> (sh + 1))) & ((ii >> sh) != (jj >> sh))
        tb = [t[q].astype(BF16) for q in G]
        mid = [_dot(jnp.where(sel, a_ab[q], 0.0).astype(BF16), tb[q]) for q in G]
        t = [t[q] + _dot(tb[q], mid[q].astype(BF16)) for q in G]
        size *= 2

    s = [s_scr[q] for q in G]
    v_st = [_stack_heads(v[q], lane_head) for q in G]
    from_state = [_dot(ar[q], s[q].astype(BF16), NT) for q in G]
    from_v = [_dot(a_kr[q], v_st[q].astype(BF16)) for q in G]
    u = [_dot1(t[q], from_state[q][:R] + from_v[q][:R]) for q in G]
    y_st = [from_state[q][R:] + from_v[q][R:] + _dot(a_rb[q], u[q].astype(BF16)) for q in G]
    for q in G:
        uv = jnp.concatenate([u[q], v_st[q]], axis=0)
        s_scr[q] = s[q] * pend[q] + _dot1(uv.T, bk[q] * pend[q])

    for q in G:
        y = y_st[q][0:C]
        for h in range(1, HPG):
            y = y + y_st[q][h * C:(h + 1) * C]
        mean = _dot_exact_rhs(y, ones) * (1.0 / RW_HEAD)
        yc = y - mean
        var = _dot_exact_rhs(yc * yc, ones) * (1.0 / RW_HEAD)
        sl = sls[q]
        yn = yc * lax.rsqrt(var + GN_EPS) * lnw_ref[:, sl] + lnb_ref[:, sl]
        bonus = _dot_exact_rhs(r[q] * k[q] * rk_ref[:, sl], ones) * v[q]
        y_ref[:, sl] = ((yn + bonus) * g_all[:, sl]).astype(y_ref.dtype)

    @pl.when(j == pl.num_programs(1) - 1)
    def _():
        st_ref[0] = s_scr[...]


def _rwkv_mix(p_rw, row_off, B, T, tm, C, prev, s0, mu, w0, wup_pad, a0, aup_pad, gup, k_k, k_a, ln_w, ln_b, r_k):
    nt = T // tm
    off = row_off // tm
    assert row_off % tm == 0 and T % tm == 0 and tm <= C
    row = lambda n: pl.BlockSpec((1, n), lambda b, j: (0, 0))
    full = lambda a: pl.BlockSpec(a.shape, lambda b, j: (0, 0))
    st_spec = pl.BlockSpec((1, NG, GW, GW), lambda b, j: (b, 0, 0, 0))
    return pl.pallas_call(
        functools.partial(_rwkv_mix_kernel, C=C),
        name="rwkv_mix",
        out_shape=[jax.ShapeDtypeStruct((B * nt * C, RW_WIDTH), BF16),
                   jax.ShapeDtypeStruct((B, NG, GW, GW), F32)],
        grid=(B, nt),
        in_specs=[pl.BlockSpec((tm, RW_COLS), lambda b, j: (off + b * nt + j, 0)),
                  pl.BlockSpec((1, 1, RW_COLS), lambda b, j: (b, 0, 0)),
                  row(RW_COLS), row(RW_WIDTH), full(wup_pad), row(RW_WIDTH), full(aup_pad), full(gup),
                  row(RW_WIDTH), row(RW_WIDTH), st_spec, row(RW_WIDTH), row(RW_WIDTH), row(RW_WIDTH)],
        out_specs=[pl.BlockSpec((C, RW_WIDTH), lambda b, j: (b * nt + j, 0)), st_spec],
        scratch_shapes=[pltpu.VMEM((NG, GW, GW), F32), pltpu.VMEM((1, RW_COLS), F32)],
        compiler_params=_params(("parallel", "arbitrary"), V7X_VMEM_LIMIT),
    )(p_rw, prev.reshape(B, 1, RW_COLS), mu.reshape(1, -1), w0.reshape(1, -1), wup_pad, a0.reshape(1, -1),
      aup_pad, gup, k_k.reshape(1, -1), k_a.reshape(1, -1), s0, ln_w.reshape(1, -1), ln_b.reshape(1, -1),
      r_k.reshape(1, -1))


def _state_to_blockdiag(s):
    B = s.shape[0]
    s = s.reshape(B, NG, HPG, RW_HEAD, RW_HEAD)
    eye = jnp.eye(HPG, dtype=s.dtype)
    bd = s[:, :, :, :, None, :] * eye[None, None, :, None, :, None]
    return bd.reshape(B, NG, GW, GW)


def _blockdiag_to_state(bd):
    B = bd.shape[0]
    x = bd.reshape(B, NG, HPG, RW_HEAD, HPG, RW_HEAD)
    return jnp.stack([x[:, :, h, :, h, :] for h in range(HPG)], axis=2).reshape(B, RW_HEADS, RW_HEAD, RW_HEAD)


def _gelu(x):
    return 0.5 * x * (1.0 + lax.erf(x * (2.0 ** -0.5)))


def _gmlp_kernel(u_ref, v_ref, lng_ref, lnb_ref, ws_ref, bs_ref, y_ref, vn_ref, *, tm):
    u = _gelu(u_ref[...])
    vf = _gelu(v_ref[...])
    mean = jnp.mean(vf, axis=-1, keepdims=True)
    vc = vf - mean
    var = jnp.mean(vc * vc, axis=-1, keepdims=True)
    vn = vc * lax.rsqrt(var + LN_EPS) * lng_ref[...] + lnb_ref[...]
    vn_ref[...] = vn
    ti = lax.broadcasted_iota(jnp.int32, (GM_CHUNK, GM_CHUNK), 0)
    tj = lax.broadcasted_iota(jnp.int32, (GM_CHUNK, GM_CHUNK), 1)
    causal = ti >= tj
    rows = min(tm, GM_CHUNK)
    for g in range(GM_GROUPS):
        wm = jnp.where(causal, ws_ref[g], 0.0).astype(BF16)
        sl = slice(g * GM_GROUP_DIM, (g + 1) * GM_GROUP_DIM)
        for c in range(max(tm // GM_CHUNK, 1)):
            rs = slice(c * GM_CHUNK, c * GM_CHUNK + rows)
            vg = vn[rs, sl].astype(BF16)
            if rows < GM_CHUNK:
                vg = jnp.concatenate([vg, jnp.zeros((GM_CHUNK - rows, GM_GROUP_DIM), BF16)], axis=0)
            mixed = (_dot(wm, vg) + bs_ref[g])[:rows]
            y_ref[rs, sl] = (u[rs, sl] * mixed).astype(y_ref.dtype)


def _gmlp(p_gm, row_off, rows, tm, ln_g, ln_b, ws, layer, bs_b, y_dtype):
    off = row_off // tm
    assert row_off % tm == 0 and rows % tm == 0
    row = pl.BlockSpec((1, GM_WIDTH), lambda i: (0, 0))
    ws_spec = pl.BlockSpec((None, GM_GROUPS, GM_CHUNK, GM_CHUNK), lambda i: (layer, 0, 0, 0))
    return pl.pallas_call(
        functools.partial(_gmlp_kernel, tm=tm),
        name="gmlp",
        out_shape=[jax.ShapeDtypeStruct((rows, GM_WIDTH), y_dtype), jax.ShapeDtypeStruct((rows, GM_WIDTH), F32)],
        grid=(rows // tm,),
        in_specs=[pl.BlockSpec((tm, GM_WIDTH), lambda i: (off + i, 0)),
                  pl.BlockSpec((tm, GM_WIDTH), lambda i: (off + i, 1)),
                  row, row, ws_spec, ws_spec],
        out_specs=[pl.BlockSpec((tm, GM_WIDTH), lambda i: (i, 0))] * 2,
        compiler_params=_params(("parallel",), V7X_VMEM_LIMIT),
    )(p_gm, p_gm, ln_g.reshape(1, -1), ln_b.reshape(1, -1), ws, bs_b)


def _attn_prep_kernel(q_ref, k_ref, cos_ref, sin_ref, qg_ref, kg_ref, qo_ref, ko_ref):
    cos = cos_ref[...]
    sin = sin_ref[...]

    def norm_rope(x, gain):
        y = x * lax.rsqrt(jnp.mean(x * x, axis=-1, keepdims=True) + NORM_EPS) * gain
        return y * cos + pltpu.roll(y, ATT_HEAD // 2, axis=1) * sin

    for h in range(N_DIL * ATT_GROUP_HEADS):
        sl = slice(h * ATT_HEAD, (h + 1) * ATT_HEAD)
        qo_ref[:, sl] = norm_rope(q_ref[:, sl], qg_ref[...]).astype(qo_ref.dtype)
        ko_ref[:, sl] = norm_rope(k_ref[:, sl], kg_ref[...]).astype(ko_ref.dtype)


def _attn_prep(p_at, cos, sin, q_gain, k_gain, tm):
    M = p_at.shape[0]
    blk = lambda c: pl.BlockSpec((tm, ATT_WIDTH), lambda i: (i, c))
    tab = pl.BlockSpec((tm, ATT_HEAD), lambda i: (i, 0))
    gain = pl.BlockSpec((1, ATT_HEAD), lambda i: (0, 0))
    return pl.pallas_call(
        _attn_prep_kernel,
        name="attn_prep",
        out_shape=[jax.ShapeDtypeStruct((M, ATT_WIDTH), F32), jax.ShapeDtypeStruct((M, ATT_WIDTH), F32)],
        grid=(M // tm,),
        in_specs=[blk(0), blk(1), tab, tab, gain, gain],
        out_specs=[pl.BlockSpec((tm, ATT_WIDTH), lambda i: (i, 0))] * 2,
        compiler_params=_params(("parallel",), V7X_VMEM_LIMIT),
    )(p_at, p_at, cos, sin, q_gain.reshape(1, -1), k_gain.reshape(1, -1))


def _attn_prompt_kernel(q0, q1, q2, k0, k1, k2, v0, v1, v2, y_ref, o_scr, lse_scr, *, T):
    scale = ATT_HEAD ** -0.5
    qi = lax.broadcasted_iota(jnp.int32, (ATT_BLK, ATT_BLK), 0)
    kj = lax.broadcasted_iota(jnp.int32, (ATT_BLK, ATT_BLK), 1)
    refs = ((q0, k0, v0), (q1, k1, v1), (q2, k2, v2))
    for g, (window, dil) in enumerate(DIL_PAIRS):
        q_ref, k_ref, v_ref = refs[g]
        nb = T // dil // ATT_BLK

        def rows(r, n, dil=dil):
            if dil == 1:
                return pl.ds(n * ATT_BLK, ATT_BLK)
            return pl.ds(r + n * ATT_BLK * dil, ATT_BLK, stride=dil)

        blocks = [(rows(r, n), rows(r, n - 1) if n > 0 else None) for r in range(dil) for n in range(nb)]
        for b0 in range(0, len(blocks), ATTN_BATCH):
            batch = blocks[b0:b0 + ATTN_BATCH]
            q = [q_ref[rs, :].astype(BF16) for rs, _ in batch]
            s_c = [jnp.where(kj <= qi, _dot(q[i], k_ref[rs, :].astype(BF16), NT) * scale, NEG_INF)
                   for i, (rs, _) in enumerate(batch)]
            s_p = [None if ps_ is None else
                   jnp.where(kj >= qi, _dot(q[i], k_ref[ps_, :].astype(BF16), NT) * scale, NEG_INF)
                   for i, (_, ps_) in enumerate(batch)]
            m = [jnp.max(s, axis=-1, keepdims=True) for s in s_c]
            m = [mc if sp is None else jnp.maximum(mc, jnp.max(sp, axis=-1, keepdims=True))
                 for mc, sp in zip(m, s_p)]
            e_c = [jnp.exp(s - mm) for s, mm in zip(s_c, m)]
            e_p = [None if sp is None else jnp.exp(sp - mm) for sp, mm in zip(s_p, m)]
            den = [jnp.sum(e, axis=-1, keepdims=True) for e in e_c]
            den = [d if e is None else d + jnp.sum(e, axis=-1, keepdims=True) for d, e in zip(den, e_p)]
            acc = [_dot(e.astype(BF16), v_ref[rs, :].astype(BF16)) for e, (rs, _) in zip(e_c, batch)]
            acc = [a if e is None else a + _dot(e.astype(BF16), v_ref[ps_, :].astype(BF16))
                   for a, e, (_, ps_) in zip(acc, e_p, batch)]
            for i, (rs, _) in enumerate(batch):
                o_scr[g, rs, :] = acc[i] / den[i]
                lse_scr[g, rs, :] = jnp.broadcast_to(m[i] + jnp.log(den[i]), (ATT_BLK, ATT_HEAD))
    a, b, c = lse_scr[0], lse_scr[1], lse_scr[2]
    m = jnp.maximum(jnp.maximum(a, b), c)
    wa, wb, wc = jnp.exp(a - m), jnp.exp(b - m), jnp.exp(c - m)
    y = (wa * o_scr[0] + wb * o_scr[1] + wc * o_scr[2]) / (wa + wb + wc)
    y_ref[...] = y.astype(y_ref.dtype)


def _attn_prompt(q_rot, k_rot, p_at, B, T):
    nh = ATT_GROUP_HEADS
    col = lambda g, base: pl.BlockSpec((T, ATT_HEAD), lambda b, h: (b, base + g * nh + h))
    vbase = 2 * ATT_WIDTH // ATT_HEAD
    return pl.pallas_call(
        functools.partial(_attn_prompt_kernel, T=T),
        name="attn_prompt",
        out_shape=jax.ShapeDtypeStruct((B * T, ATT_OUT), BF16),
        grid=(B, nh),
        in_specs=[col(g, 0) for g in range(N_DIL)] * 2 + [col(g, vbase) for g in range(N_DIL)],
        out_specs=pl.BlockSpec((T, ATT_HEAD), lambda b, h: (b, h)),
        scratch_shapes=[pltpu.VMEM((N_DIL, T, ATT_HEAD), F32)] * 2,
        compiler_params=_params(("parallel", "parallel"), V7X_VMEM_LIMIT),
    )(q_rot, q_rot, q_rot, k_rot, k_rot, k_rot, p_at, p_at, p_at)


def _attn_sample_kernel(q_ref, k_ref, v_ref, c0_ref, c1_ref, c2_ref, y_ref, *, T):
    scale = ATT_HEAD ** -0.5
    caches = (c0_ref, c1_ref, c2_ref)
    zpad = jnp.zeros((LANES - T, ATT_HEAD), F32)
    outs = [[None] * N_DIL for _ in range(ATT_GROUP_HEADS)]
    lses = [[None] * N_DIL for _ in range(ATT_GROUP_HEADS)]
    for g, (window, dil) in enumerate(DIL_PAIRS):
        cref = caches[g]
        wb = cref.shape[1]
        nkeys = wb + LANES
        t = lax.broadcasted_iota(jnp.int32, (T, nkeys), 0)
        jrow = lax.broadcasted_iota(jnp.int32, (T, nkeys), 1)
        dist = wb + t - jrow
        valid = (dist >= 0) & (dist <= window) & ((dist & (dil - 1)) == 0)
        for h in range(ATT_GROUP_HEADS):
            sl = slice((g * ATT_GROUP_HEADS + h) * ATT_HEAD, (g * ATT_GROUP_HEADS + h + 1) * ATT_HEAD)
            q = q_ref[:, sl].astype(BF16)
            kcat = jnp.concatenate([cref[0, :, 0, h, :], k_ref[:, sl], zpad], axis=0).astype(BF16)
            vcat = jnp.concatenate([cref[0, :, 1, h, :], v_ref[:, sl], zpad], axis=0).astype(BF16)
            s = jnp.where(valid, _dot(q, kcat, NT) * scale, NEG_INF)
            m = jnp.max(s, axis=-1, keepdims=True)
            lse = m + jnp.log(jnp.sum(jnp.exp(s - m), axis=-1, keepdims=True))
            outs[h][g] = _dot(jnp.exp(s - lse).astype(BF16), vcat)
            lses[h][g] = lse
    for h in range(ATT_GROUP_HEADS):
        m = jnp.maximum(jnp.maximum(lses[h][0], lses[h][1]), lses[h][2])
        w = [jnp.exp(l - m) for l in lses[h]]
        y = (w[0] * outs[h][0] + w[1] * outs[h][1] + w[2] * outs[h][2]) / (w[0] + w[1] + w[2])
        y_ref[:, h * ATT_HEAD:(h + 1) * ATT_HEAD] = y.astype(y_ref.dtype)


def _attn_sample(q_rot, k_rot, p_at, row_off, B, T, caches, layer):
    off = row_off // T
    blk = lambda c: pl.BlockSpec((T, ATT_WIDTH), lambda b: (off + b, c))
    cspec = lambda a: pl.BlockSpec((None, 1) + a.shape[2:], lambda b: (layer, b, 0, 0, 0, 0))
    return pl.pallas_call(
        functools.partial(_attn_sample_kernel, T=T),
        name="attn_sample",
        out_shape=jax.ShapeDtypeStruct((B * T, ATT_OUT), F32),
        grid=(B,),
        in_specs=[blk(0), blk(0), blk(2)] + [cspec(c) for c in caches],
        out_specs=pl.BlockSpec((T, ATT_OUT), lambda b: (b, 0)),
        compiler_params=_params(("parallel",), V7X_VMEM_LIMIT),
    )(q_rot, k_rot, p_at, *caches)


def _branch_kernel(yr_ref, yg_ref, ya_ref, wr_ref, wg_ref, wa_ref, g0_ref, g1_ref, g2_ref, o_ref):
    acc = _sigmoid(g0_ref[...]) * _dot(yr_ref[...], wr_ref[...].astype(BF16))
    acc = acc + _sigmoid(g1_ref[...]) * _dot(yg_ref[...], wg_ref[...].astype(BF16))
    acc = acc + _sigmoid(g2_ref[...]) * _dot(ya_ref[...], wa_ref[...].astype(BF16))
    o_ref[...] = acc.astype(o_ref.dtype)


def _branch(y_rw, y_gm, y_at, w_rw, w_gm, w_at, layer, p_gate, tm, tn):
    M = y_rw.shape[0]
    nb = D_MODEL // tn
    lhs = lambda kdim: pl.BlockSpec((tm, kdim), lambda i, j: (i, 0))
    rhs = lambda kdim: pl.BlockSpec((None, kdim, tn), lambda i, j: (layer, 0, j))
    gate = lambda br: pl.BlockSpec((tm, tn), lambda i, j: (i, br * nb + j))
    return pl.pallas_call(
        _branch_kernel,
        name="branch_merge",
        out_shape=jax.ShapeDtypeStruct((M, D_MODEL), BF16),
        grid=(M // tm, nb),
        in_specs=[lhs(RW_WIDTH), lhs(GM_WIDTH), lhs(ATT_OUT), rhs(RW_WIDTH), rhs(GM_WIDTH), rhs(ATT_OUT),
                  gate(0), gate(1), gate(2)],
        out_specs=pl.BlockSpec((tm, tn), lambda i, j: (i, j)),
        compiler_params=_params(("parallel", "parallel"), V7X_VMEM_LIMIT),
    )(y_rw, y_gm, y_at, w_rw, w_gm, w_at, p_gate, p_gate, p_gate)


def _rope_tables(pos):
    half = ATT_HEAD // 2
    inv = ROPE_THETA ** (-jnp.arange(half, dtype=F32) / half)
    ang = pos.astype(F32)[:, None] * inv[None, :]
    cos, sin = jnp.cos(ang), jnp.sin(ang)
    return jnp.concatenate([cos, cos], -1), jnp.concatenate([-sin, sin], -1)


def _kv_rows_kernel(*refs):
    k_ref, v_ref, o_ref = refs[0], refs[1], refs[-1]
    for h in range(ATT_GROUP_HEADS):
        sl = slice(h * ATT_HEAD, (h + 1) * ATT_HEAD)
        o_ref[0, :, 0, h, :] = k_ref[:, sl]
        o_ref[0, :, 1, h, :] = v_ref[:, sl]


def _kv_rows(k_rot, p_at, buf, layer, depth, row0, B, T, keep, g):
    tk = min(keep, ATT_BLK)
    first = (row0 + T - keep) // tk
    assert (row0 + T - keep) % tk == 0 and T % tk == 0
    rows = lambda c: pl.BlockSpec((tk, ATT_OUT), lambda b, i: (first + b * (T // tk) + i, c))
    shape = (depth, B, keep, 2, ATT_GROUP_HEADS, ATT_HEAD)
    in_specs = [rows(g), rows(2 * ATT_WIDTH // ATT_OUT + g)]
    args = [k_rot, p_at]
    if buf is not None:
        in_specs.append(pl.BlockSpec(memory_space=pl.ANY))
        args.append(buf)
    return pl.pallas_call(
        _kv_rows_kernel,
        name="kv_rows",
        out_shape=jax.ShapeDtypeStruct(shape, F32),
        grid=(B, keep // tk),
        in_specs=in_specs,
        out_specs=pl.BlockSpec((None, 1, tk, 2, ATT_GROUP_HEADS, ATT_HEAD), lambda b, i: (layer, b, i, 0, 0, 0)),
        input_output_aliases={} if buf is None else {2: 0},
        compiler_params=_params(("parallel", "parallel"), V7X_VMEM_LIMIT),
    )(*args)


def kernel(x_prompt, x_sample, cache_kv_w128, cache_kv_w512, cache_kv_w2048, state_rwkv, state_rwkv_shift, norm1, w_in, rw_mu, rw_w0, rw_w_up, rw_a0, rw_a_up, rw_g_up, rw_k_k, rw_k_a, rw_r_k, rw_ln_w, rw_ln_b, gm_ln_g, gm_ln_b, gm_ws, gm_bs, att_q_gain, att_k_gain, w_br_rwkv, w_br_gmlp, w_br_attn, w_out, norm2, w_ff1, w_ff2):
    BP, TP, _ = x_prompt.shape
    BS, TS, _ = x_sample.shape
    depth = w_in.shape[0]
    MP, MS = BP * TP, BS * TS
    M = MP + MS
    TM = ROW_TILE
    assert M % (2 * TM) == 0 and MP % 1024 == 0 and TP % SCAN_C == 0 and TS <= SCAN_C
    x = jnp.concatenate([x_prompt.reshape(MP, D_MODEL), x_sample.reshape(MS, D_MODEL)], axis=0)

    pos = jnp.concatenate([jnp.tile(jnp.arange(TP), BP), jnp.tile(PAST_LEN + jnp.arange(TS), BS)])
    cos_t, sin_t = _rope_tables(pos)
    caches_all = [cache_kv_w128, cache_kv_w512, cache_kv_w2048]
    zeros_state = jnp.zeros((BP, NG, GW, GW), F32)
    zeros_shift = jnp.zeros((BP, RW_COLS), F32)
    bs_b = jnp.broadcast_to(gm_bs[:, :, :, None], (depth, GM_GROUPS, GM_CHUNK, GM_GROUP_DIM))
    c_rw, c_gm, c_at = RW_COLS, RW_COLS + 2 * GM_WIDTH, RW_COLS + 2 * GM_WIDTH + 3 * ATT_WIDTH

    kvp = [None] * N_DIL
    kvs = [None] * N_DIL
    st_p, st_s, sh_p, sh_s, gmv_s = [], [], [], [], []
    for l in range(depth):
        if l == 0:
            xg, ssq = _norm_operand(x, norm1[0], TM)
        mm_in = functools.partial(_matmul, xg, w_in, l, tm=TM * 2, tk=D_MODEL, row_ssq=ssq)
        p_rw = mm_in(col_off=0, n_cols=RW_COLS, tn=256, name="proj_rwkv")
        p_gm = mm_in(col_off=c_rw, n_cols=2 * GM_WIDTH, tn=512, name="proj_gmlp")
        p_at = mm_in(col_off=c_gm, n_cols=3 * ATT_WIDTH, tn=512, name="proj_attn")
        p_gate = mm_in(col_off=c_at, n_cols=N_BRANCH * D_MODEL, tn=512, name="proj_gates")

        zpad = jnp.zeros((DECAY_LORA, RW_WIDTH), F32)
        wup_pad = jnp.concatenate([rw_w_up[l], zpad], axis=0)
        aup_pad = jnp.concatenate([zpad, rw_a_up[l]], axis=0)
        mix = functools.partial(_rwkv_mix, p_rw, C=SCAN_C, mu=rw_mu[l], w0=rw_w0[l], wup_pad=wup_pad, a0=rw_a0[l],
                                aup_pad=aup_pad, gup=rw_g_up[l], k_k=rw_k_k[l].reshape(-1), k_a=rw_k_a[l].reshape(-1),
                                ln_w=rw_ln_w[l].reshape(-1), ln_b=rw_ln_b[l].reshape(-1), r_k=rw_r_k[l].reshape(-1))
        y_rw_p, sT_p = mix(row_off=0, B=BP, T=TP, tm=SCAN_C, prev=zeros_shift, s0=zeros_state)
        y_rw_s, sT_s = mix(row_off=MP, B=BS, T=TS, tm=TS, prev=state_rwkv_shift[l],
                           s0=_state_to_blockdiag(state_rwkv[l]))
        y_rw_s = y_rw_s.reshape(BS, SCAN_C, RW_WIDTH)[:, :TS].reshape(MS, RW_WIDTH)
        y_rw = jnp.concatenate([y_rw_p, y_rw_s], axis=0)
        st_p.append(_blockdiag_to_state(sT_p))
        st_s.append(_blockdiag_to_state(sT_s))
        sh_p.append(p_rw[TP - 1:MP:TP])
        sh_s.append(p_rw[MP + TS - 1::TS])

        y_gm_p, _ = _gmlp(p_gm, 0, MP, 256, gm_ln_g[l], gm_ln_b[l], gm_ws, l, bs_b, BF16)
        y_gm_s, vn_s = _gmlp(p_gm, MP, MS, TS, gm_ln_g[l], gm_ln_b[l], gm_ws, l, bs_b, F32)
        y_gm = jnp.concatenate([y_gm_p, y_gm_s.astype(BF16)], axis=0)
        gmv_s.append(vn_s.reshape(BS, TS, GM_WIDTH))

        q_rot, k_rot = _attn_prep(p_at, cos_t, sin_t, att_q_gain[l], att_k_gain[l], TM)
        for g, (window, dil) in enumerate(DIL_PAIRS):
            keep = min(window, TP)
            kvp[g] = _kv_rows(k_rot, p_at, kvp[g], l, depth, 0, BP, TP, keep, g)
            kvs[g] = _kv_rows(k_rot, p_at, kvs[g], l, depth, MP, BS, TS, TS, g)
        y_at_p = _attn_prompt(q_rot, k_rot, p_at, BP, TP)
        y_at_s = _attn_sample(q_rot, k_rot, p_at, MP, BS, TS, caches_all, l)
        y_at = jnp.concatenate([y_at_p, y_at_s.astype(BF16)], axis=0)

        merged = _branch(y_rw, y_gm, y_at, w_br_rwkv, w_br_gmlp, w_br_attn, l, p_gate, TM * 2, 256)
        x, xg2, ssq2 = _matmul(merged, w_out, l, tm=TM * 2, tn=256, tk=D_MODEL, epilogue="residual", res=x,
                               next_gain=norm2[l], name="out_proj")
        act = _matmul(xg2, w_ff1, l, tm=TM * 2, tn=512, tk=D_MODEL, epilogue="relu2", out_dtype=BF16,
                      row_ssq=ssq2, name="ffn_up")
        ffn_down = functools.partial(_matmul, act, w_ff2, l, tm=TM * 2, tn=1024, tk=1024, epilogue="residual",
                                     res=x, name="ffn_down")
        if l + 1 < depth:
            x, xg, ssq = ffn_down(next_gain=norm1[l + 1])
        else:
            x = ffn_down()

    return (x[:MP].reshape(BP, TP, D_MODEL), x[MP:].reshape(BS, TS, D_MODEL),
            kvp[0], kvp[1], kvp[2], kvs[0], kvs[1], kvs[2],
            jnp.stack(st_p, 0), jnp.stack(st_s, 0), jnp.stack(sh_p, 0), jnp.stack(sh_s, 0),
            jnp.stack(gmv_s, 0))
```

```python
import functools

import jax
import jax.numpy as jnp
from jax import lax
from jax.experimental import pallas as pl
from jax.experimental.pallas import tpu as pltpu

F32 = jnp.float32
BF16 = jnp.bfloat16

LANES = 128
V7X_VMEM_LIMIT = 56 * 1024 * 1024

D_MODEL = 4096
RW_HEADS = 24
RW_HEAD = 64
RW_WIDTH = RW_HEADS * RW_HEAD
DECAY_LORA = 64
AAA_LORA = 64
GATE_LORA = 128
RW_COLS = 3 * RW_WIDTH + DECAY_LORA + AAA_LORA + GATE_LORA
GN_EPS = 64e-5
GM_CHUNK = 128
GM_GROUPS = 12
GM_GROUP_DIM = 128
GM_WIDTH = GM_GROUPS * GM_GROUP_DIM
LN_EPS = 1e-5
DIL_PAIRS = ((128, 1), (512, 4), (2048, 16))
N_DIL = 3
ATT_GROUP_HEADS = 4
ATT_HEAD = 128
ATT_WIDTH = N_DIL * ATT_GROUP_HEADS * ATT_HEAD
ATT_OUT = ATT_GROUP_HEADS * ATT_HEAD
ATT_BLK = 128
ATTN_BATCH = 8
ROPE_THETA = 10000.0
N_BRANCH = 3
D_FF = 4 * D_MODEL
NORM_EPS = 1e-6
NEG_INF = -1e30
PAST_LEN = 8192

ROW_TILE = 688

HPG = 4
GW = HPG * RW_HEAD
NG = RW_HEADS // HPG
SCAN_C = 64

NN = (((1,), (0,)), ((), ()))
NT = (((1,), (1,)), ((), ()))


def _dot(a, b, dims=NN):
    return lax.dot_general(a, b, dims, preferred_element_type=F32)


def _dot1(a, b, dims=NN):
    return _dot(a.astype(BF16), b.astype(BF16), dims)


def _split2(a):
    hi = a.astype(BF16)
    return hi, (a - hi.astype(F32)).astype(BF16)


def _dot_exact_rhs(a, e):
    hi, lo = _split2(a)
    return _dot(hi, e) + _dot(lo, e)


def _dot_exact_lhs(e, a):
    hi, lo = _split2(a)
    return _dot(e, hi) + _dot(e, lo)


def _params(sem, vmem=None):
    return pltpu.CompilerParams(dimension_semantics=sem, vmem_limit_bytes=vmem)


def _sigmoid(x):
    return 0.5 * jnp.tanh(0.5 * x) + 0.5


def _mm_kernel(*refs, nk, epilogue, scaled, normed):
    refs = list(refs)
    a_ref, b_ref = refs.pop(0), refs.pop(0)
    ssq_ref = refs.pop(0) if scaled else None
    res_ref = refs.pop(0) if epilogue == "residual" else None
    gain_ref = refs.pop(0) if normed else None
    o_ref = refs.pop(0)
    dot = lambda: _dot(a_ref[...], b_ref[0].astype(BF16))
    j = pl.program_id(1)

    def emit_normed(x_new):
        xg_ref, ssq_out = refs
        xg_ref[...] = (x_new * gain_ref[...]).astype(xg_ref.dtype)
        row = jnp.broadcast_to(jnp.sum(x_new * x_new, axis=-1, keepdims=True), ssq_out.shape)

        @pl.when(j == 0)
        def _():
            ssq_out[...] = row

        @pl.when(j > 0)
        def _():
            ssq_out[...] += row

    if nk == 1:
        part = dot()
        if scaled:
            part = part * lax.rsqrt(ssq_ref[:, 0:1] * (1.0 / D_MODEL) + NORM_EPS)
        if epilogue == "relu2":
            part = jnp.square(jnp.maximum(part, 0.0))
        elif epilogue == "residual":
            part = part + res_ref[...]
        o_ref[...] = part.astype(o_ref.dtype)
        if normed:
            emit_normed(part)
    else:
        assert epilogue == "residual" and not scaled
        k = pl.program_id(2)

        @pl.when(k == 0)
        def _():
            o_ref[...] = res_ref[...] + dot()

        @pl.when((k > 0) & (k < nk - 1) if normed else k > 0)
        def _():
            o_ref[...] = o_ref[...] + dot()

        if normed:
            @pl.when(k == nk - 1)
            def _():
                x_new = o_ref[...] + dot()
                o_ref[...] = x_new
                emit_normed(x_new)


def _matmul(a, b, layer, *, col_off=0, n_cols=None, tm, tn, tk, epilogue="none", res=None, out_dtype=F32,
            row_ssq=None, next_gain=None, name="matmul"):
    M, K = a.shape
    n_cols = b.shape[2] if n_cols is None else n_cols
    assert M % tm == 0 and n_cols % tn == 0 and K % tk == 0 and col_off % LANES == 0
    nk = K // tk
    assert nk == 1 or out_dtype == F32
    scaled, normed = row_ssq is not None, next_gain is not None
    in_specs = [pl.BlockSpec((tm, tk), lambda i, j, k: (i, k)),
                pl.BlockSpec((pl.Element(1), pl.Element(tk), pl.Element(tn)),
                             lambda i, j, k: (layer, pl.multiple_of(k * tk, tk),
                                              pl.multiple_of(col_off + j * tn, LANES)))]
    args = [a, b]
    tile = pl.BlockSpec((tm, tn), lambda i, j, k: (i, j))
    stat = pl.BlockSpec((tm, LANES), lambda i, j, k: (i, 0))
    if scaled:
        in_specs.append(stat)
        args.append(row_ssq)
    if epilogue == "residual":
        in_specs.append(tile)
        args.append(res)
    out_shape = [jax.ShapeDtypeStruct((M, n_cols), out_dtype)]
    out_specs = [tile]
    if normed:
        in_specs.append(pl.BlockSpec((1, tn), lambda i, j, k: (0, j)))
        args.append(next_gain.reshape(1, n_cols))
        out_shape += [jax.ShapeDtypeStruct((M, n_cols), BF16), jax.ShapeDtypeStruct((M, LANES), F32)]
        out_specs += [tile, stat]
    outs = pl.pallas_call(
        functools.partial(_mm_kernel, nk=nk, epilogue=epilogue, scaled=scaled, normed=normed),
        out_shape=out_shape,
        grid=(M // tm, n_cols // tn, nk),
        in_specs=in_specs,
        out_specs=out_specs,
        compiler_params=_params(("parallel", "arbitrary" if normed else "parallel", "arbitrary"), V7X_VMEM_LIMIT),
        name=name,
    )(*args)
    return outs if normed else outs[0]


def _norm_operand_kernel(x_ref, g_ref, xg_ref, ssq_ref):
    x = x_ref[...]
    xg_ref[...] = (x * g_ref[...]).astype(xg_ref.dtype)
    ssq_ref[...] = jnp.broadcast_to(jnp.sum(x * x, axis=-1, keepdims=True), ssq_ref.shape)


def _norm_operand(x, g, tm):
    M, D = x.shape
    return pl.pallas_call(
        _norm_operand_kernel,
        name="norm_operand",
        out_shape=[jax.ShapeDtypeStruct((M, D), BF16), jax.ShapeDtypeStruct((M, LANES), F32)],
        grid=(M // tm,),
        in_specs=[pl.BlockSpec((tm, D), lambda i: (i, 0)), pl.BlockSpec((1, D), lambda i: (0, 0))],
        out_specs=[pl.BlockSpec((tm, D), lambda i: (i, 0)), pl.BlockSpec((tm, LANES), lambda i: (i, 0))],
        compiler_params=_params(("parallel",), V7X_VMEM_LIMIT),
    )(x, g.reshape(1, D))


def _head_ones(width, head):
    r = lax.broadcasted_iota(jnp.int32, (width, width), 0) // head
    c = lax.broadcasted_iota(jnp.int32, (width, width), 1) // head
    return jnp.where(r == c, 1.0, 0.0).astype(BF16)


def _rwkv_prep_rows(p, carry_ref, mu_ref, w0_ref, wup_ref, a0_ref, aup_ref, gup_ref, kk_ref, ka_ref, pad_to):
    tm = p.shape[0]
    row = lax.broadcasted_iota(jnp.int32, p.shape, 0)
    p_prev = jnp.where(row == 0, carry_ref[...], pltpu.roll(p, 1, axis=0))
    carry_ref[...] = p[tm - 1:tm, :]
    ps = p + mu_ref[...] * (p_prev - p)

    W = RW_WIDTH
    r = ps[:, 0:W]
    k = ps[:, W:2 * W]
    v = ps[:, 2 * W:3 * W]
    wa = ps[:, 3 * W:3 * W + LANES]
    g_in = ps[:, 3 * W + LANES:3 * W + 2 * LANES]

    z = -(w0_ref[...] + _dot(jnp.tanh(wa).astype(BF16), wup_ref[...].astype(BF16)))
    softplus = jnp.maximum(z, 0.0) + jnp.log1p(jnp.exp(-jnp.abs(z)))
    lw = -jnp.exp(-softplus - 0.5)
    a = _sigmoid(a0_ref[...] + _dot(wa.astype(BF16), aup_ref[...].astype(BF16)))
    g = _dot(_sigmoid(g_in).astype(BF16), gup_ref[...].astype(BF16))

    kk = k * kk_ref[...]
    ones = _head_ones(GW, RW_HEAD)
    kmod = k * (1.0 + (a - 1.0) * ka_ref[...])
    kn = []
    for q in range(NG):
        kq = kk[:, q * GW:(q + 1) * GW]
        ssq = _dot_exact_rhs(kq * kq, ones)
        kn.append(kq * lax.rsqrt(jnp.maximum(ssq, 1e-24)))
    kn = jnp.concatenate(kn, axis=1)
    vals = (r, lw, kmod, v, kn, kn * a, g)
    if pad_to > tm:
        zeros = jnp.zeros((pad_to - tm, RW_WIDTH), F32)
        vals = tuple(jnp.concatenate([x, zeros], axis=0) for x in vals)
    return vals


def _stack_heads(x, lane_head):
    return jnp.concatenate([jnp.where(lane_head == h, x, 0.0) for h in range(HPG)], axis=0)


def _rwkv_mix_kernel(p_ref, prev_ref, mu_ref, w0_ref, wup_ref, a0_ref, aup_ref, gup_ref, kk_ref, ka_ref,
                     s0_ref, lnw_ref, lnb_ref, rk_ref, y_ref, st_ref, s_scr, carry_ref, *, C):
    j = pl.program_id(1)

    @pl.when(j == 0)
    def _():
        s_scr[...] = s0_ref[0]
        carry_ref[...] = prev_ref[0]

    nc = max(p_ref.shape[0] // C, 1)
    prep = _rwkv_prep_rows(p_ref[...], carry_ref, mu_ref, w0_ref, wup_ref, a0_ref, aup_ref, gup_ref, kk_ref, ka_ref,
                           nc * C)

    R = HPG * C
    G = range(NG)
    lane_head = lax.broadcasted_iota(jnp.int32, (1, GW), 1) // RW_HEAD
    ii = lax.broadcasted_iota(jnp.int32, (R, R), 0)
    jj = lax.broadcasted_iota(jnp.int32, (R, R), 1)
    ti = lax.broadcasted_iota(jnp.int32, (C, C), 0)
    tj = lax.broadcasted_iota(jnp.int32, (C, C), 1)
    tril = jnp.where(ti >= tj, 1.0, 0.0).astype(BF16)
    ones = _head_ones(GW, RW_HEAD)
    strict = ii > jj
    incl = ii >= jj
    eye = jnp.where(ii == jj, 1.0, 0.0)
    sls = [slice(q * GW, (q + 1) * GW) for q in G]

    def state_independent(c):
        r_all, lw_all, k_all, v_all, kk_all, b_all, g_all = (x[c * C:(c + 1) * C] for x in prep)
        cum_all = _dot_exact_lhs(tril, lw_all)
        pc_all = jnp.exp(cum_all)
        pinv_all = jnp.exp(-cum_all)
        at_all = -kk_all * jnp.exp(cum_all - lw_all)
        bt_all = b_all * pinv_all
        kt_all = k_all * pinv_all
        rt_all = r_all * pc_all
        ar = [jnp.concatenate([_stack_heads(at_all[:, sl], lane_head),
                               _stack_heads(rt_all[:, sl], lane_head)], axis=0).astype(BF16) for sl in sls]
        bk = [jnp.concatenate([_stack_heads(bt_all[:, sl], lane_head),
                               _stack_heads(kt_all[:, sl], lane_head)], axis=0) for sl in sls]
        gram = [_dot(ar[q], bk[q].astype(BF16), NT) for q in G]
        a_ab = [jnp.where(strict, gram[q][:R, :R], 0.0) for q in G]
        a_kr = [jnp.concatenate([jnp.where(strict, gram[q][:R, R:], 0.0),
                                 jnp.where(incl, gram[q][R:, R:], 0.0)], axis=0).astype(BF16) for q in G]
        a_rb = [jnp.where(incl, gram[q][R:, :R], 0.0).astype(BF16) for q in G]
        t = [eye + jnp.where((ii >> 1) == (jj >> 1), a_ab[q], 0.0) for q in G]
        size = 2
        while size < C:
            sh = size.bit_length() - 1
            sel = ((ii >> (sh + 1)) == (jj >> (sh + 1))) & ((ii >> sh) != (jj >> sh))
            tb = [t[q].astype(BF16) for q in G]
            mid = [_dot(jnp.where(sel, a_ab[q], 0.0).astype(BF16), tb[q]) for q in G]
            t = [t[q] + _dot(tb[q], mid[q].astype(BF16)) for q in G]
            size *= 2
        return dict(r=r_all, k=k_all, v=v_all, g=g_all, pend=[pc_all[C - 1:C, sl] for sl in sls],
                    ar=ar, bk=bk, a_kr=a_kr, a_rb=a_rb, t=t)

    def state_dependent(c, d, s):
        v_st = [_stack_heads(d["v"][:, sl], lane_head) for sl in sls]
        from_state = [_dot(d["ar"][q], s[q].astype(BF16), NT) for q in G]
        from_v = [_dot(d["a_kr"][q], v_st[q].astype(BF16)) for q in G]
        u = [_dot1(d["t"][q], from_state[q][:R] + from_v[q][:R]) for q in G]
        y_st = [from_state[q][R:] + from_v[q][R:] + _dot(d["a_rb"][q], u[q].astype(BF16)) for q in G]
        s_new = [s[q] * d["pend"][q]
                 + _dot1(jnp.concatenate([u[q], v_st[q]], axis=0).T, d["bk"][q] * d["pend"][q]) for q in G]
        for q in G:
            sl = sls[q]
            y = y_st[q][0:C]
            for h in range(1, HPG):
                y = y + y_st[q][h * C:(h + 1) * C]
            mean = _dot_exact_rhs(y, ones) * (1.0 / RW_HEAD)
            yc = y - mean
            var = _dot_exact_rhs(yc * yc, ones) * (1.0 / RW_HEAD)
            yn = yc * lax.rsqrt(var + GN_EPS) * lnw_ref[:, sl] + lnb_ref[:, sl]
            bonus = _dot_exact_rhs(d["r"][:, sl] * d["k"][:, sl] * rk_ref[:, sl], ones) * d["v"][:, sl]
            y_ref[c * C:(c + 1) * C, sl] = ((yn + bonus) * d["g"][:, sl]).astype(y_ref.dtype)
        return s_new

    parts = [state_independent(c) for c in range(nc)]
    s = [s_scr[q] for q in G]
    for c in range(nc):
        s = state_dependent(c, parts[c], s)
    for q in G:
        s_scr[q] = s[q]

    @pl.when(j == pl.num_programs(1) - 1)
    def _():
        st_ref[0] = s_scr[...]


def _rwkv_mix(p_rw, row_off, B, T, tm, C, prev, s0, mu, w0, wup_pad, a0, aup_pad, gup, k_k, k_a, ln_w, ln_b, r_k):
    nt = T // tm
    off = row_off // tm
    t_out = max(tm, C)
    assert row_off % tm == 0 and T % tm == 0 and (tm <= C or tm % C == 0)
    row = lambda n: pl.BlockSpec((1, n), lambda b, j: (0, 0))
    full = lambda a: pl.BlockSpec(a.shape, lambda b, j: (0, 0))
    st_spec = pl.BlockSpec((1, NG, GW, GW), lambda b, j: (b, 0, 0, 0))
    return pl.pallas_call(
        functools.partial(_rwkv_mix_kernel, C=C),
        name="rwkv_mix",
        out_shape=[jax.ShapeDtypeStruct((B * nt * t_out, RW_WIDTH), BF16),
                   jax.ShapeDtypeStruct((B, NG, GW, GW), F32)],
        grid=(B, nt),
        in_specs=[pl.BlockSpec((tm, RW_COLS), lambda b, j: (off + b * nt + j, 0)),
                  pl.BlockSpec((1, 1, RW_COLS), lambda b, j: (b, 0, 0)),
                  row(RW_COLS), row(RW_WIDTH), full(wup_pad), row(RW_WIDTH), full(aup_pad), full(gup),
                  row(RW_WIDTH), row(RW_WIDTH), st_spec, row(RW_WIDTH), row(RW_WIDTH), row(RW_WIDTH)],
        out_specs=[pl.BlockSpec((t_out, RW_WIDTH), lambda b, j: (b * nt + j, 0)), st_spec],
        scratch_shapes=[pltpu.VMEM((NG, GW, GW), F32), pltpu.VMEM((1, RW_COLS), F32)],
        compiler_params=_params(("parallel", "arbitrary"), V7X_VMEM_LIMIT),
    )(p_rw, prev.reshape(B, 1, RW_COLS), mu.reshape(1, -1), w0.reshape(1, -1), wup_pad, a0.reshape(1, -1),
      aup_pad, gup, k_k.reshape(1, -1), k_a.reshape(1, -1), s0, ln_w.reshape(1, -1), ln_b.reshape(1, -1),
      r_k.reshape(1, -1))


def _state_to_blockdiag(s):
    B = s.shape[0]
    s = s.reshape(B, NG, HPG, RW_HEAD, RW_HEAD)
    eye = jnp.eye(HPG, dtype=s.dtype)
    bd = s[:, :, :, :, None, :] * eye[None, None, :, None, :, None]
    return bd.reshape(B, NG, GW, GW)


def _blockdiag_to_state(bd):
    B = bd.shape[0]
    x = bd.reshape(B, NG, HPG, RW_HEAD, HPG, RW_HEAD)
    return jnp.stack([x[:, :, h, :, h, :] for h in range(HPG)], axis=2).reshape(B, RW_HEADS, RW_HEAD, RW_HEAD)


def _gelu(x):
    return 0.5 * x * (1.0 + lax.erf(x * (2.0 ** -0.5)))


def _gmlp_kernel(u_ref, v_ref, lng_ref, lnb_ref, ws_ref, bs_ref, y_ref, vn_ref, *, tm):
    u = _gelu(u_ref[...])
    vf = _gelu(v_ref[...])
    mean = jnp.mean(vf, axis=-1, keepdims=True)
    vc = vf - mean
    var = jnp.mean(vc * vc, axis=-1, keepdims=True)
    vn = vc * lax.rsqrt(var + LN_EPS) * lng_ref[...] + lnb_ref[...]
    vn_ref[...] = vn
    ti = lax.broadcasted_iota(jnp.int32, (GM_CHUNK, GM_CHUNK), 0)
    tj = lax.broadcasted_iota(jnp.int32, (GM_CHUNK, GM_CHUNK), 1)
    causal = ti >= tj
    rows = min(tm, GM_CHUNK)
    for g in range(GM_GROUPS):
        wm = jnp.where(causal, ws_ref[g], 0.0).astype(BF16)
        sl = slice(g * GM_GROUP_DIM, (g + 1) * GM_GROUP_DIM)
        for c in range(max(tm // GM_CHUNK, 1)):
            rs = slice(c * GM_CHUNK, c * GM_CHUNK + rows)
            vg = vn[rs, sl].astype(BF16)
            if rows < GM_CHUNK:
                vg = jnp.concatenate([vg, jnp.zeros((GM_CHUNK - rows, GM_GROUP_DIM), BF16)], axis=0)
            mixed = (_dot(wm, vg) + bs_ref[g])[:rows]
            y_ref[rs, sl] = (u[rs, sl] * mixed).astype(y_ref.dtype)


def _gmlp(p_gm, row_off, rows, tm, ln_g, ln_b, ws, layer, bs_b, y_dtype):
    off = row_off // tm
    assert row_off % tm == 0 and rows % tm == 0
    row = pl.BlockSpec((1, GM_WIDTH), lambda i: (0, 0))
    ws_spec = pl.BlockSpec((None, GM_GROUPS, GM_CHUNK, GM_CHUNK), lambda i: (layer, 0, 0, 0))
    return pl.pallas_call(
        functools.partial(_gmlp_kernel, tm=tm),
        name="gmlp",
        out_shape=[jax.ShapeDtypeStruct((rows, GM_WIDTH), y_dtype), jax.ShapeDtypeStruct((rows, GM_WIDTH), F32)],
        grid=(rows // tm,),
        in_specs=[pl.BlockSpec((tm, GM_WIDTH), lambda i: (off + i, 0)),
                  pl.BlockSpec((tm, GM_WIDTH), lambda i: (off + i, 1)),
                  row, row, ws_spec, ws_spec],
        out_specs=[pl.BlockSpec((tm, GM_WIDTH), lambda i: (i, 0))] * 2,
        compiler_params=_params(("parallel",), V7X_VMEM_LIMIT),
    )(p_gm, p_gm, ln_g.reshape(1, -1), ln_b.reshape(1, -1), ws, bs_b)


def _attn_prep_kernel(q_ref, k_ref, cos_ref, sin_ref, qg_ref, kg_ref, qo_ref, ko_ref):
    cos = cos_ref[...]
    sin = sin_ref[...]

    def norm_rope(x, gain):
        y = x * lax.rsqrt(jnp.mean(x * x, axis=-1, keepdims=True) + NORM_EPS) * gain
        return y * cos + pltpu.roll(y, ATT_HEAD // 2, axis=1) * sin

    for h in range(N_DIL * ATT_GROUP_HEADS):
        sl = slice(h * ATT_HEAD, (h + 1) * ATT_HEAD)
        qo_ref[:, sl] = norm_rope(q_ref[:, sl], qg_ref[...]).astype(qo_ref.dtype)
        ko_ref[:, sl] = norm_rope(k_ref[:, sl], kg_ref[...]).astype(ko_ref.dtype)


def _attn_prep(p_at, cos, sin, q_gain, k_gain, tm):
    M = p_at.shape[0]
    blk = lambda c: pl.BlockSpec((tm, ATT_WIDTH), lambda i: (i, c))
    tab = pl.BlockSpec((tm, ATT_HEAD), lambda i: (i, 0))
    gain = pl.BlockSpec((1, ATT_HEAD), lambda i: (0, 0))
    return pl.pallas_call(
        _attn_prep_kernel,
        name="attn_prep",
        out_shape=[jax.ShapeDtypeStruct((M, ATT_WIDTH), F32), jax.ShapeDtypeStruct((M, ATT_WIDTH), F32)],
        grid=(M // tm,),
        in_specs=[blk(0), blk(1), tab, tab, gain, gain],
        out_specs=[pl.BlockSpec((tm, ATT_WIDTH), lambda i: (i, 0))] * 2,
        compiler_params=_params(("parallel",), V7X_VMEM_LIMIT),
    )(p_at, p_at, cos, sin, q_gain.reshape(1, -1), k_gain.reshape(1, -1))


def _attn_prompt_kernel(q0, q1, q2, k0, k1, k2, v0, v1, v2, y_ref, o_scr, lse_scr, *, T):
    scale = ATT_HEAD ** -0.5
    qi = lax.broadcasted_iota(jnp.int32, (ATT_BLK, ATT_BLK), 0)
    kj = lax.broadcasted_iota(jnp.int32, (ATT_BLK, ATT_BLK), 1)
    refs = ((q0, k0, v0), (q1, k1, v1), (q2, k2, v2))
    for g, (window, dil) in enumerate(DIL_PAIRS):
        q_ref, k_ref, v_ref = refs[g]
        nb = T // dil // ATT_BLK

        def rows(r, n, dil=dil):
            if dil == 1:
                return pl.ds(n * ATT_BLK, ATT_BLK)
            return pl.ds(r + n * ATT_BLK * dil, ATT_BLK, stride=dil)

        blocks = [(rows(r, n), rows(r, n - 1) if n > 0 else None) for r in range(dil) for n in range(nb)]
        for b0 in range(0, len(blocks), ATTN_BATCH):
            batch = blocks[b0:b0 + ATTN_BATCH]
            q = [q_ref[rs, :].astype(BF16) for rs, _ in batch]
            s_c = [jnp.where(kj <= qi, _dot(q[i], k_ref[rs, :].astype(BF16), NT) * scale, NEG_INF)
                   for i, (rs, _) in enumerate(batch)]
            s_p = [None if ps_ is None else
                   jnp.where(kj >= qi, _dot(q[i], k_ref[ps_, :].astype(BF16), NT) * scale, NEG_INF)
                   for i, (_, ps_) in enumerate(batch)]
            m = [jnp.max(s, axis=-1, keepdims=True) for s in s_c]
            m = [mc if sp is None else jnp.maximum(mc, jnp.max(sp, axis=-1, keepdims=True))
                 for mc, sp in zip(m, s_p)]
            e_c = [jnp.exp(s - mm) for s, mm in zip(s_c, m)]
            e_p = [None if sp is None else jnp.exp(sp - mm) for sp, mm in zip(s_p, m)]
            den = [jnp.sum(e, axis=-1, keepdims=True) for e in e_c]
            den = [d if e is None else d + jnp.sum(e, axis=-1, keepdims=True) for d, e in zip(den, e_p)]
            acc = [_dot(e.astype(BF16), v_ref[rs, :].astype(BF16)) for e, (rs, _) in zip(e_c, batch)]
            acc = [a if e is None else a + _dot(e.astype(BF16), v_ref[ps_, :].astype(BF16))
                   for a, e, (_, ps_) in zip(acc, e_p, batch)]
            for i, (rs, _) in enumerate(batch):
                o_scr[g, rs, :] = acc[i] / den[i]
                lse_scr[g, rs, :] = jnp.broadcast_to(m[i] + jnp.log(den[i]), (ATT_BLK, ATT_HEAD))
    a, b, c = lse_scr[0], lse_scr[1], lse_scr[2]
    m = jnp.maximum(jnp.maximum(a, b), c)
    wa, wb, wc = jnp.exp(a - m), jnp.exp(b - m), jnp.exp(c - m)
    y = (wa * o_scr[0] + wb * o_scr[1] + wc * o_scr[2]) / (wa + wb + wc)
    y_ref[...] = y.astype(y_ref.dtype)


def _attn_prompt(q_rot, k_rot, p_at, B, T):
    nh = ATT_GROUP_HEADS
    col = lambda g, base: pl.BlockSpec((T, ATT_HEAD), lambda b, h: (b, base + g * nh + h))
    vbase = 2 * ATT_WIDTH // ATT_HEAD
    return pl.pallas_call(
        functools.partial(_attn_prompt_kernel, T=T),
        name="attn_prompt",
        out_shape=jax.ShapeDtypeStruct((B * T, ATT_OUT), BF16),
        grid=(B, nh),
        in_specs=[col(g, 0) for g in range(N_DIL)] * 2 + [col(g, vbase) for g in range(N_DIL)],
        out_specs=pl.BlockSpec((T, ATT_HEAD), lambda b, h: (b, h)),
        scratch_shapes=[pltpu.VMEM((N_DIL, T, ATT_HEAD), F32)] * 2,
        compiler_params=_params(("parallel", "parallel"), V7X_VMEM_LIMIT),
    )(q_rot, q_rot, q_rot, k_rot, k_rot, k_rot, p_at, p_at, p_at)


def _attn_sample_kernel(q_ref, k_ref, v_ref, c0_ref, c1_ref, c2_ref, y_ref, *, T):
    scale = ATT_HEAD ** -0.5
    caches = (c0_ref, c1_ref, c2_ref)
    zpad = jnp.zeros((LANES - T, ATT_HEAD), F32)
    outs = [[None] * N_DIL for _ in range(ATT_GROUP_HEADS)]
    lses = [[None] * N_DIL for _ in range(ATT_GROUP_HEADS)]
    for g, (window, dil) in enumerate(DIL_PAIRS):
        cref = caches[g]
        wb = cref.shape[1]
        nkeys = wb + LANES
        t = lax.broadcasted_iota(jnp.int32, (T, nkeys), 0)
        jrow = lax.broadcasted_iota(jnp.int32, (T, nkeys), 1)
        dist = wb + t - jrow
        valid = (dist >= 0) & (dist <= window) & ((dist & (dil - 1)) == 0)
        for h in range(ATT_GROUP_HEADS):
            sl = slice((g * ATT_GROUP_HEADS + h) * ATT_HEAD, (g * ATT_GROUP_HEADS + h + 1) * ATT_HEAD)
            q = q_ref[:, sl].astype(BF16)
            kcat = jnp.concatenate([cref[0, :, 0, h, :], k_ref[:, sl], zpad], axis=0).astype(BF16)
            vcat = jnp.concatenate([cref[0, :, 1, h, :], v_ref[:, sl], zpad], axis=0).astype(BF16)
            s = jnp.where(valid, _dot(q, kcat, NT) * scale, NEG_INF)
            m = jnp.max(s, axis=-1, keepdims=True)
            lse = m + jnp.log(jnp.sum(jnp.exp(s - m), axis=-1, keepdims=True))
            outs[h][g] = _dot(jnp.exp(s - lse).astype(BF16), vcat)
            lses[h][g] = lse
    for h in range(ATT_GROUP_HEADS):
        m = jnp.maximum(jnp.maximum(lses[h][0], lses[h][1]), lses[h][2])
        w = [jnp.exp(l - m) for l in lses[h]]
        y = (w[0] * outs[h][0] + w[1] * outs[h][1] + w[2] * outs[h][2]) / (w[0] + w[1] + w[2])
        y_ref[:, h * ATT_HEAD:(h + 1) * ATT_HEAD] = y.astype(y_ref.dtype)


def _attn_sample(q_rot, k_rot, p_at, row_off, B, T, caches, layer):
    off = row_off // T
    blk = lambda c: pl.BlockSpec((T, ATT_WIDTH), lambda b: (off + b, c))
    cspec = lambda a: pl.BlockSpec((None, 1) + a.shape[2:], lambda b: (layer, b, 0, 0, 0, 0))
    return pl.pallas_call(
        functools.partial(_attn_sample_kernel, T=T),
        name="attn_sample",
        out_shape=jax.ShapeDtypeStruct((B * T, ATT_OUT), F32),
        grid=(B,),
        in_specs=[blk(0), blk(0), blk(2)] + [cspec(c) for c in caches],
        out_specs=pl.BlockSpec((T, ATT_OUT), lambda b: (b, 0)),
        compiler_params=_params(("parallel",), V7X_VMEM_LIMIT),
    )(q_rot, k_rot, p_at, *caches)


def _branch_kernel(yr_ref, yg_ref, ya_ref, wr_ref, wg_ref, wa_ref, g0_ref, g1_ref, g2_ref, o_ref):
    acc = _sigmoid(g0_ref[...]) * _dot(yr_ref[...], wr_ref[...].astype(BF16))
    acc = acc + _sigmoid(g1_ref[...]) * _dot(yg_ref[...], wg_ref[...].astype(BF16))
    acc = acc + _sigmoid(g2_ref[...]) * _dot(ya_ref[...], wa_ref[...].astype(BF16))
    o_ref[...] = acc.astype(o_ref.dtype)


def _branch(y_rw, y_gm, y_at, w_rw, w_gm, w_at, layer, p_gate, tm, tn):
    M = y_rw.shape[0]
    nb = D_MODEL // tn
    lhs = lambda kdim: pl.BlockSpec((tm, kdim), lambda i, j: (i, 0))
    rhs = lambda kdim: pl.BlockSpec((None, kdim, tn), lambda i, j: (layer, 0, j))
    gate = lambda br: pl.BlockSpec((tm, tn), lambda i, j: (i, br * nb + j))
    return pl.pallas_call(
        _branch_kernel,
        name="branch_merge",
        out_shape=jax.ShapeDtypeStruct((M, D_MODEL), BF16),
        grid=(M // tm, nb),
        in_specs=[lhs(RW_WIDTH), lhs(GM_WIDTH), lhs(ATT_OUT), rhs(RW_WIDTH), rhs(GM_WIDTH), rhs(ATT_OUT),
                  gate(0), gate(1), gate(2)],
        out_specs=pl.BlockSpec((tm, tn), lambda i, j: (i, j)),
        compiler_params=_params(("parallel", "parallel"), V7X_VMEM_LIMIT),
    )(y_rw, y_gm, y_at, w_rw, w_gm, w_at, p_gate, p_gate, p_gate)


def _rope_tables(pos):
    half = ATT_HEAD // 2
    inv = ROPE_THETA ** (-jnp.arange(half, dtype=F32) / half)
    ang = pos.astype(F32)[:, None] * inv[None, :]
    cos, sin = jnp.cos(ang), jnp.sin(ang)
    return jnp.concatenate([cos, cos], -1), jnp.concatenate([-sin, sin], -1)


def _kv_rows_kernel(*refs):
    k_ref, v_ref, o_ref = refs[0], refs[1], refs[-1]
    for h in range(ATT_GROUP_HEADS):
        sl = slice(h * ATT_HEAD, (h + 1) * ATT_HEAD)
        o_ref[0, :, 0, h, :] = k_ref[:, sl]
        o_ref[0, :, 1, h, :] = v_ref[:, sl]


def _kv_rows(k_rot, p_at, buf, layer, depth, row0, B, T, keep, g):
    tk = min(keep, ATT_BLK)
    first = (row0 + T - keep) // tk
    assert (row0 + T - keep) % tk == 0 and T % tk == 0
    rows = lambda c: pl.BlockSpec((tk, ATT_OUT), lambda b, i: (first + b * (T // tk) + i, c))
    shape = (depth, B, keep, 2, ATT_GROUP_HEADS, ATT_HEAD)
    in_specs = [rows(g), rows(2 * ATT_WIDTH // ATT_OUT + g)]
    args = [k_rot, p_at]
    if buf is not None:
        in_specs.append(pl.BlockSpec(memory_space=pl.ANY))
        args.append(buf)
    return pl.pallas_call(
        _kv_rows_kernel,
        name="kv_rows",
        out_shape=jax.ShapeDtypeStruct(shape, F32),
        grid=(B, keep // tk),
        in_specs=in_specs,
        out_specs=pl.BlockSpec((None, 1, tk, 2, ATT_GROUP_HEADS, ATT_HEAD), lambda b, i: (layer, b, i, 0, 0, 0)),
        input_output_aliases={} if buf is None else {2: 0},
        compiler_params=_params(("parallel", "parallel"), V7X_VMEM_LIMIT),
    )(*args)


def kernel(x_prompt, x_sample, cache_kv_w128, cache_kv_w512, cache_kv_w2048, state_rwkv, state_rwkv_shift, norm1, w_in, rw_mu, rw_w0, rw_w_up, rw_a0, rw_a_up, rw_g_up, rw_k_k, rw_k_a, rw_r_k, rw_ln_w, rw_ln_b, gm_ln_g, gm_ln_b, gm_ws, gm_bs, att_q_gain, att_k_gain, w_br_rwkv, w_br_gmlp, w_br_attn, w_out, norm2, w_ff1, w_ff2):
    BP, TP, _ = x_prompt.shape
    BS, TS, _ = x_sample.shape
    depth = w_in.shape[0]
    MP, MS = BP * TP, BS * TS
    M = MP + MS
    TM = ROW_TILE
    assert M % (2 * TM) == 0 and MP % 1024 == 0 and TP % SCAN_C == 0 and TS <= SCAN_C
    x = jnp.concatenate([x_prompt.reshape(MP, D_MODEL), x_sample.reshape(MS, D_MODEL)], axis=0)

    pos = jnp.concatenate([jnp.tile(jnp.arange(TP), BP), jnp.tile(PAST_LEN + jnp.arange(TS), BS)])
    cos_t, sin_t = _rope_tables(pos)
    caches_all = [cache_kv_w128, cache_kv_w512, cache_kv_w2048]
    zeros_state = jnp.zeros((BP, NG, GW, GW), F32)
    zeros_shift = jnp.zeros((BP, RW_COLS), F32)
    bs_b = jnp.broadcast_to(gm_bs[:, :, :, None], (depth, GM_GROUPS, GM_CHUNK, GM_GROUP_DIM))
    c_rw, c_gm, c_at = RW_COLS, RW_COLS + 2 * GM_WIDTH, RW_COLS + 2 * GM_WIDTH + 3 * ATT_WIDTH

    kvp = [None] * N_DIL
    kvs = [None] * N_DIL
    st_p, st_s, sh_p, sh_s, gmv_s = [], [], [], [], []
    for l in range(depth):
        if l == 0:
            xg, ssq = _norm_operand(x, norm1[0], TM)
        mm_in = functools.partial(_matmul, xg, w_in, l, tm=TM * 2, tk=D_MODEL, row_ssq=ssq)
        p_rw = mm_in(col_off=0, n_cols=RW_COLS, tn=256, name="proj_rwkv")
        p_gm = mm_in(col_off=c_rw, n_cols=2 * GM_WIDTH, tn=512, name="proj_gmlp")
        p_at = mm_in(col_off=c_gm, n_cols=3 * ATT_WIDTH, tn=512, name="proj_attn")
        p_gate = mm_in(col_off=c_at, n_cols=N_BRANCH * D_MODEL, tn=512, name="proj_gates")

        zpad = jnp.zeros((DECAY_LORA, RW_WIDTH), F32)
        wup_pad = jnp.concatenate([rw_w_up[l], zpad], axis=0)
        aup_pad = jnp.concatenate([zpad, rw_a_up[l]], axis=0)
        mix = functools.partial(_rwkv_mix, p_rw, C=SCAN_C, mu=rw_mu[l], w0=rw_w0[l], wup_pad=wup_pad, a0=rw_a0[l],
                                aup_pad=aup_pad, gup=rw_g_up[l], k_k=rw_k_k[l].reshape(-1), k_a=rw_k_a[l].reshape(-1),
                                ln_w=rw_ln_w[l].reshape(-1), ln_b=rw_ln_b[l].reshape(-1), r_k=rw_r_k[l].reshape(-1))
        y_rw_p, sT_p = mix(row_off=0, B=BP, T=TP, tm=2 * SCAN_C, prev=zeros_shift, s0=zeros_state)
        y_rw_s, sT_s = mix(row_off=MP, B=BS, T=TS, tm=TS, prev=state_rwkv_shift[l],
                           s0=_state_to_blockdiag(state_rwkv[l]))
        y_rw_s = y_rw_s.reshape(BS, SCAN_C, RW_WIDTH)[:, :TS].reshape(MS, RW_WIDTH)
        y_rw = jnp.concatenate([y_rw_p, y_rw_s], axis=0)
        st_p.append(_blockdiag_to_state(sT_p))
        st_s.append(_blockdiag_to_state(sT_s))
        sh_p.append(p_rw[TP - 1:MP:TP])
        sh_s.append(p_rw[MP + TS - 1::TS])

        y_gm_p, _ = _gmlp(p_gm, 0, MP, 256, gm_ln_g[l], gm_ln_b[l], gm_ws, l, bs_b, BF16)
        y_gm_s, vn_s = _gmlp(p_gm, MP, MS, TS, gm_ln_g[l], gm_ln_b[l], gm_ws, l, bs_b, F32)
        y_gm = jnp.concatenate([y_gm_p, y_gm_s.astype(BF16)], axis=0)
        gmv_s.append(vn_s.reshape(BS, TS, GM_WIDTH))

        q_rot, k_rot = _attn_prep(p_at, cos_t, sin_t, att_q_gain[l], att_k_gain[l], TM)
        for g, (window, dil) in enumerate(DIL_PAIRS):
            keep = min(window, TP)
            kvp[g] = _kv_rows(k_rot, p_at, kvp[g], l, depth, 0, BP, TP, keep, g)
            kvs[g] = _kv_rows(k_rot, p_at, kvs[g], l, depth, MP, BS, TS, TS, g)
        y_at_p = _attn_prompt(q_rot, k_rot, p_at, BP, TP)
        y_at_s = _attn_sample(q_rot, k_rot, p_at, MP, BS, TS, caches_all, l)
        y_at = jnp.concatenate([y_at_p, y_at_s.astype(BF16)], axis=0)

        merged = _branch(y_rw, y_gm, y_at, w_br_rwkv, w_br_gmlp, w_br_attn, l, p_gate, TM * 2, 256)
        x, xg2, ssq2 = _matmul(merged, w_out, l, tm=TM * 2, tn=256, tk=D_MODEL, epilogue="residual", res=x,
                               next_gain=norm2[l], name="out_proj")
        act = _matmul(xg2, w_ff1, l, tm=TM * 2, tn=512, tk=D_MODEL, epilogue="relu2", out_dtype=BF16,
                      row_ssq=ssq2, name="ffn_up")
        ffn_down = functools.partial(_matmul, act, w_ff2, l, tm=TM * 2, tn=1024, tk=1024, epilogue="residual",
                                     res=x, name="ffn_down")
        if l + 1 < depth:
            x, xg, ssq = ffn_down(next_gain=norm1[l + 1])
        else:
            x = ffn_down()

    return (x[:MP].reshape(BP, TP, D_MODEL), x[MP:].reshape(BS, TS, D_MODEL),
            kvp[0], kvp[1], kvp[2], kvs[0], kvs[1], kvs[2],
            jnp.stack(st_p, 0), jnp.stack(st_s, 0), jnp.stack(sh_p, 0), jnp.stack(sh_s, 0),
            jnp.stack(gmv_s, 0))
```

```python
import functools

import jax
import jax.numpy as jnp
from jax import lax
from jax.experimental import pallas as pl
from jax.experimental.pallas import tpu as pltpu

F32 = jnp.float32
BF16 = jnp.bfloat16

LANES = 128
V7X_VMEM_LIMIT = 56 * 1024 * 1024

D_MODEL = 4096
RW_HEADS = 24
RW_HEAD = 64
RW_WIDTH = RW_HEADS * RW_HEAD
DECAY_LORA = 64
AAA_LORA = 64
GATE_LORA = 128
RW_COLS = 3 * RW_WIDTH + DECAY_LORA + AAA_LORA + GATE_LORA
GN_EPS = 64e-5
GM_CHUNK = 128
GM_GROUPS = 12
GM_GROUP_DIM = 128
GM_WIDTH = GM_GROUPS * GM_GROUP_DIM
LN_EPS = 1e-5
DIL_PAIRS = ((128, 1), (512, 4), (2048, 16))
N_DIL = 3
ATT_GROUP_HEADS = 4
ATT_HEAD = 128
ATT_WIDTH = N_DIL * ATT_GROUP_HEADS * ATT_HEAD
ATT_OUT = ATT_GROUP_HEADS * ATT_HEAD
ATT_BLK = 128
ATTN_BATCH = 8
ROPE_THETA = 10000.0
N_BRANCH = 3
D_FF = 4 * D_MODEL
NORM_EPS = 1e-6
NEG_INF = -1e30
PAST_LEN = 8192

ROW_TILE = 688

HPG = 4
GW = HPG * RW_HEAD
NG = RW_HEADS // HPG
SCAN_C = 64
SAMPLE_C = 32

NN = (((1,), (0,)), ((), ()))
NT = (((1,), (1,)), ((), ()))


def _dot(a, b, dims=NN):
    return lax.dot_general(a, b, dims, preferred_element_type=F32)


def _dot1(a, b, dims=NN):
    return _dot(a.astype(BF16), b.astype(BF16), dims)


def _split2(a):
    hi = a.astype(BF16)
    return hi, (a - hi.astype(F32)).astype(BF16)


def _dot_exact_rhs(a, e):
    hi, lo = _split2(a)
    return _dot(hi, e) + _dot(lo, e)


def _dot_exact_lhs(e, a):
    hi, lo = _split2(a)
    return _dot(e, hi) + _dot(e, lo)


def _params(sem, vmem=None):
    return pltpu.CompilerParams(dimension_semantics=sem, vmem_limit_bytes=vmem)


def _sigmoid(x):
    return 0.5 * jnp.tanh(0.5 * x) + 0.5


def _mm_kernel(*refs, nk, epilogue, scaled, normed):
    refs = list(refs)
    a_ref, b_ref = refs.pop(0), refs.pop(0)
    ssq_ref = refs.pop(0) if scaled else None
    res_ref = refs.pop(0) if epilogue == "residual" else None
    gain_ref = refs.pop(0) if normed else None
    o_ref = refs.pop(0)
    dot = lambda: _dot(a_ref[...], b_ref[0].astype(BF16))
    j = pl.program_id(1)

    def emit_normed(x_new):
        xg_ref, ssq_out = refs
        xg_ref[...] = (x_new * gain_ref[...]).astype(xg_ref.dtype)
        row = jnp.broadcast_to(jnp.sum(x_new * x_new, axis=-1, keepdims=True), ssq_out.shape)

        @pl.when(j == 0)
        def _():
            ssq_out[...] = row

        @pl.when(j > 0)
        def _():
            ssq_out[...] += row

    if nk == 1:
        part = dot()
        if scaled:
            part = part * lax.rsqrt(ssq_ref[:, 0:1] * (1.0 / D_MODEL) + NORM_EPS)
        if epilogue == "relu2":
            part = jnp.square(jnp.maximum(part, 0.0))
        elif epilogue == "residual":
            part = part + res_ref[...]
        o_ref[...] = part.astype(o_ref.dtype)
        if normed:
            emit_normed(part)
    else:
        assert epilogue == "residual" and not scaled
        k = pl.program_id(2)

        @pl.when(k == 0)
        def _():
            o_ref[...] = res_ref[...] + dot()

        @pl.when((k > 0) & (k < nk - 1) if normed else k > 0)
        def _():
            o_ref[...] = o_ref[...] + dot()

        if normed:
            @pl.when(k == nk - 1)
            def _():
                x_new = o_ref[...] + dot()
                o_ref[...] = x_new
                emit_normed(x_new)


def _matmul(a, b, layer, *, col_off=0, n_cols=None, tm, tn, tk, epilogue="none", res=None, out_dtype=F32,
            row_ssq=None, next_gain=None, name="matmul"):
    M, K = a.shape
    n_cols = b.shape[2] if n_cols is None else n_cols
    assert M % tm == 0 and n_cols % tn == 0 and K % tk == 0 and col_off % LANES == 0
    nk = K // tk
    assert nk == 1 or out_dtype == F32
    scaled, normed = row_ssq is not None, next_gain is not None
    in_specs = [pl.BlockSpec((tm, tk), lambda i, j, k: (i, k)),
                pl.BlockSpec((pl.Element(1), pl.Element(tk), pl.Element(tn)),
                             lambda i, j, k: (layer, pl.multiple_of(k * tk, tk),
                                              pl.multiple_of(col_off + j * tn, LANES)))]
    args = [a, b]
    tile = pl.BlockSpec((tm, tn), lambda i, j, k: (i, j))
    stat = pl.BlockSpec((tm, LANES), lambda i, j, k: (i, 0))
    if scaled:
        in_specs.append(stat)
        args.append(row_ssq)
    if epilogue == "residual":
        in_specs.append(tile)
        args.append(res)
    out_shape = [jax.ShapeDtypeStruct((M, n_cols), out_dtype)]
    out_specs = [tile]
    if normed:
        in_specs.append(pl.BlockSpec((1, tn), lambda i, j, k: (0, j)))
        args.append(next_gain.reshape(1, n_cols))
        out_shape += [jax.ShapeDtypeStruct((M, n_cols), BF16), jax.ShapeDtypeStruct((M, LANES), F32)]
        out_specs += [tile, stat]
    outs = pl.pallas_call(
        functools.partial(_mm_kernel, nk=nk, epilogue=epilogue, scaled=scaled, normed=normed),
        out_shape=out_shape,
        grid=(M // tm, n_cols // tn, nk),
        in_specs=in_specs,
        out_specs=out_specs,
        compiler_params=_params(("parallel", "arbitrary" if normed else "parallel", "arbitrary"), V7X_VMEM_LIMIT),
        name=name,
    )(*args)
    return outs if normed else outs[0]


def _norm_operand_kernel(x_ref, g_ref, xg_ref, ssq_ref):
    x = x_ref[...]
    xg_ref[...] = (x * g_ref[...]).astype(xg_ref.dtype)
    ssq_ref[...] = jnp.broadcast_to(jnp.sum(x * x, axis=-1, keepdims=True), ssq_ref.shape)


def _norm_operand(x, g, tm):
    M, D = x.shape
    return pl.pallas_call(
        _norm_operand_kernel,
        name="norm_operand",
        out_shape=[jax.ShapeDtypeStruct((M, D), BF16), jax.ShapeDtypeStruct((M, LANES), F32)],
        grid=(M // tm,),
        in_specs=[pl.BlockSpec((tm, D), lambda i: (i, 0)), pl.BlockSpec((1, D), lambda i: (0, 0))],
        out_specs=[pl.BlockSpec((tm, D), lambda i: (i, 0)), pl.BlockSpec((tm, LANES), lambda i: (i, 0))],
        compiler_params=_params(("parallel",), V7X_VMEM_LIMIT),
    )(x, g.reshape(1, D))


def _head_ones(width, head):
    r = lax.broadcasted_iota(jnp.int32, (width, width), 0) // head
    c = lax.broadcasted_iota(jnp.int32, (width, width), 1) // head
    return jnp.where(r == c, 1.0, 0.0).astype(BF16)


def _rwkv_prep_rows(p, carry_ref, mu_ref, w0_ref, wup_ref, a0_ref, aup_ref, gup_ref, kk_ref, ka_ref, pad_to):
    tm = p.shape[0]
    row = lax.broadcasted_iota(jnp.int32, p.shape, 0)
    p_prev = jnp.where(row == 0, carry_ref[...], pltpu.roll(p, 1, axis=0))
    carry_ref[...] = p[tm - 1:tm, :]
    ps = p + mu_ref[...] * (p_prev - p)

    W = RW_WIDTH
    r = ps[:, 0:W]
    k = ps[:, W:2 * W]
    v = ps[:, 2 * W:3 * W]
    wa = ps[:, 3 * W:3 * W + LANES]
    g_in = ps[:, 3 * W + LANES:3 * W + 2 * LANES]

    z = -(w0_ref[...] + _dot(jnp.tanh(wa).astype(BF16), wup_ref[...].astype(BF16)))
    softplus = jnp.maximum(z, 0.0) + jnp.log1p(jnp.exp(-jnp.abs(z)))
    lw = -jnp.exp(-softplus - 0.5)
    a = _sigmoid(a0_ref[...] + _dot(wa.astype(BF16), aup_ref[...].astype(BF16)))
    g = _dot(_sigmoid(g_in).astype(BF16), gup_ref[...].astype(BF16))

    kk = k * kk_ref[...]
    ones = _head_ones(GW, RW_HEAD)
    kmod = k * (1.0 + (a - 1.0) * ka_ref[...])
    kn = []
    for q in range(NG):
        kq = kk[:, q * GW:(q + 1) * GW]
        ssq = _dot_exact_rhs(kq * kq, ones)
        kn.append(kq * lax.rsqrt(jnp.maximum(ssq, 1e-24)))
    kn = jnp.concatenate(kn, axis=1)
    vals = (r, lw, kmod, v, kn, kn * a, g)
    if pad_to > tm:
        zeros = jnp.zeros((pad_to - tm, RW_WIDTH), F32)
        vals = tuple(jnp.concatenate([x, zeros], axis=0) for x in vals)
    return vals


def _stack_heads(x, lane_head):
    return jnp.concatenate([jnp.where(lane_head == h, x, 0.0) for h in range(HPG)], axis=0)


def _rwkv_mix_kernel(p_ref, prev_ref, mu_ref, w0_ref, wup_ref, a0_ref, aup_ref, gup_ref, kk_ref, ka_ref,
                     s0_ref, lnw_ref, lnb_ref, rk_ref, y_ref, st_ref, s_scr, carry_ref, *, C):
    j = pl.program_id(1)

    @pl.when(j == 0)
    def _():
        s_scr[...] = s0_ref[0]
        carry_ref[...] = prev_ref[0]

    nc = max(p_ref.shape[0] // C, 1)
    prep = _rwkv_prep_rows(p_ref[...], carry_ref, mu_ref, w0_ref, wup_ref, a0_ref, aup_ref, gup_ref, kk_ref, ka_ref,
                           nc * C)

    R = HPG * C
    G = range(NG)
    lane_head = lax.broadcasted_iota(jnp.int32, (1, GW), 1) // RW_HEAD
    ii = lax.broadcasted_iota(jnp.int32, (R, R), 0)
    jj = lax.broadcasted_iota(jnp.int32, (R, R), 1)
    ti = lax.broadcasted_iota(jnp.int32, (C, C), 0)
    tj = lax.broadcasted_iota(jnp.int32, (C, C), 1)
    tril = jnp.where(ti >= tj, 1.0, 0.0).astype(BF16)
    ones = _head_ones(GW, RW_HEAD)
    strict = ii > jj
    incl = ii >= jj
    eye = jnp.where(ii == jj, 1.0, 0.0)
    sls = [slice(q * GW, (q + 1) * GW) for q in G]

    def state_independent(c):
        r_all, lw_all, k_all, v_all, kk_all, b_all, g_all = (x[c * C:(c + 1) * C] for x in prep)
        cum_all = _dot_exact_lhs(tril, lw_all)
        pc_all = jnp.exp(cum_all)
        pinv_all = jnp.exp(-cum_all)
        at_all = -kk_all * jnp.exp(cum_all - lw_all)
        bt_all = b_all * pinv_all
        kt_all = k_all * pinv_all
        rt_all = r_all * pc_all
        ar = [jnp.concatenate([_stack_heads(at_all[:, sl], lane_head),
                               _stack_heads(rt_all[:, sl], lane_head)], axis=0).astype(BF16) for sl in sls]
        bk = [jnp.concatenate([_stack_heads(bt_all[:, sl], lane_head),
                               _stack_heads(kt_all[:, sl], lane_head)], axis=0) for sl in sls]
        gram = [_dot(ar[q], bk[q].astype(BF16), NT) for q in G]
        a_ab = [jnp.where(strict, gram[q][:R, :R], 0.0) for q in G]
        a_kr = [jnp.concatenate([jnp.where(strict, gram[q][:R, R:], 0.0),
                                 jnp.where(incl, gram[q][R:, R:], 0.0)], axis=0).astype(BF16) for q in G]
        a_rb = [jnp.where(incl, gram[q][R:, :R], 0.0).astype(BF16) for q in G]
        t = [eye + jnp.where((ii >> 1) == (jj >> 1), a_ab[q], 0.0) for q in G]
        size = 2
        while size < C:
            sh = size.bit_length() - 1
            sel = ((ii >> (sh + 1)) == (jj >> (sh + 1))) & ((ii >> sh) != (jj >> sh))
            tb = [t[q].astype(BF16) for q in G]
            mid = [_dot(jnp.where(sel, a_ab[q], 0.0).astype(BF16), tb[q]) for q in G]
            t = [t[q] + _dot(tb[q], mid[q].astype(BF16)) for q in G]
            size *= 2
        return dict(r=r_all, k=k_all, v=v_all, g=g_all, pend=[pc_all[C - 1:C, sl] for sl in sls],
                    ar=ar, bk=bk, a_kr=a_kr, a_rb=a_rb, t=t)

    def state_dependent(c, d, s):
        v_st = [_stack_heads(d["v"][:, sl], lane_head) for sl in sls]
        from_state = [_dot(d["ar"][q], s[q].astype(BF16), NT) for q in G]
        from_v = [_dot(d["a_kr"][q], v_st[q].astype(BF16)) for q in G]
        u = [_dot1(d["t"][q], from_state[q][:R] + from_v[q][:R]) for q in G]
        y_st = [from_state[q][R:] + from_v[q][R:] + _dot(d["a_rb"][q], u[q].astype(BF16)) for q in G]
        s_new = [s[q] * d["pend"][q]
                 + _dot1(jnp.concatenate([u[q], v_st[q]], axis=0).T, d["bk"][q] * d["pend"][q]) for q in G]
        for q in G:
            sl = sls[q]
            y = y_st[q][0:C]
            for h in range(1, HPG):
                y = y + y_st[q][h * C:(h + 1) * C]
            mean = _dot_exact_rhs(y, ones) * (1.0 / RW_HEAD)
            yc = y - mean
            var = _dot_exact_rhs(yc * yc, ones) * (1.0 / RW_HEAD)
            yn = yc * lax.rsqrt(var + GN_EPS) * lnw_ref[:, sl] + lnb_ref[:, sl]
            bonus = _dot_exact_rhs(d["r"][:, sl] * d["k"][:, sl] * rk_ref[:, sl], ones) * d["v"][:, sl]
            y_ref[c * C:(c + 1) * C, sl] = ((yn + bonus) * d["g"][:, sl]).astype(y_ref.dtype)
        return s_new

    parts = [state_independent(c) for c in range(nc)]
    s = [s_scr[q] for q in G]
    for c in range(nc):
        s = state_dependent(c, parts[c], s)
    for q in G:
        s_scr[q] = s[q]

    @pl.when(j == pl.num_programs(1) - 1)
    def _():
        st_ref[0] = s_scr[...]


def _rwkv_mix(p_rw, row_off, B, T, tm, C, prev, s0, mu, w0, wup_pad, a0, aup_pad, gup, k_k, k_a, ln_w, ln_b, r_k):
    nt = T // tm
    off = row_off // tm
    t_out = max(tm, C)
    assert row_off % tm == 0 and T % tm == 0 and (tm <= C or tm % C == 0)
    row = lambda n: pl.BlockSpec((1, n), lambda b, j: (0, 0))
    full = lambda a: pl.BlockSpec(a.shape, lambda b, j: (0, 0))
    st_spec = pl.BlockSpec((1, NG, GW, GW), lambda b, j: (b, 0, 0, 0))
    return pl.pallas_call(
        functools.partial(_rwkv_mix_kernel, C=C),
        name="rwkv_mix",
        out_shape=[jax.ShapeDtypeStruct((B * nt * t_out, RW_WIDTH), BF16),
                   jax.ShapeDtypeStruct((B, NG, GW, GW), F32)],
        grid=(B, nt),
        in_specs=[pl.BlockSpec((tm, RW_COLS), lambda b, j: (off + b * nt + j, 0)),
                  pl.BlockSpec((1, 1, RW_COLS), lambda b, j: (b, 0, 0)),
                  row(RW_COLS), row(RW_WIDTH), full(wup_pad), row(RW_WIDTH), full(aup_pad), full(gup),
                  row(RW_WIDTH), row(RW_WIDTH), st_spec, row(RW_WIDTH), row(RW_WIDTH), row(RW_WIDTH)],
        out_specs=[pl.BlockSpec((t_out, RW_WIDTH), lambda b, j: (b * nt + j, 0)), st_spec],
        scratch_shapes=[pltpu.VMEM((NG, GW, GW), F32), pltpu.VMEM((1, RW_COLS), F32)],
        compiler_params=_params(("parallel", "arbitrary"), V7X_VMEM_LIMIT),
    )(p_rw, prev.reshape(B, 1, RW_COLS), mu.reshape(1, -1), w0.reshape(1, -1), wup_pad, a0.reshape(1, -1),
      aup_pad, gup, k_k.reshape(1, -1), k_a.reshape(1, -1), s0, ln_w.reshape(1, -1), ln_b.reshape(1, -1),
      r_k.reshape(1, -1))


def _state_to_blockdiag(s):
    B = s.shape[0]
    s = s.reshape(B, NG, HPG, RW_HEAD, RW_HEAD)
    eye = jnp.eye(HPG, dtype=s.dtype)
    bd = s[:, :, :, :, None, :] * eye[None, None, :, None, :, None]
    return bd.reshape(B, NG, GW, GW)


def _blockdiag_to_state(bd):
    B = bd.shape[0]
    x = bd.reshape(B, NG, HPG, RW_HEAD, HPG, RW_HEAD)
    return jnp.stack([x[:, :, h, :, h, :] for h in range(HPG)], axis=2).reshape(B, RW_HEADS, RW_HEAD, RW_HEAD)


def _gelu(x):
    return 0.5 * x * (1.0 + lax.erf(x * (2.0 ** -0.5)))


def _gmlp_kernel(u_ref, v_ref, lng_ref, lnb_ref, ws_ref, bs_ref, y_ref, vn_ref, *, tm):
    u = _gelu(u_ref[...])
    vf = _gelu(v_ref[...])
    mean = jnp.mean(vf, axis=-1, keepdims=True)
    vc = vf - mean
    var = jnp.mean(vc * vc, axis=-1, keepdims=True)
    vn = vc * lax.rsqrt(var + LN_EPS) * lng_ref[...] + lnb_ref[...]
    vn_ref[...] = vn
    ti = lax.broadcasted_iota(jnp.int32, (GM_CHUNK, GM_CHUNK), 0)
    tj = lax.broadcasted_iota(jnp.int32, (GM_CHUNK, GM_CHUNK), 1)
    causal = ti >= tj
    rows = min(tm, GM_CHUNK)
    for g in range(GM_GROUPS):
        wm = jnp.where(causal, ws_ref[g], 0.0).astype(BF16)
        sl = slice(g * GM_GROUP_DIM, (g + 1) * GM_GROUP_DIM)
        for c in range(max(tm // GM_CHUNK, 1)):
            rs = slice(c * GM_CHUNK, c * GM_CHUNK + rows)
            vg = vn[rs, sl].astype(BF16)
            if rows < GM_CHUNK:
                vg = jnp.concatenate([vg, jnp.zeros((GM_CHUNK - rows, GM_GROUP_DIM), BF16)], axis=0)
            mixed = (_dot(wm, vg) + bs_ref[g])[:rows]
            y_ref[rs, sl] = (u[rs, sl] * mixed).astype(y_ref.dtype)


def _gmlp(p_gm, row_off, rows, tm, ln_g, ln_b, ws, layer, bs_b, y_dtype):
    off = row_off // tm
    assert row_off % tm == 0 and rows % tm == 0
    row = pl.BlockSpec((1, GM_WIDTH), lambda i: (0, 0))
    ws_spec = pl.BlockSpec((None, GM_GROUPS, GM_CHUNK, GM_CHUNK), lambda i: (layer, 0, 0, 0))
    return pl.pallas_call(
        functools.partial(_gmlp_kernel, tm=tm),
        name="gmlp",
        out_shape=[jax.ShapeDtypeStruct((rows, GM_WIDTH), y_dtype), jax.ShapeDtypeStruct((rows, GM_WIDTH), F32)],
        grid=(rows // tm,),
        in_specs=[pl.BlockSpec((tm, GM_WIDTH), lambda i: (off + i, 0)),
                  pl.BlockSpec((tm, GM_WIDTH), lambda i: (off + i, 1)),
                  row, row, ws_spec, ws_spec],
        out_specs=[pl.BlockSpec((tm, GM_WIDTH), lambda i: (i, 0))] * 2,
        compiler_params=_params(("parallel",), V7X_VMEM_LIMIT),
    )(p_gm, p_gm, ln_g.reshape(1, -1), ln_b.reshape(1, -1), ws, bs_b)


def _attn_prep_kernel(q_ref, k_ref, cos_ref, sin_ref, qg_ref, kg_ref, qo_ref, ko_ref):
    cos = cos_ref[...]
    sin = sin_ref[...]

    ones = jnp.ones((ATT_HEAD, ATT_HEAD), BF16)

    def norm_rope(x, gain):
        ssq = _dot((x * x).astype(BF16), ones)
        y = x * lax.rsqrt(ssq * (1.0 / ATT_HEAD) + NORM_EPS) * gain
        return y * cos + pltpu.roll(y, ATT_HEAD // 2, axis=1) * sin

    for h in range(N_DIL * ATT_GROUP_HEADS):
        sl = slice(h * ATT_HEAD, (h + 1) * ATT_HEAD)
        qo_ref[:, sl] = norm_rope(q_ref[:, sl], qg_ref[...]).astype(qo_ref.dtype)
        ko_ref[:, sl] = norm_rope(k_ref[:, sl], kg_ref[...]).astype(ko_ref.dtype)


def _attn_prep(p_at, cos, sin, q_gain, k_gain, tm):
    M = p_at.shape[0]
    blk = lambda c: pl.BlockSpec((tm, ATT_WIDTH), lambda i: (i, c))
    tab = pl.BlockSpec((tm, ATT_HEAD), lambda i: (i, 0))
    gain = pl.BlockSpec((1, ATT_HEAD), lambda i: (0, 0))
    return pl.pallas_call(
        _attn_prep_kernel,
        name="attn_prep",
        out_shape=[jax.ShapeDtypeStruct((M, ATT_WIDTH), F32), jax.ShapeDtypeStruct((M, ATT_WIDTH), F32)],
        grid=(M // tm,),
        in_specs=[blk(0), blk(1), tab, tab, gain, gain],
        out_specs=[pl.BlockSpec((tm, ATT_WIDTH), lambda i: (i, 0))] * 2,
        compiler_params=_params(("parallel",), V7X_VMEM_LIMIT),
    )(p_at, p_at, cos, sin, q_gain.reshape(1, -1), k_gain.reshape(1, -1))


def _attn_prompt_kernel(q0, q1, q2, k0, k1, k2, v0, v1, v2, y_ref, o_scr, lse_scr, *, T):
    scale = ATT_HEAD ** -0.5
    qi = lax.broadcasted_iota(jnp.int32, (ATT_BLK, ATT_BLK), 0)
    kj = lax.broadcasted_iota(jnp.int32, (ATT_BLK, ATT_BLK), 1)
    refs = ((q0, k0, v0), (q1, k1, v1), (q2, k2, v2))
    for g, (window, dil) in enumerate(DIL_PAIRS):
        q_ref, k_ref, v_ref = refs[g]
        nb = T // dil // ATT_BLK

        def rows(r, n, dil=dil):
            if dil == 1:
                return pl.ds(n * ATT_BLK, ATT_BLK)
            return pl.ds(r + n * ATT_BLK * dil, ATT_BLK, stride=dil)

        blocks = [(rows(r, n), rows(r, n - 1) if n > 0 else None) for r in range(dil) for n in range(nb)]
        for b0 in range(0, len(blocks), ATTN_BATCH):
            batch = blocks[b0:b0 + ATTN_BATCH]
            q = [q_ref[rs, :].astype(BF16) for rs, _ in batch]
            s_c = [jnp.where(kj <= qi, _dot(q[i], k_ref[rs, :].astype(BF16), NT) * scale, NEG_INF)
                   for i, (rs, _) in enumerate(batch)]
            s_p = [None if ps_ is None else
                   jnp.where(kj >= qi, _dot(q[i], k_ref[ps_, :].astype(BF16), NT) * scale, NEG_INF)
                   for i, (_, ps_) in enumerate(batch)]
            m = [jnp.max(s, axis=-1, keepdims=True) for s in s_c]
            m = [mc if sp is None else jnp.maximum(mc, jnp.max(sp, axis=-1, keepdims=True))
                 for mc, sp in zip(m, s_p)]
            e_c = [jnp.exp(s - mm) for s, mm in zip(s_c, m)]
            e_p = [None if sp is None else jnp.exp(sp - mm) for sp, mm in zip(s_p, m)]
            den = [jnp.sum(e, axis=-1, keepdims=True) for e in e_c]
            den = [d if e is None else d + jnp.sum(e, axis=-1, keepdims=True) for d, e in zip(den, e_p)]
            acc = [_dot(e.astype(BF16), v_ref[rs, :].astype(BF16)) for e, (rs, _) in zip(e_c, batch)]
            acc = [a if e is None else a + _dot(e.astype(BF16), v_ref[ps_, :].astype(BF16))
                   for a, e, (_, ps_) in zip(acc, e_p, batch)]
            for i, (rs, _) in enumerate(batch):
                o_scr[g, rs, :] = acc[i] / den[i]
                lse_scr[g, rs, :] = jnp.broadcast_to(m[i] + jnp.log(den[i]), (ATT_BLK, ATT_HEAD))
    a, b, c = lse_scr[0], lse_scr[1], lse_scr[2]
    m = jnp.maximum(jnp.maximum(a, b), c)
    wa, wb, wc = jnp.exp(a - m), jnp.exp(b - m), jnp.exp(c - m)
    y = (wa * o_scr[0] + wb * o_scr[1] + wc * o_scr[2]) / (wa + wb + wc)
    y_ref[...] = y.astype(y_ref.dtype)


def _attn_prompt(q_rot, k_rot, p_at, B, T):
    nh = ATT_GROUP_HEADS
    col = lambda g, base: pl.BlockSpec((T, ATT_HEAD), lambda b, h: (b, base + g * nh + h))
    vbase = 2 * ATT_WIDTH // ATT_HEAD
    return pl.pallas_call(
        functools.partial(_attn_prompt_kernel, T=T),
        name="attn_prompt",
        out_shape=jax.ShapeDtypeStruct((B * T, ATT_OUT), BF16),
        grid=(B, nh),
        in_specs=[col(g, 0) for g in range(N_DIL)] * 2 + [col(g, vbase) for g in range(N_DIL)],
        out_specs=pl.BlockSpec((T, ATT_HEAD), lambda b, h: (b, h)),
        scratch_shapes=[pltpu.VMEM((N_DIL, T, ATT_HEAD), F32)] * 2,
        compiler_params=_params(("parallel", "parallel"), V7X_VMEM_LIMIT),
    )(q_rot, q_rot, q_rot, k_rot, k_rot, k_rot, p_at, p_at, p_at)


def _attn_sample_kernel(q_ref, k_ref, v_ref, c0_ref, c1_ref, c2_ref, y_ref, *, T):
    scale = ATT_HEAD ** -0.5
    caches = (c0_ref, c1_ref, c2_ref)
    zpad = jnp.zeros((LANES - T, ATT_HEAD), F32)
    outs = [[None] * N_DIL for _ in range(ATT_GROUP_HEADS)]
    lses = [[None] * N_DIL for _ in range(ATT_GROUP_HEADS)]
    for g, (window, dil) in enumerate(DIL_PAIRS):
        cref = caches[g]
        wb = cref.shape[1]
        nkeys = wb + LANES
        t = lax.broadcasted_iota(jnp.int32, (T, nkeys), 0)
        jrow = lax.broadcasted_iota(jnp.int32, (T, nkeys), 1)
        dist = wb + t - jrow
        valid = (dist >= 0) & (dist <= window) & ((dist & (dil - 1)) == 0)
        for h in range(ATT_GROUP_HEADS):
            sl = slice((g * ATT_GROUP_HEADS + h) * ATT_HEAD, (g * ATT_GROUP_HEADS + h + 1) * ATT_HEAD)
            q = q_ref[:, sl].astype(BF16)
            kcat = jnp.concatenate([cref[0, :, 0, h, :], k_ref[:, sl], zpad], axis=0).astype(BF16)
            vcat = jnp.concatenate([cref[0, :, 1, h, :], v_ref[:, sl], zpad], axis=0).astype(BF16)
            s = jnp.where(valid, _dot(q, kcat, NT) * scale, NEG_INF)
            m = jnp.max(s, axis=-1, keepdims=True)
            lse = m + jnp.log(jnp.sum(jnp.exp(s - m), axis=-1, keepdims=True))
            outs[h][g] = _dot(jnp.exp(s - lse).astype(BF16), vcat)
            lses[h][g] = lse
    for h in range(ATT_GROUP_HEADS):
        m = jnp.maximum(jnp.maximum(lses[h][0], lses[h][1]), lses[h][2])
        w = [jnp.exp(l - m) for l in lses[h]]
        y = (w[0] * outs[h][0] + w[1] * outs[h][1] + w[2] * outs[h][2]) / (w[0] + w[1] + w[2])
        y_ref[:, h * ATT_HEAD:(h + 1) * ATT_HEAD] = y.astype(y_ref.dtype)


def _attn_sample(q_rot, k_rot, p_at, row_off, B, T, caches, layer):
    off = row_off // T
    blk = lambda c: pl.BlockSpec((T, ATT_WIDTH), lambda b: (off + b, c))
    cspec = lambda a: pl.BlockSpec((None, 1) + a.shape[2:], lambda b: (layer, b, 0, 0, 0, 0))
    return pl.pallas_call(
        functools.partial(_attn_sample_kernel, T=T),
        name="attn_sample",
        out_shape=jax.ShapeDtypeStruct((B * T, ATT_OUT), F32),
        grid=(B,),
        in_specs=[blk(0), blk(0), blk(2)] + [cspec(c) for c in caches],
        out_specs=pl.BlockSpec((T, ATT_OUT), lambda b: (b, 0)),
        compiler_params=_params(("parallel",), V7X_VMEM_LIMIT),
    )(q_rot, k_rot, p_at, *caches)


def _branch_kernel(yr_ref, yg_ref, ya_ref, wr_ref, wg_ref, wa_ref, g0_ref, g1_ref, g2_ref, o_ref):
    acc = _sigmoid(g0_ref[...]) * _dot(yr_ref[...], wr_ref[...].astype(BF16))
    acc = acc + _sigmoid(g1_ref[...]) * _dot(yg_ref[...], wg_ref[...].astype(BF16))
    acc = acc + _sigmoid(g2_ref[...]) * _dot(ya_ref[...], wa_ref[...].astype(BF16))
    o_ref[...] = acc.astype(o_ref.dtype)


def _branch(y_rw, y_gm, y_at, w_rw, w_gm, w_at, layer, p_gate, tm, tn):
    M = y_rw.shape[0]
    nb = D_MODEL // tn
    lhs = lambda kdim: pl.BlockSpec((tm, kdim), lambda i, j: (i, 0))
    rhs = lambda kdim: pl.BlockSpec((None, kdim, tn), lambda i, j: (layer, 0, j))
    gate = lambda br: pl.BlockSpec((tm, tn), lambda i, j: (i, br * nb + j))
    return pl.pallas_call(
        _branch_kernel,
        name="branch_merge",
        out_shape=jax.ShapeDtypeStruct((M, D_MODEL), BF16),
        grid=(M // tm, nb),
        in_specs=[lhs(RW_WIDTH), lhs(GM_WIDTH), lhs(ATT_OUT), rhs(RW_WIDTH), rhs(GM_WIDTH), rhs(ATT_OUT),
                  gate(0), gate(1), gate(2)],
        out_specs=pl.BlockSpec((tm, tn), lambda i, j: (i, j)),
        compiler_params=_params(("parallel", "parallel"), V7X_VMEM_LIMIT),
    )(y_rw, y_gm, y_at, w_rw, w_gm, w_at, p_gate, p_gate, p_gate)


def _rope_tables(pos):
    half = ATT_HEAD // 2
    inv = ROPE_THETA ** (-jnp.arange(half, dtype=F32) / half)
    ang = pos.astype(F32)[:, None] * inv[None, :]
    cos, sin = jnp.cos(ang), jnp.sin(ang)
    return jnp.concatenate([cos, cos], -1), jnp.concatenate([-sin, sin], -1)


def _kv_rows_kernel(*refs):
    k_ref, v_ref, o_ref = refs[0], refs[1], refs[-1]
    for h in range(ATT_GROUP_HEADS):
        sl = slice(h * ATT_HEAD, (h + 1) * ATT_HEAD)
        o_ref[0, :, 0, h, :] = k_ref[:, sl]
        o_ref[0, :, 1, h, :] = v_ref[:, sl]


def _kv_rows(k_rot, p_at, buf, layer, depth, row0, B, T, keep, g):
    tk = min(keep, 2 * ATT_BLK)
    first = (row0 + T - keep) // tk
    assert (row0 + T - keep) % tk == 0 and T % tk == 0
    rows = lambda c: pl.BlockSpec((tk, ATT_OUT), lambda b, i: (first + b * (T // tk) + i, c))
    shape = (depth, B, keep, 2, ATT_GROUP_HEADS, ATT_HEAD)
    in_specs = [rows(g), rows(2 * ATT_WIDTH // ATT_OUT + g)]
    args = [k_rot, p_at]
    if buf is not None:
        in_specs.append(pl.BlockSpec(memory_space=pl.ANY))
        args.append(buf)
    return pl.pallas_call(
        _kv_rows_kernel,
        name="kv_rows",
        out_shape=jax.ShapeDtypeStruct(shape, F32),
        grid=(B, keep // tk),
        in_specs=in_specs,
        out_specs=pl.BlockSpec((None, 1, tk, 2, ATT_GROUP_HEADS, ATT_HEAD), lambda b, i: (layer, b, i, 0, 0, 0)),
        input_output_aliases={} if buf is None else {2: 0},
        compiler_params=_params(("parallel", "parallel"), V7X_VMEM_LIMIT),
    )(*args)


def kernel(x_prompt, x_sample, cache_kv_w128, cache_kv_w512, cache_kv_w2048, state_rwkv, state_rwkv_shift, norm1, w_in, rw_mu, rw_w0, rw_w_up, rw_a0, rw_a_up, rw_g_up, rw_k_k, rw_k_a, rw_r_k, rw_ln_w, rw_ln_b, gm_ln_g, gm_ln_b, gm_ws, gm_bs, att_q_gain, att_k_gain, w_br_rwkv, w_br_gmlp, w_br_attn, w_out, norm2, w_ff1, w_ff2):
    BP, TP, _ = x_prompt.shape
    BS, TS, _ = x_sample.shape
    depth = w_in.shape[0]
    MP, MS = BP * TP, BS * TS
    M = MP + MS
    TM = ROW_TILE
    assert M % (2 * TM) == 0 and MP % 1024 == 0 and TP % (2 * SCAN_C) == 0 and TS <= SAMPLE_C
    x = jnp.concatenate([x_prompt.reshape(MP, D_MODEL), x_sample.reshape(MS, D_MODEL)], axis=0)

    pos = jnp.concatenate([jnp.tile(jnp.arange(TP), BP), jnp.tile(PAST_LEN + jnp.arange(TS), BS)])
    cos_t, sin_t = _rope_tables(pos)
    caches_all = [cache_kv_w128, cache_kv_w512, cache_kv_w2048]
    zeros_state = jnp.zeros((BP, NG, GW, GW), F32)
    zeros_shift = jnp.zeros((BP, RW_COLS), F32)
    bs_b = jnp.broadcast_to(gm_bs[:, :, :, None], (depth, GM_GROUPS, GM_CHUNK, GM_GROUP_DIM))
    c_rw, c_gm, c_at = RW_COLS, RW_COLS + 2 * GM_WIDTH, RW_COLS + 2 * GM_WIDTH + 3 * ATT_WIDTH

    kvp = [None] * N_DIL
    kvs = [None] * N_DIL
    st_p, st_s, sh_p, sh_s, gmv_s = [], [], [], [], []
    for l in range(depth):
        if l == 0:
            xg, ssq = _norm_operand(x, norm1[0], TM)
        mm_in = functools.partial(_matmul, xg, w_in, l, tm=TM * 2, tk=D_MODEL, row_ssq=ssq)
        p_rw = mm_in(col_off=0, n_cols=RW_COLS, tn=256, name="proj_rwkv")
        p_gm = mm_in(col_off=c_rw, n_cols=2 * GM_WIDTH, tn=512, name="proj_gmlp")
        p_at = mm_in(col_off=c_gm, n_cols=3 * ATT_WIDTH, tn=512, name="proj_attn")
        p_gate = mm_in(col_off=c_at, n_cols=N_BRANCH * D_MODEL, tn=512, name="proj_gates")

        zpad = jnp.zeros((DECAY_LORA, RW_WIDTH), F32)
        wup_pad = jnp.concatenate([rw_w_up[l], zpad], axis=0)
        aup_pad = jnp.concatenate([zpad, rw_a_up[l]], axis=0)
        mix = functools.partial(_rwkv_mix, p_rw, mu=rw_mu[l], w0=rw_w0[l], wup_pad=wup_pad, a0=rw_a0[l],
                                aup_pad=aup_pad, gup=rw_g_up[l], k_k=rw_k_k[l].reshape(-1), k_a=rw_k_a[l].reshape(-1),
                                ln_w=rw_ln_w[l].reshape(-1), ln_b=rw_ln_b[l].reshape(-1), r_k=rw_r_k[l].reshape(-1))
        y_rw_p, sT_p = mix(row_off=0, B=BP, T=TP, tm=2 * SCAN_C, C=SCAN_C, prev=zeros_shift, s0=zeros_state)
        y_rw_s, sT_s = mix(row_off=MP, B=BS, T=TS, tm=TS, C=SAMPLE_C, prev=state_rwkv_shift[l],
                           s0=_state_to_blockdiag(state_rwkv[l]))
        y_rw_s = y_rw_s.reshape(BS, SAMPLE_C, RW_WIDTH)[:, :TS].reshape(MS, RW_WIDTH)
        y_rw = jnp.concatenate([y_rw_p, y_rw_s], axis=0)
        st_p.append(_blockdiag_to_state(sT_p))
        st_s.append(_blockdiag_to_state(sT_s))
        sh_p.append(p_rw[TP - 1:MP:TP])
        sh_s.append(p_rw[MP + TS - 1::TS])

        y_gm_p, _ = _gmlp(p_gm, 0, MP, 256, gm_ln_g[l], gm_ln_b[l], gm_ws, l, bs_b, BF16)
        y_gm_s, vn_s = _gmlp(p_gm, MP, MS, TS, gm_ln_g[l], gm_ln_b[l], gm_ws, l, bs_b, F32)
        y_gm = jnp.concatenate([y_gm_p, y_gm_s.astype(BF16)], axis=0)
        gmv_s.append(vn_s.reshape(BS, TS, GM_WIDTH))

        q_rot, k_rot = _attn_prep(p_at, cos_t, sin_t, att_q_gain[l], att_k_gain[l], TM)
        for g, (window, dil) in enumerate(DIL_PAIRS):
            keep = min(window, TP)
            kvp[g] = _kv_rows(k_rot, p_at, kvp[g], l, depth, 0, BP, TP, keep, g)
            kvs[g] = _kv_rows(k_rot, p_at, kvs[g], l, depth, MP, BS, TS, TS, g)
        y_at_p = _attn_prompt(q_rot, k_rot, p_at, BP, TP)
        y_at_s = _attn_sample(q_rot, k_rot, p_at, MP, BS, TS, caches_all, l)
        y_at = jnp.concatenate([y_at_p, y_at_s.astype(BF16)], axis=0)

        merged = _branch(y_rw, y_gm, y_at, w_br_rwkv, w_br_gmlp, w_br_attn, l, p_gate, TM * 2, 256)
        x, xg2, ssq2 = _matmul(merged, w_out, l, tm=TM * 2, tn=256, tk=D_MODEL, epilogue="residual", res=x,
                               next_gain=norm2[l], name="out_proj")
        act = _matmul(xg2, w_ff1, l, tm=TM * 2, tn=512, tk=D_MODEL, epilogue="relu2", out_dtype=BF16,
                      row_ssq=ssq2, name="ffn_up")
        ffn_down = functools.partial(_matmul, act, w_ff2, l, tm=TM * 2, tn=1024, tk=1024, epilogue="residual",
                                     res=x, name="ffn_down")
        if l + 1 < depth:
            x, xg, ssq = ffn_down(next_gain=norm1[l + 1])
        else:
            x = ffn_down()

    return (x[:MP].reshape(BP, TP, D_MODEL), x[MP:].reshape(BS, TS, D_MODEL),
            kvp[0], kvp[1], kvp[2], kvs[0], kvs[1], kvs[2],
            jnp.stack(st_p, 0), jnp.stack(st_s, 0), jnp.stack(sh_p, 0), jnp.stack(sh_s, 0),
            jnp.stack(gmv_s, 0))
```

```python
import functools

import jax
import jax.numpy as jnp
from jax import lax
from jax.experimental import pallas as pl
from jax.experimental.pallas import tpu as pltpu

F32 = jnp.float32
BF16 = jnp.bfloat16

LANES = 128
V7X_VMEM_LIMIT = 56 * 1024 * 1024

D_MODEL = 4096
RW_HEADS = 24
RW_HEAD = 64
RW_WIDTH = RW_HEADS * RW_HEAD
DECAY_LORA = 64
AAA_LORA = 64
GATE_LORA = 128
RW_COLS = 3 * RW_WIDTH + DECAY_LORA + AAA_LORA + GATE_LORA
GN_EPS = 64e-5
GM_CHUNK = 128
GM_GROUPS = 12
GM_GROUP_DIM = 128
GM_WIDTH = GM_GROUPS * GM_GROUP_DIM
LN_EPS = 1e-5
DIL_PAIRS = ((128, 1), (512, 4), (2048, 16))
N_DIL = 3
ATT_GROUP_HEADS = 4
ATT_HEAD = 128
ATT_WIDTH = N_DIL * ATT_GROUP_HEADS * ATT_HEAD
ATT_OUT = ATT_GROUP_HEADS * ATT_HEAD
ATT_BLK = 128
ATTN_BATCH = 8
ROPE_THETA = 10000.0
N_BRANCH = 3
D_FF = 4 * D_MODEL
NORM_EPS = 1e-6
NEG_INF = -1e30
PAST_LEN = 8192

ROW_TILE = 688

HPG = 4
GW = HPG * RW_HEAD
NG = RW_HEADS // HPG
SCAN_C = 64
SAMPLE_C = 32

NN = (((1,), (0,)), ((), ()))
NT = (((1,), (1,)), ((), ()))


def _dot(a, b, dims=NN):
    return lax.dot_general(a, b, dims, preferred_element_type=F32)


def _dot1(a, b, dims=NN):
    return _dot(a.astype(BF16), b.astype(BF16), dims)


def _split2(a):
    hi = a.astype(BF16)
    return hi, (a - hi.astype(F32)).astype(BF16)


def _dot_exact_rhs(a, e):
    hi, lo = _split2(a)
    return _dot(hi, e) + _dot(lo, e)


def _dot_exact_lhs(e, a):
    hi, lo = _split2(a)
    return _dot(e, hi) + _dot(e, lo)


def _params(sem, vmem=None):
    return pltpu.CompilerParams(dimension_semantics=sem, vmem_limit_bytes=vmem)


def _sigmoid(x):
    return 0.5 * jnp.tanh(0.5 * x) + 0.5


def _mm_kernel(*refs, nk, epilogue, scaled, normed):
    refs = list(refs)
    a_ref, b_ref = refs.pop(0), refs.pop(0)
    ssq_ref = refs.pop(0) if scaled else None
    res_ref = refs.pop(0) if epilogue == "residual" else None
    gain_ref = refs.pop(0) if normed else None
    o_ref = refs.pop(0)
    dot = lambda: _dot(a_ref[...], b_ref[0].astype(BF16))
    j = pl.program_id(1)

    def emit_normed(x_new):
        xg_ref, ssq_out = refs
        xg_ref[...] = (x_new * gain_ref[...]).astype(xg_ref.dtype)
        row = jnp.broadcast_to(jnp.sum(x_new * x_new, axis=-1, keepdims=True), ssq_out.shape)

        @pl.when(j == 0)
        def _():
            ssq_out[...] = row

        @pl.when(j > 0)
        def _():
            ssq_out[...] += row

    if nk == 1:
        part = dot()
        if scaled:
            part = part * lax.rsqrt(ssq_ref[:, 0:1] * (1.0 / D_MODEL) + NORM_EPS)
        if epilogue == "relu2":
            part = jnp.square(jnp.maximum(part, 0.0))
        elif epilogue == "residual":
            part = part + res_ref[...]
        o_ref[...] = part.astype(o_ref.dtype)
        if normed:
            emit_normed(part)
    else:
        assert epilogue == "residual" and not scaled
        k = pl.program_id(2)

        @pl.when(k == 0)
        def _():
            o_ref[...] = res_ref[...] + dot()

        @pl.when((k > 0) & (k < nk - 1) if normed else k > 0)
        def _():
            o_ref[...] = o_ref[...] + dot()

        if normed:
            @pl.when(k == nk - 1)
            def _():
                x_new = o_ref[...] + dot()
                o_ref[...] = x_new
                emit_normed(x_new)


def _matmul(a, b, layer, *, col_off=0, n_cols=None, tm, tn, tk, epilogue="none", res=None, out_dtype=F32,
            row_ssq=None, next_gain=None, name="matmul"):
    M, K = a.shape
    n_cols = b.shape[2] if n_cols is None else n_cols
    assert M % tm == 0 and n_cols % tn == 0 and K % tk == 0 and col_off % LANES == 0
    nk = K // tk
    assert nk == 1 or out_dtype == F32
    scaled, normed = row_ssq is not None, next_gain is not None
    in_specs = [pl.BlockSpec((tm, tk), lambda i, j, k: (i, k)),
                pl.BlockSpec((pl.Element(1), pl.Element(tk), pl.Element(tn)),
                             lambda i, j, k: (layer, pl.multiple_of(k * tk, tk),
                                              pl.multiple_of(col_off + j * tn, LANES)))]
    args = [a, b]
    tile = pl.BlockSpec((tm, tn), lambda i, j, k: (i, j))
    stat = pl.BlockSpec((tm, LANES), lambda i, j, k: (i, 0))
    if scaled:
        in_specs.append(stat)
        args.append(row_ssq)
    if epilogue == "residual":
        in_specs.append(tile)
        args.append(res)
    out_shape = [jax.ShapeDtypeStruct((M, n_cols), out_dtype)]
    out_specs = [tile]
    if normed:
        in_specs.append(pl.BlockSpec((1, tn), lambda i, j, k: (0, j)))
        args.append(next_gain.reshape(1, n_cols))
        out_shape += [jax.ShapeDtypeStruct((M, n_cols), BF16), jax.ShapeDtypeStruct((M, LANES), F32)]
        out_specs += [tile, stat]
    outs = pl.pallas_call(
        functools.partial(_mm_kernel, nk=nk, epilogue=epilogue, scaled=scaled, normed=normed),
        out_shape=out_shape,
        grid=(M // tm, n_cols // tn, nk),
        in_specs=in_specs,
        out_specs=out_specs,
        compiler_params=_params(("parallel", "arbitrary" if normed else "parallel", "arbitrary"), V7X_VMEM_LIMIT),
        name=name,
    )(*args)
    return outs if normed else outs[0]


def _norm_operand_kernel(x_ref, g_ref, xg_ref, ssq_ref):
    x = x_ref[...]
    xg_ref[...] = (x * g_ref[...]).astype(xg_ref.dtype)
    ssq_ref[...] = jnp.broadcast_to(jnp.sum(x * x, axis=-1, keepdims=True), ssq_ref.shape)


def _norm_operand(x, g, tm):
    M, D = x.shape
    return pl.pallas_call(
        _norm_operand_kernel,
        name="norm_operand",
        out_shape=[jax.ShapeDtypeStruct((M, D), BF16), jax.ShapeDtypeStruct((M, LANES), F32)],
        grid=(M // tm,),
        in_specs=[pl.BlockSpec((tm, D), lambda i: (i, 0)), pl.BlockSpec((1, D), lambda i: (0, 0))],
        out_specs=[pl.BlockSpec((tm, D), lambda i: (i, 0)), pl.BlockSpec((tm, LANES), lambda i: (i, 0))],
        compiler_params=_params(("parallel",), V7X_VMEM_LIMIT),
    )(x, g.reshape(1, D))


def _head_ones(width, head):
    r = lax.broadcasted_iota(jnp.int32, (width, width), 0) // head
    c = lax.broadcasted_iota(jnp.int32, (width, width), 1) // head
    return jnp.where(r == c, 1.0, 0.0).astype(BF16)


def _rwkv_prep_rows(p, carry_ref, mu_ref, w0_ref, wup_ref, a0_ref, aup_ref, gup_ref, kk_ref, ka_ref, pad_to):
    tm = p.shape[0]
    row = lax.broadcasted_iota(jnp.int32, p.shape, 0)
    p_prev = jnp.where(row == 0, carry_ref[...], pltpu.roll(p, 1, axis=0))
    carry_ref[...] = p[tm - 1:tm, :]
    ps = p + mu_ref[...] * (p_prev - p)

    W = RW_WIDTH
    r = ps[:, 0:W]
    k = ps[:, W:2 * W]
    v = ps[:, 2 * W:3 * W]
    wa = ps[:, 3 * W:3 * W + LANES]
    g_in = ps[:, 3 * W + LANES:3 * W + 2 * LANES]

    z = -(w0_ref[...] + _dot(jnp.tanh(wa).astype(BF16), wup_ref[...].astype(BF16)))
    softplus = jnp.maximum(z, 0.0) + jnp.log1p(jnp.exp(-jnp.abs(z)))
    lw = -jnp.exp(-softplus - 0.5)
    a = _sigmoid(a0_ref[...] + _dot(wa.astype(BF16), aup_ref[...].astype(BF16)))
    g = _dot(_sigmoid(g_in).astype(BF16), gup_ref[...].astype(BF16))

    kk = k * kk_ref[...]
    ones = _head_ones(GW, RW_HEAD)
    kmod = k * (1.0 + (a - 1.0) * ka_ref[...])
    kn = []
    for q in range(NG):
        kq = kk[:, q * GW:(q + 1) * GW]
        ssq = _dot_exact_rhs(kq * kq, ones)
        kn.append(kq * lax.rsqrt(jnp.maximum(ssq, 1e-24)))
    kn = jnp.concatenate(kn, axis=1)
    vals = (r, lw, kmod, v, kn, kn * a, g)
    if pad_to > tm:
        zeros = jnp.zeros((pad_to - tm, RW_WIDTH), F32)
        vals = tuple(jnp.concatenate([x, zeros], axis=0) for x in vals)
    return vals


def _stack_heads(x, lane_head):
    return jnp.concatenate([jnp.where(lane_head == h, x, 0.0) for h in range(HPG)], axis=0)


def _rwkv_mix_kernel(p_ref, prev_ref, mu_ref, w0_ref, wup_ref, a0_ref, aup_ref, gup_ref, kk_ref, ka_ref,
                     s0_ref, lnw_ref, lnb_ref, rk_ref, y_ref, st_ref, sh_ref, s_scr, carry_ref, *, C):
    j = pl.program_id(1)

    @pl.when(j == 0)
    def _():
        s_scr[...] = s0_ref[0]
        carry_ref[...] = prev_ref[0]

    nc = max(p_ref.shape[0] // C, 1)
    prep = _rwkv_prep_rows(p_ref[...], carry_ref, mu_ref, w0_ref, wup_ref, a0_ref, aup_ref, gup_ref, kk_ref, ka_ref,
                           nc * C)

    R = HPG * C
    G = range(NG)
    lane_head = lax.broadcasted_iota(jnp.int32, (1, GW), 1) // RW_HEAD
    ii = lax.broadcasted_iota(jnp.int32, (R, R), 0)
    jj = lax.broadcasted_iota(jnp.int32, (R, R), 1)
    ti = lax.broadcasted_iota(jnp.int32, (C, C), 0)
    tj = lax.broadcasted_iota(jnp.int32, (C, C), 1)
    tril = jnp.where(ti >= tj, 1.0, 0.0).astype(BF16)
    ones = _head_ones(GW, RW_HEAD)
    strict = ii > jj
    incl = ii >= jj
    eye = jnp.where(ii == jj, 1.0, 0.0)
    sls = [slice(q * GW, (q + 1) * GW) for q in G]

    def state_independent(c):
        r_all, lw_all, k_all, v_all, kk_all, b_all, g_all = (x[c * C:(c + 1) * C] for x in prep)
        cum_all = _dot_exact_lhs(tril, lw_all)
        pc_all = jnp.exp(cum_all)
        pinv_all = jnp.exp(-cum_all)
        at_all = -kk_all * jnp.exp(cum_all - lw_all)
        bt_all = b_all * pinv_all
        kt_all = k_all * pinv_all
        rt_all = r_all * pc_all
        ar = [jnp.concatenate([_stack_heads(at_all[:, sl], lane_head),
                               _stack_heads(rt_all[:, sl], lane_head)], axis=0).astype(BF16) for sl in sls]
        bk = [jnp.concatenate([_stack_heads(bt_all[:, sl], lane_head),
                               _stack_heads(kt_all[:, sl], lane_head)], axis=0) for sl in sls]
        gram = [_dot(ar[q], bk[q].astype(BF16), NT) for q in G]
        a_ab = [jnp.where(strict, gram[q][:R, :R], 0.0) for q in G]
        a_kr = [jnp.concatenate([jnp.where(strict, gram[q][:R, R:], 0.0),
                                 jnp.where(incl, gram[q][R:, R:], 0.0)], axis=0).astype(BF16) for q in G]
        a_rb = [jnp.where(incl, gram[q][R:, :R], 0.0).astype(BF16) for q in G]
        t = [eye + jnp.where((ii >> 1) == (jj >> 1), a_ab[q], 0.0) for q in G]
        size = 2
        while size < C:
            sh = size.bit_length() - 1
            sel = ((ii >> (sh + 1)) == (jj >> (sh + 1))) & ((ii >> sh) != (jj >> sh))
            tb = [t[q].astype(BF16) for q in G]
            mid = [_dot(jnp.where(sel, a_ab[q], 0.0).astype(BF16), tb[q]) for q in G]
            t = [t[q] + _dot(tb[q], mid[q].astype(BF16)) for q in G]
            size *= 2
        return dict(r=r_all, k=k_all, v=v_all, g=g_all, pend=[pc_all[C - 1:C, sl] for sl in sls],
                    ar=ar, bk=bk, a_kr=a_kr, a_rb=a_rb, t=t)

    def state_dependent(c, d, s):
        v_st = [_stack_heads(d["v"][:, sl], lane_head) for sl in sls]
        from_state = [_dot(d["ar"][q], s[q].astype(BF16), NT) for q in G]
        from_v = [_dot(d["a_kr"][q], v_st[q].astype(BF16)) for q in G]
        u = [_dot1(d["t"][q], from_state[q][:R] + from_v[q][:R]) for q in G]
        y_st = [from_state[q][R:] + from_v[q][R:] + _dot(d["a_rb"][q], u[q].astype(BF16)) for q in G]
        s_new = [s[q] * d["pend"][q]
                 + _dot1(jnp.concatenate([u[q], v_st[q]], axis=0).T, d["bk"][q] * d["pend"][q]) for q in G]
        for q in G:
            sl = sls[q]
            y = y_st[q][0:C]
            for h in range(1, HPG):
                y = y + y_st[q][h * C:(h + 1) * C]
            mean = _dot_exact_rhs(y, ones) * (1.0 / RW_HEAD)
            yc = y - mean
            var = _dot_exact_rhs(yc * yc, ones) * (1.0 / RW_HEAD)
            yn = yc * lax.rsqrt(var + GN_EPS) * lnw_ref[:, sl] + lnb_ref[:, sl]
            bonus = _dot_exact_rhs(d["r"][:, sl] * d["k"][:, sl] * rk_ref[:, sl], ones) * d["v"][:, sl]
            y_ref[c * C:(c + 1) * C, sl] = ((yn + bonus) * d["g"][:, sl]).astype(y_ref.dtype)
        return s_new

    parts = [state_independent(c) for c in range(nc)]
    s = [s_scr[q] for q in G]
    for c in range(nc):
        s = state_dependent(c, parts[c], s)
    for q in G:
        s_scr[q] = s[q]

    @pl.when(j == pl.num_programs(1) - 1)
    def _():
        for q in G:
            for h in range(HPG):
                hs = slice(h * RW_HEAD, (h + 1) * RW_HEAD)
                st_ref[0, q * HPG + h] = s_scr[q, hs, hs]
        sh_ref[0] = carry_ref[...]


def _rwkv_mix(p_rw, row_off, B, T, tm, C, prev, s0, mu, w0, wup_pad, a0, aup_pad, gup, k_k, k_a, ln_w, ln_b, r_k):
    nt = T // tm
    off = row_off // tm
    t_out = max(tm, C)
    assert row_off % tm == 0 and T % tm == 0 and (tm <= C or tm % C == 0)
    row = lambda n: pl.BlockSpec((1, n), lambda b, j: (0, 0))
    full = lambda a: pl.BlockSpec(a.shape, lambda b, j: (0, 0))
    slab, s0 = s0
    st_spec = pl.BlockSpec((None, 1, NG, GW, GW), lambda b, j: (slab, b, 0, 0, 0))
    return pl.pallas_call(
        functools.partial(_rwkv_mix_kernel, C=C),
        name="rwkv_mix",
        out_shape=[jax.ShapeDtypeStruct((B * nt * t_out, RW_WIDTH), BF16),
                   jax.ShapeDtypeStruct((B, RW_HEADS, RW_HEAD, RW_HEAD), F32),
                   jax.ShapeDtypeStruct((B, 1, RW_COLS), F32)],
        grid=(B, nt),
        in_specs=[pl.BlockSpec((tm, RW_COLS), lambda b, j: (off + b * nt + j, 0)),
                  pl.BlockSpec((1, 1, RW_COLS), lambda b, j: (b, 0, 0)),
                  row(RW_COLS), row(RW_WIDTH), full(wup_pad), row(RW_WIDTH), full(aup_pad), full(gup),
                  row(RW_WIDTH), row(RW_WIDTH), st_spec, row(RW_WIDTH), row(RW_WIDTH), row(RW_WIDTH)],
        out_specs=[pl.BlockSpec((t_out, RW_WIDTH), lambda b, j: (b * nt + j, 0)),
                   pl.BlockSpec((1, RW_HEADS, RW_HEAD, RW_HEAD), lambda b, j: (b, 0, 0, 0)),
                   pl.BlockSpec((1, 1, RW_COLS), lambda b, j: (b, 0, 0))],
        scratch_shapes=[pltpu.VMEM((NG, GW, GW), F32), pltpu.VMEM((1, RW_COLS), F32)],
        compiler_params=_params(("parallel", "arbitrary"), V7X_VMEM_LIMIT),
    )(p_rw, prev.reshape(B, 1, RW_COLS), mu.reshape(1, -1), w0.reshape(1, -1), wup_pad, a0.reshape(1, -1),
      aup_pad, gup, k_k.reshape(1, -1), k_a.reshape(1, -1), s0, ln_w.reshape(1, -1), ln_b.reshape(1, -1),
      r_k.reshape(1, -1))


def _state_to_blockdiag(s):
    B = s.shape[0]
    s = s.reshape(B, NG, HPG, RW_HEAD, RW_HEAD)
    eye = jnp.eye(HPG, dtype=s.dtype)
    bd = s[:, :, :, :, None, :] * eye[None, None, :, None, :, None]
    return bd.reshape(B, NG, GW, GW)


def _gelu(x):
    return 0.5 * x * (1.0 + lax.erf(x * (2.0 ** -0.5)))


def _gmlp_kernel(u_ref, v_ref, lng_ref, lnb_ref, ws_ref, bs_ref, y_ref, vn_ref, *, tm):
    u = _gelu(u_ref[...])
    vf = _gelu(v_ref[...])
    mean = jnp.mean(vf, axis=-1, keepdims=True)
    vc = vf - mean
    var = jnp.mean(vc * vc, axis=-1, keepdims=True)
    vn = vc * lax.rsqrt(var + LN_EPS) * lng_ref[...] + lnb_ref[...]
    vn_ref[...] = vn
    ti = lax.broadcasted_iota(jnp.int32, (GM_CHUNK, GM_CHUNK), 0)
    tj = lax.broadcasted_iota(jnp.int32, (GM_CHUNK, GM_CHUNK), 1)
    causal = ti >= tj
    rows = min(tm, GM_CHUNK)
    for g in range(GM_GROUPS):
        wm = jnp.where(causal, ws_ref[g], 0.0).astype(BF16)
        sl = slice(g * GM_GROUP_DIM, (g + 1) * GM_GROUP_DIM)
        for c in range(max(tm // GM_CHUNK, 1)):
            rs = slice(c * GM_CHUNK, c * GM_CHUNK + rows)
            vg = vn[rs, sl].astype(BF16)
            if rows < GM_CHUNK:
                vg = jnp.concatenate([vg, jnp.zeros((GM_CHUNK - rows, GM_GROUP_DIM), BF16)], axis=0)
            mixed = (_dot(wm, vg) + bs_ref[g])[:rows]
            y_ref[rs, sl] = (u[rs, sl] * mixed).astype(y_ref.dtype)


def _gmlp(p_gm, row_off, rows, tm, ln_g, ln_b, ws, layer, bs_b, y_dtype):
    off = row_off // tm
    assert row_off % tm == 0 and rows % tm == 0
    row = pl.BlockSpec((1, GM_WIDTH), lambda i: (0, 0))
    ws_spec = pl.BlockSpec((None, GM_GROUPS, GM_CHUNK, GM_CHUNK), lambda i: (layer, 0, 0, 0))
    return pl.pallas_call(
        functools.partial(_gmlp_kernel, tm=tm),
        name="gmlp",
        out_shape=[jax.ShapeDtypeStruct((rows, GM_WIDTH), y_dtype), jax.ShapeDtypeStruct((rows, GM_WIDTH), F32)],
        grid=(rows // tm,),
        in_specs=[pl.BlockSpec((tm, GM_WIDTH), lambda i: (off + i, 0)),
                  pl.BlockSpec((tm, GM_WIDTH), lambda i: (off + i, 1)),
                  row, row, ws_spec, ws_spec],
        out_specs=[pl.BlockSpec((tm, GM_WIDTH), lambda i: (i, 0))] * 2,
        compiler_params=_params(("parallel",), V7X_VMEM_LIMIT),
    )(p_gm, p_gm, ln_g.reshape(1, -1), ln_b.reshape(1, -1), ws, bs_b)


def _attn_prep_kernel(q_ref, k_ref, cos_ref, sin_ref, qg_ref, kg_ref, qo_ref, ko_ref):
    cos = cos_ref[...]
    sin = sin_ref[...]

    ones = jnp.ones((ATT_HEAD, ATT_HEAD), BF16)

    def norm_rope(x, gain):
        ssq = _dot((x * x).astype(BF16), ones)
        y = x * lax.rsqrt(ssq * (1.0 / ATT_HEAD) + NORM_EPS) * gain
        return y * cos + pltpu.roll(y, ATT_HEAD // 2, axis=1) * sin

    for h in range(N_DIL * ATT_GROUP_HEADS):
        sl = slice(h * ATT_HEAD, (h + 1) * ATT_HEAD)
        qo_ref[:, sl] = norm_rope(q_ref[:, sl], qg_ref[...]).astype(qo_ref.dtype)
        ko_ref[:, sl] = norm_rope(k_ref[:, sl], kg_ref[...]).astype(ko_ref.dtype)


def _attn_prep(p_at, cos, sin, q_gain, k_gain, tm):
    M = p_at.shape[0]
    blk = lambda c: pl.BlockSpec((tm, ATT_WIDTH), lambda i: (i, c))
    tab = pl.BlockSpec((tm, ATT_HEAD), lambda i: (i, 0))
    gain = pl.BlockSpec((1, ATT_HEAD), lambda i: (0, 0))
    return pl.pallas_call(
        _attn_prep_kernel,
        name="attn_prep",
        out_shape=[jax.ShapeDtypeStruct((M, ATT_WIDTH), F32), jax.ShapeDtypeStruct((M, ATT_WIDTH), F32)],
        grid=(M // tm,),
        in_specs=[blk(0), blk(1), tab, tab, gain, gain],
        out_specs=[pl.BlockSpec((tm, ATT_WIDTH), lambda i: (i, 0))] * 2,
        compiler_params=_params(("parallel",), V7X_VMEM_LIMIT),
    )(p_at, p_at, cos, sin, q_gain.reshape(1, -1), k_gain.reshape(1, -1))


def _attn_prompt_kernel(q0, q1, q2, k0, k1, k2, v0, v1, v2, y_ref, o_scr, lse_scr, *, T):
    scale = ATT_HEAD ** -0.5
    qi = lax.broadcasted_iota(jnp.int32, (ATT_BLK, ATT_BLK), 0)
    kj = lax.broadcasted_iota(jnp.int32, (ATT_BLK, ATT_BLK), 1)
    refs = ((q0, k0, v0), (q1, k1, v1), (q2, k2, v2))
    for g, (window, dil) in enumerate(DIL_PAIRS):
        q_ref, k_ref, v_ref = refs[g]
        nb = T // dil // ATT_BLK

        def rows(r, n, dil=dil):
            if dil == 1:
                return pl.ds(n * ATT_BLK, ATT_BLK)
            return pl.ds(r + n * ATT_BLK * dil, ATT_BLK, stride=dil)

        blocks = [(rows(r, n), rows(r, n - 1) if n > 0 else None) for r in range(dil) for n in range(nb)]
        for b0 in range(0, len(blocks), ATTN_BATCH):
            batch = blocks[b0:b0 + ATTN_BATCH]
            q = [q_ref[rs, :].astype(BF16) for rs, _ in batch]
            s_c = [jnp.where(kj <= qi, _dot(q[i], k_ref[rs, :].astype(BF16), NT) * scale, NEG_INF)
                   for i, (rs, _) in enumerate(batch)]
            s_p = [None if ps_ is None else
                   jnp.where(kj >= qi, _dot(q[i], k_ref[ps_, :].astype(BF16), NT) * scale, NEG_INF)
                   for i, (_, ps_) in enumerate(batch)]
            m = [jnp.max(s, axis=-1, keepdims=True) for s in s_c]
            m = [mc if sp is None else jnp.maximum(mc, jnp.max(sp, axis=-1, keepdims=True))
                 for mc, sp in zip(m, s_p)]
            e_c = [jnp.exp(s - mm) for s, mm in zip(s_c, m)]
            e_p = [None if sp is None else jnp.exp(sp - mm) for sp, mm in zip(s_p, m)]
            den = [jnp.sum(e, axis=-1, keepdims=True) for e in e_c]
            den = [d if e is None else d + jnp.sum(e, axis=-1, keepdims=True) for d, e in zip(den, e_p)]
            acc = [_dot(e.astype(BF16), v_ref[rs, :].astype(BF16)) for e, (rs, _) in zip(e_c, batch)]
            acc = [a if e is None else a + _dot(e.astype(BF16), v_ref[ps_, :].astype(BF16))
                   for a, e, (_, ps_) in zip(acc, e_p, batch)]
            for i, (rs, _) in enumerate(batch):
                o_scr[g, rs, :] = acc[i] / den[i]
                lse_scr[g, rs, :] = jnp.broadcast_to(m[i] + jnp.log(den[i]), (ATT_BLK, ATT_HEAD))
    a, b, c = lse_scr[0], lse_scr[1], lse_scr[2]
    m = jnp.maximum(jnp.maximum(a, b), c)
    wa, wb, wc = jnp.exp(a - m), jnp.exp(b - m), jnp.exp(c - m)
    y = (wa * o_scr[0] + wb * o_scr[1] + wc * o_scr[2]) / (wa + wb + wc)
    y_ref[...] = y.astype(y_ref.dtype)


def _attn_prompt(q_rot, k_rot, p_at, B, T):
    nh = ATT_GROUP_HEADS
    col = lambda g, base: pl.BlockSpec((T, ATT_HEAD), lambda b, h: (b, base + g * nh + h))
    vbase = 2 * ATT_WIDTH // ATT_HEAD
    return pl.pallas_call(
        functools.partial(_attn_prompt_kernel, T=T),
        name="attn_prompt",
        out_shape=jax.ShapeDtypeStruct((B * T, ATT_OUT), BF16),
        grid=(B, nh),
        in_specs=[col(g, 0) for g in range(N_DIL)] * 2 + [col(g, vbase) for g in range(N_DIL)],
        out_specs=pl.BlockSpec((T, ATT_HEAD), lambda b, h: (b, h)),
        scratch_shapes=[pltpu.VMEM((N_DIL, T, ATT_HEAD), F32)] * 2,
        compiler_params=_params(("parallel", "parallel"), V7X_VMEM_LIMIT),
    )(q_rot, q_rot, q_rot, k_rot, k_rot, k_rot, p_at, p_at, p_at)


def _attn_sample_kernel(q_ref, k_ref, v_ref, c0_ref, c1_ref, c2_ref, y_ref, *, T):
    scale = ATT_HEAD ** -0.5
    caches = (c0_ref, c1_ref, c2_ref)
    zpad = jnp.zeros((LANES - T, ATT_HEAD), F32)
    outs = [[None] * N_DIL for _ in range(ATT_GROUP_HEADS)]
    lses = [[None] * N_DIL for _ in range(ATT_GROUP_HEADS)]
    for g, (window, dil) in enumerate(DIL_PAIRS):
        cref = caches[g]
        wb = cref.shape[1]
        nkeys = wb + LANES
        t = lax.broadcasted_iota(jnp.int32, (T, nkeys), 0)
        jrow = lax.broadcasted_iota(jnp.int32, (T, nkeys), 1)
        dist = wb + t - jrow
        valid = (dist >= 0) & (dist <= window) & ((dist & (dil - 1)) == 0)
        for h in range(ATT_GROUP_HEADS):
            sl = slice((g * ATT_GROUP_HEADS + h) * ATT_HEAD, (g * ATT_GROUP_HEADS + h + 1) * ATT_HEAD)
            q = q_ref[:, sl].astype(BF16)
            kcat = jnp.concatenate([cref[0, :, 0, h, :], k_ref[:, sl], zpad], axis=0).astype(BF16)
            vcat = jnp.concatenate([cref[0, :, 1, h, :], v_ref[:, sl], zpad], axis=0).astype(BF16)
            s = jnp.where(valid, _dot(q, kcat, NT) * scale, NEG_INF)
            m = jnp.max(s, axis=-1, keepdims=True)
            lse = m + jnp.log(jnp.sum(jnp.exp(s - m), axis=-1, keepdims=True))
            outs[h][g] = _dot(jnp.exp(s - lse).astype(BF16), vcat)
            lses[h][g] = lse
    for h in range(ATT_GROUP_HEADS):
        m = jnp.maximum(jnp.maximum(lses[h][0], lses[h][1]), lses[h][2])
        w = [jnp.exp(l - m) for l in lses[h]]
        y = (w[0] * outs[h][0] + w[1] * outs[h][1] + w[2] * outs[h][2]) / (w[0] + w[1] + w[2])
        y_ref[:, h * ATT_HEAD:(h + 1) * ATT_HEAD] = y.astype(y_ref.dtype)


def _attn_sample(q_rot, k_rot, p_at, row_off, B, T, caches, layer):
    off = row_off // T
    blk = lambda c: pl.BlockSpec((T, ATT_WIDTH), lambda b: (off + b, c))
    cspec = lambda a: pl.BlockSpec((None, 1) + a.shape[2:], lambda b: (layer, b, 0, 0, 0, 0))
    return pl.pallas_call(
        functools.partial(_attn_sample_kernel, T=T),
        name="attn_sample",
        out_shape=jax.ShapeDtypeStruct((B * T, ATT_OUT), F32),
        grid=(B,),
        in_specs=[blk(0), blk(0), blk(2)] + [cspec(c) for c in caches],
        out_specs=pl.BlockSpec((T, ATT_OUT), lambda b: (b, 0)),
        compiler_params=_params(("parallel",), V7X_VMEM_LIMIT),
    )(q_rot, k_rot, p_at, *caches)


def _branch_kernel(yr_ref, yg_ref, ya_ref, wr_ref, wg_ref, wa_ref, g0_ref, g1_ref, g2_ref, o_ref):
    acc = _sigmoid(g0_ref[...]) * _dot(yr_ref[...], wr_ref[...].astype(BF16))
    acc = acc + _sigmoid(g1_ref[...]) * _dot(yg_ref[...], wg_ref[...].astype(BF16))
    acc = acc + _sigmoid(g2_ref[...]) * _dot(ya_ref[...], wa_ref[...].astype(BF16))
    o_ref[...] = acc.astype(o_ref.dtype)


def _branch(y_rw, y_gm, y_at, w_rw, w_gm, w_at, layer, p_gate, tm, tn):
    M = y_rw.shape[0]
    nb = D_MODEL // tn
    lhs = lambda kdim: pl.BlockSpec((tm, kdim), lambda i, j: (i, 0))
    rhs = lambda kdim: pl.BlockSpec((None, kdim, tn), lambda i, j: (layer, 0, j))
    gate = lambda br: pl.BlockSpec((tm, tn), lambda i, j: (i, br * nb + j))
    return pl.pallas_call(
        _branch_kernel,
        name="branch_merge",
        out_shape=jax.ShapeDtypeStruct((M, D_MODEL), BF16),
        grid=(M // tm, nb),
        in_specs=[lhs(RW_WIDTH), lhs(GM_WIDTH), lhs(ATT_OUT), rhs(RW_WIDTH), rhs(GM_WIDTH), rhs(ATT_OUT),
                  gate(0), gate(1), gate(2)],
        out_specs=pl.BlockSpec((tm, tn), lambda i, j: (i, j)),
        compiler_params=_params(("parallel", "parallel"), V7X_VMEM_LIMIT),
    )(y_rw, y_gm, y_at, w_rw, w_gm, w_at, p_gate, p_gate, p_gate)


def _rope_tables(pos):
    half = ATT_HEAD // 2
    inv = ROPE_THETA ** (-jnp.arange(half, dtype=F32) / half)
    ang = pos.astype(F32)[:, None] * inv[None, :]
    cos, sin = jnp.cos(ang), jnp.sin(ang)
    return jnp.concatenate([cos, cos], -1), jnp.concatenate([-sin, sin], -1)


def _kv_rows_kernel(*refs):
    k_ref, v_ref, o_ref = refs[0], refs[1], refs[-1]
    for h in range(ATT_GROUP_HEADS):
        sl = slice(h * ATT_HEAD, (h + 1) * ATT_HEAD)
        o_ref[0, :, 0, h, :] = k_ref[:, sl]
        o_ref[0, :, 1, h, :] = v_ref[:, sl]


def _kv_rows(k_rot, p_at, buf, layer, depth, row0, B, T, keep, g):
    tk = min(keep, 2 * ATT_BLK)
    first = (row0 + T - keep) // tk
    assert (row0 + T - keep) % tk == 0 and T % tk == 0
    rows = lambda c: pl.BlockSpec((tk, ATT_OUT), lambda b, i: (first + b * (T // tk) + i, c))
    shape = (depth, B, keep, 2, ATT_GROUP_HEADS, ATT_HEAD)
    in_specs = [rows(g), rows(2 * ATT_WIDTH // ATT_OUT + g)]
    args = [k_rot, p_at]
    if buf is not None:
        in_specs.append(pl.BlockSpec(memory_space=pl.ANY))
        args.append(buf)
    return pl.pallas_call(
        _kv_rows_kernel,
        name="kv_rows",
        out_shape=jax.ShapeDtypeStruct(shape, F32),
        grid=(B, keep // tk),
        in_specs=in_specs,
        out_specs=pl.BlockSpec((None, 1, tk, 2, ATT_GROUP_HEADS, ATT_HEAD), lambda b, i: (layer, b, i, 0, 0, 0)),
        input_output_aliases={} if buf is None else {2: 0},
        compiler_params=_params(("parallel", "parallel"), V7X_VMEM_LIMIT),
    )(*args)


def kernel(x_prompt, x_sample, cache_kv_w128, cache_kv_w512, cache_kv_w2048, state_rwkv, state_rwkv_shift, norm1, w_in, rw_mu, rw_w0, rw_w_up, rw_a0, rw_a_up, rw_g_up, rw_k_k, rw_k_a, rw_r_k, rw_ln_w, rw_ln_b, gm_ln_g, gm_ln_b, gm_ws, gm_bs, att_q_gain, att_k_gain, w_br_rwkv, w_br_gmlp, w_br_attn, w_out, norm2, w_ff1, w_ff2):
    BP, TP, _ = x_prompt.shape
    BS, TS, _ = x_sample.shape
    depth = w_in.shape[0]
    MP, MS = BP * TP, BS * TS
    M = MP + MS
    TM = ROW_TILE
    assert M % (2 * TM) == 0 and MP % 1024 == 0 and TP % (2 * SCAN_C) == 0 and TS <= SAMPLE_C
    x = jnp.concatenate([x_prompt.reshape(MP, D_MODEL), x_sample.reshape(MS, D_MODEL)], axis=0)

    pos = jnp.concatenate([jnp.tile(jnp.arange(TP), BP), jnp.tile(PAST_LEN + jnp.arange(TS), BS)])
    cos_t, sin_t = _rope_tables(pos)
    caches_all = [cache_kv_w128, cache_kv_w512, cache_kv_w2048]
    zeros_state = jnp.zeros((1, BP, NG, GW, GW), F32)
    sample_states = _state_to_blockdiag(state_rwkv.reshape((depth * BS,) + state_rwkv.shape[2:])).reshape(
        depth, BS, NG, GW, GW)
    zeros_shift = jnp.zeros((BP, RW_COLS), F32)
    bs_b = jnp.broadcast_to(gm_bs[:, :, :, None], (depth, GM_GROUPS, GM_CHUNK, GM_GROUP_DIM))
    c_rw, c_gm, c_at = RW_COLS, RW_COLS + 2 * GM_WIDTH, RW_COLS + 2 * GM_WIDTH + 3 * ATT_WIDTH

    kvp = [None] * N_DIL
    kvs = [None] * N_DIL
    st_p, st_s, sh_p, sh_s, gmv_s = [], [], [], [], []
    for l in range(depth):
        if l == 0:
            xg, ssq = _norm_operand(x, norm1[0], TM)
        mm_in = functools.partial(_matmul, xg, w_in, l, tm=TM * 2, tk=D_MODEL, row_ssq=ssq)
        p_rw = mm_in(col_off=0, n_cols=-(-RW_COLS // 512) * 512, tn=512, name="proj_rwkv")
        p_gm = mm_in(col_off=c_rw, n_cols=2 * GM_WIDTH, tn=512, name="proj_gmlp")
        p_at = mm_in(col_off=c_gm, n_cols=3 * ATT_WIDTH, tn=512, name="proj_attn")
        p_gate = mm_in(col_off=c_at, n_cols=N_BRANCH * D_MODEL, tn=512, name="proj_gates")

        zpad = jnp.zeros((DECAY_LORA, RW_WIDTH), F32)
        wup_pad = jnp.concatenate([rw_w_up[l], zpad], axis=0)
        aup_pad = jnp.concatenate([zpad, rw_a_up[l]], axis=0)
        mix = functools.partial(_rwkv_mix, p_rw, mu=rw_mu[l], w0=rw_w0[l], wup_pad=wup_pad, a0=rw_a0[l],
                                aup_pad=aup_pad, gup=rw_g_up[l], k_k=rw_k_k[l].reshape(-1), k_a=rw_k_a[l].reshape(-1),
                                ln_w=rw_ln_w[l].reshape(-1), ln_b=rw_ln_b[l].reshape(-1), r_k=rw_r_k[l].reshape(-1))
        y_rw_p, sT_p, last_p = mix(row_off=0, B=BP, T=TP, tm=2 * SCAN_C, C=SCAN_C, prev=zeros_shift,
                                   s0=(0, zeros_state))
        y_rw_s, sT_s, last_s = mix(row_off=MP, B=BS, T=TS, tm=TS, C=SAMPLE_C, prev=state_rwkv_shift[l],
                                   s0=(l, sample_states))
        y_rw_s = y_rw_s.reshape(BS, SAMPLE_C, RW_WIDTH)[:, :TS].reshape(MS, RW_WIDTH)
        y_rw = jnp.concatenate([y_rw_p, y_rw_s], axis=0)
        st_p.append(sT_p)
        st_s.append(sT_s)
        sh_p.append(last_p.reshape(BP, RW_COLS))
        sh_s.append(last_s.reshape(BS, RW_COLS))

        y_gm_p, _ = _gmlp(p_gm, 0, MP, 256, gm_ln_g[l], gm_ln_b[l], gm_ws, l, bs_b, BF16)
        y_gm_s, vn_s = _gmlp(p_gm, MP, MS, TS, gm_ln_g[l], gm_ln_b[l], gm_ws, l, bs_b, F32)
        y_gm = jnp.concatenate([y_gm_p, y_gm_s.astype(BF16)], axis=0)
        gmv_s.append(vn_s.reshape(BS, TS, GM_WIDTH))

        q_rot, k_rot = _attn_prep(p_at, cos_t, sin_t, att_q_gain[l], att_k_gain[l], TM)
        for g, (window, dil) in enumerate(DIL_PAIRS):
            keep = min(window, TP)
            kvp[g] = _kv_rows(k_rot, p_at, kvp[g], l, depth, 0, BP, TP, keep, g)
            kvs[g] = _kv_rows(k_rot, p_at, kvs[g], l, depth, MP, BS, TS, TS, g)
        y_at_p = _attn_prompt(q_rot, k_rot, p_at, BP, TP)
        y_at_s = _attn_sample(q_rot, k_rot, p_at, MP, BS, TS, caches_all, l)
        y_at = jnp.concatenate([y_at_p, y_at_s.astype(BF16)], axis=0)

        merged = _branch(y_rw, y_gm, y_at, w_br_rwkv, w_br_gmlp, w_br_attn, l, p_gate, TM * 2, 256)
        x, xg2, ssq2 = _matmul(merged, w_out, l, tm=TM * 2, tn=256, tk=D_MODEL, epilogue="residual", res=x,
                               next_gain=norm2[l], name="out_proj")
        act = _matmul(xg2, w_ff1, l, tm=TM * 2, tn=512, tk=D_MODEL, epilogue="relu2", out_dtype=BF16,
                      row_ssq=ssq2, name="ffn_up")
        ffn_down = functools.partial(_matmul, act, w_ff2, l, tm=TM * 2, tn=1024, tk=1024, epilogue="residual",
                                     res=x, name="ffn_down")
        if l + 1 < depth:
            x, xg, ssq = ffn_down(next_gain=norm1[l + 1])
        else:
            x = ffn_down()

    return (x[:MP].reshape(BP, TP, D_MODEL), x[MP:].reshape(BS, TS, D_MODEL),
            kvp[0], kvp[1], kvp[2], kvs[0], kvs[1], kvs[2],
            jnp.stack(st_p, 0), jnp.stack(st_s, 0), jnp.stack(sh_p, 0), jnp.stack(sh_s, 0),
            jnp.stack(gmv_s, 0))
```
